```python
import jax, jax.numpy as jnp
from jax import lax
import numpy as np

D_MODEL = 1024
BATCH = 1
SEQ = 16384
DEPTH = 1
DEC_BATCH = 128
DEC_SEQ = 1
PAST_LEN = 8192
PAGE_SIZE = 128

D_RNN = D_MODEL // 2
RNN_BLOCKS = 8
RNN_BLOCK = D_RNN // RNN_BLOCKS
CONV_W = 4
RG_C = 8.0
HEAD_DIM = 64
N_HEADS = (D_MODEL - D_RNN) // HEAD_DIM
N_KV = 2
GQA_R = N_HEADS // N_KV
D_Q = N_HEADS * HEAD_DIM
D_KV = N_KV * HEAD_DIM
D_GATE = 3 * N_HEADS
CMP_STRIDE = 16
CMP_LEN = 2 * CMP_STRIDE
CMP_HID = 128
SEL_BLOCK = 64
SEL_RATIO = SEL_BLOCK // CMP_STRIDE
TOP_N = 16
WINDOW = 512
Q_BLOCK = 128
FORCE_SCORE = 1.0e4
OFF_GATE_RG = D_RNN
OFF_Q = 2 * D_RNN
OFF_KV = OFF_Q + D_Q
OFF_NSA_GATE = OFF_KV + 6 * D_KV
D_IN = OFF_NSA_GATE + D_GATE
D_FF = ((8 * D_MODEL // 3 + 255) // 256) * 256
ALPHA = (2.0 * DEPTH) ** 0.25
BETA = (8.0 * DEPTH) ** -0.25
F32 = jnp.float32

kernel_name = 'hymba_rglru_nsa_deepnorm_adaln_step'


def layer_norm(x, g, b, eps=1e-5):
    xf = x.astype(F32)
    mu = jnp.mean(xf, -1, keepdims=True)
    var = jnp.mean(jnp.square(xf - mu), -1, keepdims=True)
    return ((xf - mu) * lax.rsqrt(var + eps)).astype(x.dtype) * g + b


def rms_norm(x, g, eps=1e-6):
    xf = x.astype(F32)
    return (xf * lax.rsqrt(jnp.mean(xf * xf, -1, keepdims=True) + eps)).astype(x.dtype) * g


def masked_softmax(s, mask, axis):
    s = jnp.where(mask, s.astype(F32), -1e30)
    m = jnp.max(s, axis=axis, keepdims=True)
    p = jnp.where(mask, jnp.exp(s - m), 0.0)
    return p / jnp.maximum(jnp.sum(p, axis=axis, keepdims=True), 1e-30)


def alibi_slopes():
    h = jnp.arange(1, N_HEADS + 1, dtype=F32)
    return jnp.exp2(-8.0 * h / N_HEADS).reshape(N_KV, GQA_R)


def causal_conv(xr, buf, w, b):
    T = xr.shape[1]
    xp = jnp.concatenate([buf.astype(xr.dtype), xr], axis=1)
    y = b + sum(xp[:, k:k + T] * w[k] for k in range(CONV_W))
    return y, xp[:, T:]


def rg_lru(xc, h0, w_a, b_a, w_x, b_x, lam):
    B, T, _ = xc.shape
    xb = xc.reshape(B, T, RNN_BLOCKS, RNN_BLOCK)
    r = jax.nn.sigmoid(jnp.einsum('btnc,ncd->btnd', xb, w_a).reshape(B, T, D_RNN) + b_a)
    i = jax.nn.sigmoid(jnp.einsum('btnc,ncd->btnd', xb, w_x).reshape(B, T, D_RNN) + b_x)
    log_a = -RG_C * r.astype(F32) * jax.nn.softplus(-lam.astype(F32))
    a = jnp.exp(log_a)
    b = jnp.sqrt(-jnp.expm1(2.0 * log_a)) * (i * xc).astype(F32)
    b = b.at[:, 0].add(a[:, 0] * h0.astype(F32))

    def combine(lhs, rhs):
        a1, b1 = lhs
        a2, b2 = rhs
        return a1 * a2, a2 * b1 + b2

    _, h = lax.associative_scan(combine, (a, b), axis=1)
    return h.astype(xc.dtype), h[:, -1].astype(xc.dtype)


def compress(k, pe, w1, w2):
    B, L = k.shape[:2]
    L16 = -(-L // CMP_STRIDE) * CMP_STRIDE
    k = jnp.pad(k, ((0, 0), (0, L16 - L), (0, 0), (0, 0)))
    ch = k.reshape(B, L16 // CMP_STRIDE, CMP_STRIDE, N_KV, HEAD_DIM)
    S = CMP_STRIDE
    h = (jnp.einsum('bnlgd,ldh->bngh', ch[:, :-1] + pe[:S, None, :], w1[:S])
         + jnp.einsum('bnlgd,ldh->bngh', ch[:, 1:] + pe[S:, None, :], w1[S:]))
    return jax.nn.silu(h) @ w2


def sel_blocks(k):
    B, L = k.shape[:2]
    ns = -(-L // SEL_BLOCK)
    k = jnp.pad(k, ((0, 0), (0, ns * SEL_BLOCK - L), (0, 0), (0, 0)))
    return k.reshape(B, ns, SEL_BLOCK, N_KV, HEAD_DIM).transpose(0, 3, 1, 2, 4)


def nsa_block(q, q_pos, kc, vc, ks, vs, kw, vw, w_pos, gates, slopes):
    B, Qb = q.shape[:2]
    scale = HEAD_DIM ** -0.5
    qg = q.reshape(B, Qb, N_KV, GQA_R, HEAD_DIM)
    qpf = q_pos.astype(F32)
    sl = slopes[:, :, None, None]
    nc = kc.shape[1]
    c_end = jnp.arange(nc, dtype=jnp.int32) * CMP_STRIDE + (CMP_LEN - 1)
    s = jnp.einsum('bqgrd,bngd->bgrqn', qg, kc).astype(F32) * scale
    s = s - sl * (qpf[:, None] - c_end.astype(F32))
    p_c = masked_softmax(s, c_end[None, :] <= q_pos[:, None], -1)
    o_c = jnp.einsum('bgrqn,bngd->bqgrd', p_c.astype(vc.dtype), vc)
    ns = ks.shape[2]
    imp = jnp.pad(p_c.sum(2), ((0, 0), (0, 0), (0, 0), (CMP_LEN // CMP_STRIDE - 1, SEL_RATIO * ns - nc)))
    p_slc = sum(imp[..., o:o + SEL_RATIO * (ns - 1) + 1:SEL_RATIO] for o in range(SEL_RATIO + 1))
    blk = jnp.arange(ns, dtype=jnp.int32)[None, :]
    cur = (q_pos // SEL_BLOCK)[:, None]
    forced = (blk == 0) | (blk == cur) | (blk == cur - 1)
    score = jnp.where(blk * SEL_BLOCK <= q_pos[:, None], jnp.where(forced, FORCE_SCORE, p_slc), -1.0)
    top_v, top_i = lax.top_k(score, min(TOP_N, ns))
    gather = jax.vmap(jax.vmap(lambda kb, ib: kb[ib]))
    k_sel = gather(ks, top_i)
    v_sel = gather(vs, top_i)
    k_pos = top_i[..., None] * SEL_BLOCK + jnp.arange(SEL_BLOCK, dtype=jnp.int32)
    s = jnp.einsum('bqgrd,bgqnkd->bgrqnk', qg, k_sel).astype(F32) * scale
    s = s - slopes[:, :, None, None, None] * (qpf[:, None, None] - k_pos.astype(F32))[:, :, None]
    m_s = (k_pos <= q_pos[:, None, None]) & (top_v >= 0.0)[..., None]
    p_s = masked_softmax(s, m_s[:, :, None], (-2, -1))
    o_s = jnp.einsum('bgrqnk,bgqnkd->bqgrd', p_s.astype(vs.dtype), v_sel)
    dist = q_pos[:, None] - w_pos[None, :]
    s = jnp.einsum('bqgrd,bkgd->bgrqk', qg, kw).astype(F32) * scale
    s = s - sl * dist.astype(F32)
    m_w = (dist >= 0) & (dist < WINDOW) & (w_pos[None, :] >= 0)
    p_w = masked_softmax(s, m_w, -1)
    o_w = jnp.einsum('bgrqk,bkgd->bqgrd', p_w.astype(vw.dtype), vw)
    g = jax.nn.sigmoid(gates.astype(F32)).astype(q.dtype).reshape(B, Qb, N_KV, GQA_R, 3)
    o = o_c * g[..., 0:1] + o_s * g[..., 1:2] + o_w * g[..., 2:3]
    return o.reshape(B, Qb, D_Q)


def setup_inputs(seed: int = 0) -> dict:
    key = jax.random.key(seed)
    nxt = iter(list(jax.random.split(key, 40)))

    def nrm(shape, s=1.0):
        return jax.random.normal(next(nxt), shape, F32) * s

    L = DEPTH
    n_pages = PAST_LEN // PAGE_SIZE
    n_pool = (DEC_BATCH * n_pages * 5 + 3) // 4
    w_buf = min(WINDOW, PAST_LEN)
    col_scale = np.ones((D_IN,), np.float32)
    for j in (1, 3, 5):
        col_scale[OFF_KV + j * D_KV:OFF_KV + (j + 1) * D_KV] = BETA
    a0 = jax.random.uniform(next(nxt), (L, D_RNN), F32, 0.9, 0.999)
    p0 = a0 ** (1.0 / RG_C)
    rg_lam = jnp.log(p0) - jnp.log1p(-p0)
    page_table = jax.random.permutation(next(nxt), n_pool)[:DEC_BATCH * n_pages]
    page_table = page_table.reshape(DEC_BATCH, n_pages).astype(jnp.int32)
    return {
        'x_prompt': nrm((BATCH, SEQ, D_MODEL)),
        'x_sample': nrm((DEC_BATCH, DEC_SEQ, D_MODEL)),
        'cache_kv': nrm((L, n_pool, PAGE_SIZE, 4, N_KV, HEAD_DIM)),
        'state_win': nrm((L, DEC_BATCH, w_buf, 2, N_KV, HEAD_DIM)),
        'state_conv': nrm((L, DEC_BATCH, CONV_W - 1, D_RNN)),
        'state_h': nrm((L, DEC_BATCH, D_RNN), 0.5),
        'page_table': page_table,
        'c_prompt': nrm((BATCH, D_MODEL)),
        'c_sample': nrm((DEC_BATCH, D_MODEL)),
        'w_ada': nrm((L, D_MODEL, 6 * D_MODEL), D_MODEL ** -0.5),
        'b_ada': nrm((L, 6 * D_MODEL), 0.02),
        'w_in': nrm((L, D_MODEL, D_IN), D_MODEL ** -0.5) * jnp.asarray(col_scale),
        'conv_w': nrm((L, CONV_W, D_RNN), CONV_W ** -0.5),
        'conv_b': nrm((L, D_RNN), 0.02),
        'rg_wa': nrm((L, RNN_BLOCKS, RNN_BLOCK, RNN_BLOCK), RNN_BLOCK ** -0.5),
        'rg_ba': nrm((L, D_RNN), 0.02),
        'rg_wx': nrm((L, RNN_BLOCKS, RNN_BLOCK, RNN_BLOCK), RNN_BLOCK ** -0.5),
        'rg_bx': nrm((L, D_RNN), 0.02),
        'rg_lam': rg_lam,
        'cmp_pe_k': nrm((L, CMP_LEN, HEAD_DIM), 0.02),
        'cmp_w1_k': nrm((L, CMP_LEN, HEAD_DIM, CMP_HID), (CMP_LEN * HEAD_DIM) ** -0.5),
        'cmp_w2_k': nrm((L, CMP_HID, HEAD_DIM), CMP_HID ** -0.5),
        'cmp_pe_v': nrm((L, CMP_LEN, HEAD_DIM), 0.02),
        'cmp_w1_v': nrm((L, CMP_LEN, HEAD_DIM, CMP_HID), (CMP_LEN * HEAD_DIM) ** -0.5),
        'cmp_w2_v': nrm((L, CMP_HID, HEAD_DIM), CMP_HID ** -0.5),
        'norm_rg': 1.0 + nrm((L, D_RNN), 0.02),
        'norm_attn': 1.0 + nrm((L, D_Q), 0.02),
        'w_out': nrm((L, D_MODEL, D_MODEL), BETA * D_MODEL ** -0.5),
        'ln1_g': 1.0 + nrm((L, D_MODEL), 0.02),
        'ln1_b': nrm((L, D_MODEL), 0.02),
        'w_up': nrm((L, D_MODEL, 2 * D_FF), D_MODEL ** -0.5),
        'w_down': nrm((L, D_FF, D_MODEL), BETA * D_FF ** -0.5),
        'ln2_g': 1.0 + nrm((L, D_MODEL), 0.02),
        'ln2_b': nrm((L, D_MODEL), 0.02),
    }


def reference(x_prompt, x_sample, cache_kv, state_win, state_conv, state_h, page_table,
              c_prompt, c_sample, w_ada, b_ada, w_in, conv_w, conv_b, rg_wa, rg_ba, rg_wx,
              rg_bx, rg_lam, cmp_pe_k, cmp_w1_k, cmp_w2_k, cmp_pe_v, cmp_w1_v, cmp_w2_v,
              norm_rg, norm_attn, w_out, ln1_g, ln1_b, w_up, w_down, ln2_g, ln2_b):
    slopes = alibi_slopes()
    past_len = page_table.shape[1] * PAGE_SIZE

    def modulation(c, l):
        m = jax.nn.silu(c) @ w_ada[l] + b_ada[l]
        return jnp.split(m[:, None, :], 6, axis=-1)

    def in_proj(x, shift, scale, l):
        B, T, _ = x.shape
        p = (x * (1.0 + scale) + shift) @ w_in[l]
        xr = p[..., :OFF_GATE_RG]
        gr = p[..., OFF_GATE_RG:OFF_Q]
        q = p[..., OFF_Q:OFF_KV].reshape(B, T, N_HEADS, HEAD_DIM)
        kv = p[..., OFF_KV:OFF_NSA_GATE].reshape(B, T, 6, N_KV, HEAD_DIM)
        gts = p[..., OFF_NSA_GATE:].reshape(B, T, N_HEADS, 3)
        return xr, gr, q, kv, gts

    def rg_branch(xr, gr, conv_buf, h0, l):
        xc, new_buf = causal_conv(xr, conv_buf, conv_w[l], conv_b[l])
        h, h_last = rg_lru(xc, h0, rg_wa[l], rg_ba[l], rg_wx[l], rg_bx[l], rg_lam[l])
        return h * jax.nn.gelu(gr), new_buf, h_last

    def compress_kv(k_rows, v_rows, l):
        kc = compress(k_rows, cmp_pe_k[l], cmp_w1_k[l], cmp_w2_k[l])
        vc = compress(v_rows, cmp_pe_v[l], cmp_w1_v[l], cmp_w2_v[l])
        return kc, vc

    def nsa_prompt(q, kv, gts, l):
        B, T = q.shape[:2]
        kc, vc = compress_kv(kv[:, :, 0], kv[:, :, 1], l)
        ks, vs = sel_blocks(kv[:, :, 2]), sel_blocks(kv[:, :, 3])
        win = jnp.pad(kv[:, :, 4:], ((0, 0), (WINDOW, 0), (0, 0), (0, 0), (0, 0)))

        def one_block(i):
            t0 = i * Q_BLOCK
            qb = lax.dynamic_slice_in_dim(q, t0, Q_BLOCK, axis=1)
            gb = lax.dynamic_slice_in_dim(gts, t0, Q_BLOCK, axis=1)
            wb = lax.dynamic_slice_in_dim(win, t0, WINDOW + Q_BLOCK, axis=1)
            q_pos = t0 + jnp.arange(Q_BLOCK, dtype=jnp.int32)
            w_pos = t0 - WINDOW + jnp.arange(WINDOW + Q_BLOCK, dtype=jnp.int32)
            return nsa_block(qb, q_pos, kc, vc, ks, vs, wb[:, :, 0], wb[:, :, 1], w_pos, gb, slopes)

        out = lax.map(one_block, jnp.arange(T // Q_BLOCK, dtype=jnp.int32))
        out = out.transpose(1, 0, 2, 3).reshape(B, T, D_Q)
        return out, kv[:, :, :4], kv[:, T - min(WINDOW, T):, 4:]

    def nsa_sample(q, kv, gts, cache_l, win_l, l):
        B, S = q.shape[:2]
        past = cache_l[page_table].reshape(B, past_len, 4, N_KV, HEAD_DIM)
        full = jnp.concatenate([past.astype(kv.dtype), kv[:, :, :4]], axis=1)
        kc, vc = compress_kv(full[:, :, 0], full[:, :, 1], l)
        ks, vs = sel_blocks(full[:, :, 2]), sel_blocks(full[:, :, 3])
        wbuf = win_l.shape[1]
        win = jnp.concatenate([win_l.astype(kv.dtype), kv[:, :, 4:]], axis=1)
        q_pos = past_len + jnp.arange(S, dtype=jnp.int32)
        w_pos = past_len - wbuf + jnp.arange(wbuf + S, dtype=jnp.int32)
        out = nsa_block(q, q_pos, kc, vc, ks, vs, win[:, :, 0], win[:, :, 1], w_pos, gts, slopes)
        return out, kv[:, :, :4], win[:, S:]

    def merge_and_ffn(x, rg_y, attn_y, mod, l):
        _, _, gate1, shift2, scale2, gate2 = mod
        h = jnp.concatenate([rms_norm(rg_y, norm_rg[l]), rms_norm(attn_y, norm_attn[l])], axis=-1)
        x = layer_norm(ALPHA * x + gate1 * (h @ w_out[l]), ln1_g[l], ln1_b[l])
        gu = (x * (1.0 + scale2) + shift2) @ w_up[l]
        f = (jax.nn.silu(gu[..., :D_FF]) * gu[..., D_FF:]) @ w_down[l]
        return layer_norm(ALPHA * x + gate2 * f, ln2_g[l], ln2_b[l])

    xp, xs = x_prompt, x_sample
    kvp, kvs, winp, wins, convp, convs, hp, hs = [], [], [], [], [], [], [], []
    for l in range(DEPTH):
        mod = modulation(c_prompt, l)
        xr, gr, q, kv, gts = in_proj(xp, mod[0], mod[1], l)
        zero_buf = jnp.zeros((xp.shape[0], CONV_W - 1, D_RNN), xp.dtype)
        zero_h = jnp.zeros((xp.shape[0], D_RNN), xp.dtype)
        rg_y, cb, hl = rg_branch(xr, gr, zero_buf, zero_h, l)
        attn_y, rows, wb = nsa_prompt(q, kv, gts, l)
        xp = merge_and_ffn(xp, rg_y, attn_y, mod, l)
        kvp.append(rows)
        winp.append(wb)
        convp.append(cb)
        hp.append(hl)
        mod = modulation(c_sample, l)
        xr, gr, q, kv, gts = in_proj(xs, mod[0], mod[1], l)
        rg_y, cb, hl = rg_branch(xr, gr, state_conv[l], state_h[l], l)
        attn_y, rows, wb = nsa_sample(q, kv, gts, cache_kv[l], state_win[l], l)
        xs = merge_and_ffn(xs, rg_y, attn_y, mod, l)
        kvs.append(rows)
        wins.append(wb)
        convs.append(cb)
        hs.append(hl)

    y_prompt, y_sample = xp, xs
    kv_rows_prompt, kv_rows_sample = jnp.stack(kvp), jnp.stack(kvs)
    win_prompt, win_sample = jnp.stack(winp), jnp.stack(wins)
    conv_prompt, conv_sample = jnp.stack(convp), jnp.stack(convs)
    h_prompt, h_sample = jnp.stack(hp), jnp.stack(hs)
    return (y_prompt, y_sample, kv_rows_prompt, kv_rows_sample, win_prompt, win_sample,
            conv_prompt, conv_sample, h_prompt, h_sample)
```

```python
import functools

import numpy as np
import jax
import jax.numpy as jnp
from jax import lax
from jax.experimental import pallas as pl
from jax.experimental.pallas import tpu as pltpu

F32 = jnp.float32
BF16 = jnp.bfloat16

D_MODEL = 1024
D_RNN = D_MODEL // 2
RNN_BLOCKS = 8
RNN_BLOCK = D_RNN // RNN_BLOCKS
CONV_W = 4
RG_C = 8.0
HEAD_DIM = 64
N_HEADS = (D_MODEL - D_RNN) // HEAD_DIM
N_KV = 2
GQA_R = N_HEADS // N_KV
D_Q = N_HEADS * HEAD_DIM
D_KV = N_KV * HEAD_DIM
D_GATE = 3 * N_HEADS
CMP_STRIDE = 16
CMP_LEN = 2 * CMP_STRIDE
CMP_HID = 128
SEL_BLOCK = 64
SEL_RATIO = SEL_BLOCK // CMP_STRIDE
TOP_N = 16
WINDOW = 512
Q_BLOCK = 128
FORCE_SCORE = 1.0e4
PAGE_SIZE = 128
D_FF = ((8 * D_MODEL // 3 + 255) // 256) * 256
SCALE = HEAD_DIM ** -0.5

OFF_Q = 2 * D_RNN
OFF_KV = OFF_Q + D_Q
OFF_GATE = OFF_KV + 6 * D_KV
GATE_PAD = 128
D_IN_PAD = OFF_GATE + GATE_PAD

NEG = -1e30
LANE = 128
VMEM_LIMIT = 56 * 1024 * 1024


def _sigmoid(x):
    return 1.0 / (1.0 + jnp.exp(-x))


def _gelu_tanh(x):
    c = np.float32(np.sqrt(2.0 / np.pi))
    return 0.5 * x * (1.0 + jnp.tanh(c * (x + np.float32(0.044715) * (x * x * x))))


def _softplus(x):
    return jnp.maximum(x, 0.0) + jnp.log1p(jnp.exp(-jnp.abs(x)))


def _pow2_neg(e_int):
    return lax.bitcast_convert_type((127 - e_int) << 23, F32)


def _masked_softmax(s, mask):
    s = jnp.where(mask, s, NEG)
    m = jnp.max(s, axis=-1, keepdims=True)
    p = jnp.where(mask, jnp.exp(s - m), 0.0)
    return p / jnp.maximum(jnp.sum(p, axis=-1, keepdims=True), 1e-30)


def _split3_dot(x, m_bf16):
    hi = x.astype(BF16)
    r1 = x - hi.astype(F32)
    mid = r1.astype(BF16)
    lo = (r1 - mid.astype(F32)).astype(BF16)
    d = functools.partial(jnp.dot, preferred_element_type=F32)
    return d(hi, m_bf16) + d(mid, m_bf16) + d(lo, m_bf16)


def _topk_select(score, blkf, k):
    work = score
    idxs, vals = [], []
    for _ in range(k):
        m = jnp.max(work, axis=1, keepdims=True)
        idx = jnp.min(jnp.where(work == m, blkf, 1e9), axis=1, keepdims=True)
        work = jnp.where(blkf == idx, -2.0, work)
        idxs.append(idx)
        vals.append(m)
    return idxs, vals


def _mod_kernel(c_ref, w_ref, b_ref, o_ref):
    c = c_ref[...]
    a = (c * _sigmoid(c)).astype(BF16)
    o_ref[...] = jnp.dot(a, w_ref[...].astype(BF16), preferred_element_type=F32) + b_ref[...]


def _modulation(c_all, w_ada, b_ada):
    r, n = c_all.shape[0], w_ada.shape[1]
    tn = 512
    return pl.pallas_call(
        _mod_kernel,
        grid=(n // tn,),
        in_specs=[pl.BlockSpec((r, D_MODEL), lambda j: (0, 0)),
                  pl.BlockSpec((D_MODEL, tn), lambda j: (0, j)),
                  pl.BlockSpec((1, tn), lambda j: (0, j))],
        out_specs=pl.BlockSpec((r, tn), lambda j: (0, j)),
        out_shape=jax.ShapeDtypeStruct((r, n), F32),
        name="adaln_mod",
    )(c_all, w_ada, b_ada.reshape(1, n))


def _inproj_kernel(x_ref, sh_ref, sc_ref, w_ref, xg_ref, q_ref, rows_ref, gt_ref, vs_ref, vw_ref,
                   kvt_ref, kst_ref, kwt_ref, *, n_row_cols):
    u = x_ref[...] * (1.0 + sc_ref[...]) + sh_ref[...]
    p = jnp.dot(u.astype(BF16), w_ref[...], preferred_element_type=F32)
    xg_ref[...] = p[:, 0:OFF_Q]
    q_ref[...] = (p[:, OFF_Q:OFF_KV] * SCALE).astype(BF16)
    kv = p[:, OFF_KV:OFF_GATE]
    rows_ref[...] = kv[:, 0:n_row_cols]
    gt_ref[...] = p[:, OFF_GATE:D_IN_PAD]
    vs_ref[...] = kv[:, 3 * D_KV:4 * D_KV].astype(BF16)
    vw_ref[...] = kv[:, 5 * D_KV:6 * D_KV].astype(BF16)
    kvt = kv.T
    kvt_ref[...] = kvt
    kst_ref[...] = kvt[2 * D_KV:3 * D_KV].astype(BF16)
    kwt_ref[...] = kvt[4 * D_KV:5 * D_KV].astype(BF16)


def _in_proj(x2d, shift, scale, w_bf16, tm, n_row_cols):
    r = x2d.shape[0]
    rm = shift.shape[0]
    mod_spec = (pl.BlockSpec((1, D_MODEL), lambda i: (0, 0)) if rm == 1
                else pl.BlockSpec((tm, D_MODEL), lambda i: (i, 0)))
    row = lambda w: pl.BlockSpec((tm, w), lambda i: (i, 0))
    col = lambda h: pl.BlockSpec((h, tm), lambda i: (0, i))
    return pl.pallas_call(
        functools.partial(_inproj_kernel, n_row_cols=n_row_cols),
        grid=(r // tm,),
        in_specs=[row(D_MODEL), mod_spec, mod_spec,
                  pl.BlockSpec((D_MODEL, D_IN_PAD), lambda i: (0, 0))],
        out_specs=[row(OFF_Q), row(D_Q), row(n_row_cols), row(GATE_PAD), row(D_KV), row(D_KV),
                   col(6 * D_KV), col(D_KV), col(D_KV)],
        out_shape=[jax.ShapeDtypeStruct((r, OFF_Q), F32),
                   jax.ShapeDtypeStruct((r, D_Q), BF16),
                   jax.ShapeDtypeStruct((r, n_row_cols), F32),
                   jax.ShapeDtypeStruct((r, GATE_PAD), F32),
                   jax.ShapeDtypeStruct((r, D_KV), BF16),
                   jax.ShapeDtypeStruct((r, D_KV), BF16),
                   jax.ShapeDtypeStruct((6 * D_KV, r), F32),
                   jax.ShapeDtypeStruct((D_KV, r), BF16),
                   jax.ShapeDtypeStruct((D_KV, r), BF16)],
        compiler_params=pltpu.CompilerParams(dimension_semantics=("arbitrary",),
                                             vmem_limit_bytes=VMEM_LIMIT),
        name="in_proj",
    )(x2d, shift, scale, w_bf16)


def _rg_gates(xc, wrg_ref, ba_ref, bx_ref, lam_ref):
    g = jnp.dot(xc.astype(BF16), wrg_ref[...], preferred_element_type=F32)
    r = _sigmoid(g[:, 0:D_RNN] + ba_ref[...])
    ig = _sigmoid(g[:, D_RNN:2 * D_RNN] + bx_ref[...])
    log_a = -RG_C * r * _softplus(-lam_ref[...])
    a = jnp.exp(log_a)
    b = jnp.sqrt(-jnp.tanh(log_a) * (a * a + 1.0)) * (ig * xc)
    return a, b


def _rg_out(h, gr, nrm_ref):
    y = h * _gelu_tanh(gr)
    return (y * lax.rsqrt(jnp.mean(y * y, axis=-1, keepdims=True) + 1e-6)) * nrm_ref[...]


def _rg_prompt_kernel(xg_ref, cw_ref, cb_ref, wrg_ref, ba_ref, bx_ref, lam_ref, nrm_ref,
                      y_ref, tail_ref, hl_ref, xp_s, a_s, b_s, h_s, hc_s, *, tc):
    i = pl.program_id(0)

    @pl.when(i == 0)
    def _():
        xp_s[0:8, :] = jnp.zeros((8, D_RNN), F32)
        hc_s[...] = jnp.zeros((8, D_RNN), F32)

    xr = xg_ref[:, 0:D_RNN]
    gr = xg_ref[:, D_RNN:2 * D_RNN]
    xp_s[8:8 + tc, :] = xr
    cw = cw_ref[...]
    xc = (cb_ref[...] + cw[0:1] * xp_s[5:5 + tc, :] + cw[1:2] * xp_s[6:6 + tc, :]
          + cw[2:3] * xp_s[7:7 + tc, :] + cw[3:4] * xr)
    a, b = _rg_gates(xc, wrg_ref, ba_ref, bx_ref, lam_ref)
    a_s[...] = a
    b_s[...] = b
    rowi = lax.broadcasted_iota(jnp.int32, (8, D_RNN), 0)

    def tile(gi, hc):
        r0 = pl.multiple_of(gi * 8, 8)
        at = a_s[pl.ds(r0, 8), :]
        bt = b_s[pl.ds(r0, 8), :]
        for d in (1, 2, 4):
            keep = rowi >= d
            a_sh = jnp.where(keep, pltpu.roll(at, d, 0), 1.0)
            b_sh = jnp.where(keep, pltpu.roll(bt, d, 0), 0.0)
            bt = at * b_sh + bt
            at = at * a_sh
        h = at * hc + bt
        h_s[pl.ds(r0, 8), :] = h
        return h[7:8, :]

    hc = lax.fori_loop(0, tc // 8, tile, hc_s[0:1, :])
    hc_s[0:1, :] = hc
    xp_s[0:8, :] = xr[tc - 8:tc]
    y_ref[...] = _rg_out(h_s[...], gr, nrm_ref).astype(BF16)
    tail_ref[...] = xr[tc - 8:tc]
    hl_ref[...] = jnp.broadcast_to(hc, (8, D_RNN))


def _rg_prompt(xg, cw, cb, wrg, ba, bx, lam, nrm, tc):
    t = xg.shape[0]
    vec = pl.BlockSpec((1, D_RNN), lambda i: (0, 0))
    return pl.pallas_call(
        functools.partial(_rg_prompt_kernel, tc=tc),
        grid=(t // tc,),
        in_specs=[pl.BlockSpec((tc, 2 * D_RNN), lambda i: (i, 0)),
                  pl.BlockSpec((CONV_W, D_RNN), lambda i: (0, 0)), vec,
                  pl.BlockSpec((D_RNN, 2 * D_RNN), lambda i: (0, 0)), vec, vec, vec, vec],
        out_specs=[pl.BlockSpec((tc, D_RNN), lambda i: (i, 0)),
                   pl.BlockSpec((8, D_RNN), lambda i: (0, 0)),
                   pl.BlockSpec((8, D_RNN), lambda i: (0, 0))],
        out_shape=[jax.ShapeDtypeStruct((t, D_RNN), BF16),
                   jax.ShapeDtypeStruct((8, D_RNN), F32),
                   jax.ShapeDtypeStruct((8, D_RNN), F32)],
        scratch_shapes=[pltpu.VMEM((tc + 8, D_RNN), F32), pltpu.VMEM((tc, D_RNN), F32),
                        pltpu.VMEM((tc, D_RNN), F32), pltpu.VMEM((tc, D_RNN), F32),
                        pltpu.VMEM((8, D_RNN), F32)],
        compiler_params=pltpu.CompilerParams(dimension_semantics=("arbitrary",)),
        name="rg_prompt",
    )(xg, cw, cb, wrg, ba, bx, lam, nrm)


def _rg_sample_kernel(xg_ref, c0_ref, c1_ref, c2_ref, h0_ref, cw_ref, cb_ref, wrg_ref, ba_ref,
                      bx_ref, lam_ref, nrm_ref, y_ref, h_ref):
    xr = xg_ref[:, 0:D_RNN]
    gr = xg_ref[:, D_RNN:2 * D_RNN]
    cw = cw_ref[...]
    xc = (cb_ref[...] + cw[0:1] * c0_ref[...] + cw[1:2] * c1_ref[...] + cw[2:3] * c2_ref[...]
          + cw[3:4] * xr)
    a, b = _rg_gates(xc, wrg_ref, ba_ref, bx_ref, lam_ref)
    h = a * h0_ref[...] + b
    h_ref[...] = h
    y_ref[...] = _rg_out(h, gr, nrm_ref).astype(BF16)


def _rg_sample(xg, c0, c1, c2, h0, cw, cb, wrg, ba, bx, lam, nrm):
    b = xg.shape[0]
    return pl.pallas_call(
        _rg_sample_kernel,
        out_shape=[jax.ShapeDtypeStruct((b, D_RNN), BF16), jax.ShapeDtypeStruct((b, D_RNN), F32)],
        name="rg_sample",
    )(xg, c0, c1, c2, h0, cw, cb, wrg, ba, bx, lam, nrm)


def _compress_chunk(x_s, base, nblk, pek_ref, pev_ref, wk_ref, wv_ref, w2k_ref, w2v_ref):
    outs = []
    for t, (pe_ref, w1_ref, w2_ref) in enumerate(((pek_ref, wk_ref, w2k_ref),
                                                   (pev_ref, wv_ref, w2v_ref))):
        pieces = []
        for l in range(CMP_LEN):
            xl = x_s[t, pl.ds(base + l, nblk, stride=CMP_STRIDE), :]
            pieces.append((xl + pe_ref[:, l * D_KV:(l + 1) * D_KV]).astype(BF16))
        xcat = jnp.concatenate(pieces, axis=1)
        h = jnp.dot(xcat, w1_ref[...], preferred_element_type=F32)
        h = h * _sigmoid(h)
        outs.append(jnp.dot(h.astype(BF16), w2_ref[...], preferred_element_type=F32))
    return outs


CMP_CHUNK = 128
CMP_ROWS = CMP_CHUNK * CMP_STRIDE


def _cmp_prompt_kernel(x_ref, nxt_ref, pek_ref, pev_ref, wk_ref, wv_ref, w2k_ref, w2v_ref,
                       kct_ref, vc_ref, x_s):
    for t in range(2):
        x_s[t, 0:CMP_ROWS, :] = x_ref[:, t * D_KV:(t + 1) * D_KV]
        x_s[t, CMP_ROWS:CMP_ROWS + CMP_STRIDE, :] = nxt_ref[:, t * D_KV:(t + 1) * D_KV]
    kc, vc = _compress_chunk(x_s, 0, CMP_CHUNK, pek_ref, pev_ref, wk_ref, wv_ref, w2k_ref, w2v_ref)
    kct_ref[...] = kc.T.astype(BF16)
    vc_ref[...] = vc.astype(BF16)


def _cmp_prompt(rows, pek, pev, wk, wv, w2k, w2v):
    t = rows.shape[0]
    n_steps = t // CMP_ROWS
    last16 = t // CMP_STRIDE - 1
    full = lambda a: pl.BlockSpec(a.shape, lambda i: (0, 0))
    return pl.pallas_call(
        _cmp_prompt_kernel,
        grid=(n_steps,),
        in_specs=[pl.BlockSpec((CMP_ROWS, 2 * D_KV), lambda i: (i, 0)),
                  pl.BlockSpec((CMP_STRIDE, 2 * D_KV),
                               lambda i: (jnp.minimum((i + 1) * CMP_CHUNK, last16), 0)),
                  full(pek), full(pev), full(wk), full(wv), full(w2k), full(w2v)],
        out_specs=[pl.BlockSpec((D_KV, CMP_CHUNK), lambda i: (0, i)),
                   pl.BlockSpec((CMP_CHUNK, D_KV), lambda i: (i, 0))],
        out_shape=[jax.ShapeDtypeStruct((D_KV, t // CMP_STRIDE), BF16),
                   jax.ShapeDtypeStruct((t // CMP_STRIDE, D_KV), BF16)],
        scratch_shapes=[pltpu.VMEM((2, CMP_ROWS + CMP_STRIDE, D_KV), F32)],
        compiler_params=pltpu.CompilerParams(dimension_semantics=("arbitrary",),
                                             vmem_limit_bytes=VMEM_LIMIT),
        name="cmp_prompt",
    )(rows, rows, pek, pev, wk, wv, w2k, w2v)


SEL_CHUNK = 512
WIN_KEYS = WINDOW + Q_BLOCK


def _attn_prompt_kernel(q_ref, gt_ref, kst_ref, vs_ref, kwt_ref, vw_ref, kct_ref, vc_ref,
                        msel_ref, nrm_ref, o_ref):
    i = pl.program_id(0)
    t0 = i * Q_BLOCK
    n_cmp = kct_ref.shape[1]
    n_sel = msel_ref.shape[1]
    q = q_ref[...]
    qpos = t0 + lax.broadcasted_iota(jnp.int32, (Q_BLOCK, 1), 0)
    qpos4 = jnp.concatenate([qpos] * GQA_R, axis=0)
    qposf = qpos.astype(F32)
    qposf4 = qpos4.astype(F32)
    cend = lax.broadcasted_iota(jnp.int32, (1, n_cmp), 1) * CMP_STRIDE + (CMP_LEN - 1)
    blk = lax.broadcasted_iota(jnp.int32, (1, n_sel), 1)
    blkf = blk.astype(F32)
    gates = _sigmoid(gt_ref[...])
    msel = msel_ref[...]
    cur = qpos >> 6
    forced = (blk == 0) | (blk == cur) | (blk == cur - 1)
    in_past = blk * SEL_BLOCK <= qpos
    n_chunks = (t0 + Q_BLOCK + SEL_CHUNK - 1) // SEL_CHUNK
    win0 = pl.multiple_of(jnp.maximum(t0 - WINDOW, 0), LANE)
    wpos = win0 + lax.broadcasted_iota(jnp.int32, (1, WIN_KEYS), 1)
    wdist = qpos4 - wpos
    wmask = (wdist >= 0) & (wdist < WINDOW)
    wdistf = wdist.astype(F32)
    head_out = []
    for g in range(N_KV):
        gs = slice(g * HEAD_DIM, (g + 1) * HEAD_DIM)
        qg = jnp.concatenate(
            [q[:, (g * GQA_R + r) * HEAD_DIM:(g * GQA_R + r + 1) * HEAD_DIM] for r in range(GQA_R)],
            axis=0)
        slopes = [np.float32(2.0 ** -(g * GQA_R + r + 1)) for r in range(GQA_R)]
        slope4 = jnp.concatenate([jnp.full((Q_BLOCK, 1), s, F32) for s in slopes], axis=0)
        s = jnp.dot(qg, kct_ref[gs, :], preferred_element_type=F32)
        s = s - slope4 * (qposf4 - cend.astype(F32))
        p = _masked_softmax(s, cend <= qpos4)
        o_c = jnp.dot(p.astype(BF16), vc_ref[...], preferred_element_type=F32)[:, gs]
        psum = p[0:Q_BLOCK]
        for r in range(1, GQA_R):
            psum = psum + p[r * Q_BLOCK:(r + 1) * Q_BLOCK]
        p_slc = _split3_dot(psum, msel)
        score = jnp.where(in_past, jnp.where(forced, FORCE_SCORE, p_slc), -1.0)
        idxs, vals = _topk_select(score, blkf, min(TOP_N, n_sel))
        sel = jnp.zeros((Q_BLOCK, n_sel), F32)
        for idx, val in zip(idxs, vals):
            sel = jnp.where((blkf == idx) & (val >= 0.0), 1.0, sel)
        selb = sel.astype(BF16)

        def sel_chunk(c, carry, qg=qg, selb=selb, gs=gs, slopes=slopes):
            m, l, acc = carry
            k0 = pl.multiple_of(c * SEL_CHUNK, SEL_CHUNK)
            sc = jnp.dot(qg, kst_ref[gs, pl.ds(k0, SEL_CHUNK)], preferred_element_type=F32)
            kpos = k0 + lax.broadcasted_iota(jnp.int32, (1, SEL_CHUNK), 1)
            blk_of_key = (k0 // SEL_BLOCK
                          + (lax.broadcasted_iota(jnp.int32, (n_sel, SEL_CHUNK), 1) >> 6))
            expand = (lax.broadcasted_iota(jnp.int32, (n_sel, SEL_CHUNK), 0) == blk_of_key)
            picked = jnp.dot(selb, jnp.where(expand, 1.0, 0.0).astype(BF16),
                             preferred_element_type=F32)
            ok = (picked > 0.5) & (kpos <= qpos)
            dist = (qpos - kpos).astype(F32)
            ps, ms, ls, accs = [], [], [], []
            for r in range(GQA_R):
                rs = slice(r * Q_BLOCK, (r + 1) * Q_BLOCK)
                sr = jnp.where(ok, sc[rs] - slopes[r] * dist, NEG)
                m_new = jnp.maximum(m[rs], jnp.max(sr, axis=-1, keepdims=True))
                pr = jnp.where(ok, jnp.exp(sr - m_new), 0.0)
                alpha = jnp.exp(m[rs] - m_new)
                ls.append(alpha * l[rs] + jnp.sum(pr, axis=-1, keepdims=True))
                ms.append(m_new)
                ps.append(pr)
                accs.append(alpha * acc[rs])
            pcat = jnp.concatenate(ps, axis=0).astype(BF16)
            pv = jnp.dot(pcat, vs_ref[pl.ds(k0, SEL_CHUNK), :], preferred_element_type=F32)
            return (jnp.concatenate(ms, axis=0), jnp.concatenate(ls, axis=0),
                    jnp.concatenate(accs, axis=0) + pv)

        rows = GQA_R * Q_BLOCK
        m, l, acc = lax.fori_loop(
            0, n_chunks, sel_chunk,
            (jnp.full((rows, 1), NEG, F32), jnp.zeros((rows, 1), F32), jnp.zeros((rows, D_KV), F32)))
        o_s = (acc / jnp.maximum(l, 1e-30))[:, gs]
        sw = jnp.dot(qg, kwt_ref[gs, pl.ds(win0, WIN_KEYS)], preferred_element_type=F32)
        sw = sw - slope4 * wdistf
        pw = _masked_softmax(sw, wmask)
        o_w = jnp.dot(pw.astype(BF16), vw_ref[pl.ds(win0, WIN_KEYS), :],
                      preferred_element_type=F32)[:, gs]
        for r in range(GQA_R):
            h = g * GQA_R + r
            rs = slice(r * Q_BLOCK, (r + 1) * Q_BLOCK)
            head_out.append(o_c[rs] * gates[:, 3 * h:3 * h + 1] + o_s[rs] * gates[:, 3 * h + 1:3 * h + 2]
                            + o_w[rs] * gates[:, 3 * h + 2:3 * h + 3])
    y = jnp.concatenate(head_out, axis=1)
    y = (y * lax.rsqrt(jnp.mean(y * y, axis=-1, keepdims=True) + 1e-6)) * nrm_ref[...]
    o_ref[...] = y.astype(BF16)


def _attn_prompt(q, gates, kst, vs, kwt, vw, kct, vc, msel, nrm):
    t = q.shape[0]
    full = lambda a: pl.BlockSpec(a.shape, lambda i: (0, 0), pipeline_mode=pl.Buffered(1))
    return pl.pallas_call(
        _attn_prompt_kernel,
        grid=(t // Q_BLOCK,),
        in_specs=[pl.BlockSpec((Q_BLOCK, D_Q), lambda i: (i, 0)),
                  pl.BlockSpec((Q_BLOCK, GATE_PAD), lambda i: (i, 0)),
                  full(kst), full(vs), full(kwt), full(vw), full(kct), full(vc), full(msel),
                  full(nrm)],
        out_specs=pl.BlockSpec((Q_BLOCK, D_Q), lambda i: (i, 0)),
        out_shape=jax.ShapeDtypeStruct((t, D_Q), BF16),
        compiler_params=pltpu.CompilerParams(dimension_semantics=("arbitrary",),
                                             vmem_limit_bytes=VMEM_LIMIT),
        name="attn_prompt",
    )(q, gates, kst, vs, kwt, vw, kct, vc, msel, nrm)


FF_CHUNK = D_FF // 2


def _layer_norm(x, g, b):
    mu = jnp.mean(x, axis=-1, keepdims=True)
    xc = x - mu
    var = jnp.mean(xc * xc, axis=-1, keepdims=True)
    return (xc * lax.rsqrt(var + 1e-5)) * g + b


def _ffn_kernel(x_ref, hr_ref, ha_ref, g1_ref, sh2_ref, sc2_ref, g2_ref, wo_ref, l1g_ref, l1b_ref,
                wup_ref, wdn_ref, l2g_ref, l2b_ref, o_ref, *, alpha):
    d = functools.partial(jnp.dot, preferred_element_type=F32)
    mix = d(hr_ref[...], wo_ref[0:D_RNN, :]) + d(ha_ref[...], wo_ref[D_RNN:D_MODEL, :])
    x1 = _layer_norm(alpha * x_ref[...] + g1_ref[...] * mix, l1g_ref[...], l1b_ref[...])
    u = (x1 * (1.0 + sc2_ref[...]) + sh2_ref[...]).astype(BF16)
    f = None
    for c in range(0, D_FF, FF_CHUNK):
        gate = d(u, wup_ref[:, c:c + FF_CHUNK])
        up = d(u, wup_ref[:, D_FF + c:D_FF + c + FF_CHUNK])
        part = d((gate * _sigmoid(gate) * up).astype(BF16), wdn_ref[c:c + FF_CHUNK, :])
        f = part if f is None else f + part
    o_ref[...] = _layer_norm(alpha * x1 + g2_ref[...] * f, l2g_ref[...], l2b_ref[...])


def _merge_ffn(x2d, hr, ha, g1, sh2, sc2, g2, wo, l1g, l1b, wup, wdn, l2g, l2b, tm, alpha):
    r = x2d.shape[0]
    rm = g1.shape[0]
    mod_spec = (pl.BlockSpec((1, D_MODEL), lambda i: (0, 0)) if rm == 1
                else pl.BlockSpec((tm, D_MODEL), lambda i: (i, 0)))
    vec = pl.BlockSpec((1, D_MODEL), lambda i: (0, 0))
    full = lambda a: pl.BlockSpec(a.shape, lambda i: (0, 0), pipeline_mode=pl.Buffered(1))
    return pl.pallas_call(
        functools.partial(_ffn_kernel, alpha=alpha),
        grid=(r // tm,),
        in_specs=[pl.BlockSpec((tm, D_MODEL), lambda i: (i, 0)),
                  pl.BlockSpec((tm, D_RNN), lambda i: (i, 0)),
                  pl.BlockSpec((tm, D_Q), lambda i: (i, 0)),
                  mod_spec, mod_spec, mod_spec, mod_spec,
                  full(wo), vec, vec, full(wup), full(wdn), vec, vec],
        out_specs=pl.BlockSpec((tm, D_MODEL), lambda i: (i, 0)),
        out_shape=jax.ShapeDtypeStruct((r, D_MODEL), F32),
        compiler_params=pltpu.CompilerParams(dimension_semantics=("arbitrary",),
                                             vmem_limit_bytes=VMEM_LIMIT),
        name="merge_ffn",
    )(x2d, hr, ha, g1, sh2, sc2, g2, wo, l1g, l1b, wup, wdn, l2g, l2b)


CACHE_ROWS_PER_PAGE = 4 * N_KV * HEAD_DIM
CMP_ROWS_PER_PAGE = 2 * N_KV * HEAD_DIM


def _sample_cmp_kernel(pt_ref, cache_ref, q_ref, new_ref, pek_ref, pev_ref, wk_ref, wv_ref,
                       w2k_ref, w2v_ref, msel_ref, oc_ref, pslc_ref, buf, x_s, kc_s, vc_s, sem,
                       *, n_pages):
    b = pl.program_id(0)
    nb = pl.num_programs(0)
    past_len = n_pages * PAGE_SIZE
    n_cmp = past_len // CMP_STRIDE

    def page_copy(bb, p, slot):
        pool = pt_ref[bb * n_pages + p]
        return pltpu.make_async_copy(
            cache_ref.at[pl.ds(pool * CACHE_ROWS_PER_PAGE, CMP_ROWS_PER_PAGE), :],
            buf.at[slot, p], sem.at[slot])

    def start_all(bb, slot):
        for p in range(n_pages):
            page_copy(bb, p, slot).start()

    @pl.when(b == 0)
    def _():
        start_all(0, 0)

    slot = b % 2

    @pl.when(b + 1 < nb)
    def _():
        start_all(b + 1, 1 - slot)

    for p in range(n_pages):
        page_copy(b, p, slot).wait()

    def to_rows(p, carry):
        r0 = pl.multiple_of(p * PAGE_SIZE, PAGE_SIZE)
        for t in range(2):
            x_s[t, pl.ds(r0, PAGE_SIZE), :] = buf[slot, p, t * D_KV:(t + 1) * D_KV, :].T
        return carry

    lax.fori_loop(0, n_pages, to_rows, 0)
    tail_row = lax.broadcasted_iota(jnp.int32, (CMP_STRIDE, D_KV), 0)
    for t in range(2):
        x_s[t, past_len:past_len + CMP_STRIDE, :] = jnp.where(
            tail_row == 0, new_ref[0][:, t * D_KV:(t + 1) * D_KV], 0.0)

    def chunk(c, carry):
        base = pl.multiple_of(c * CMP_ROWS, CMP_ROWS)
        kc, vc = _compress_chunk(x_s, base, CMP_CHUNK, pek_ref, pev_ref, wk_ref, wv_ref,
                                 w2k_ref, w2v_ref)
        o0 = pl.multiple_of(c * CMP_CHUNK, CMP_CHUNK)
        kc_s[pl.ds(o0, CMP_CHUNK), :] = kc
        vc_s[pl.ds(o0, CMP_CHUNK), :] = vc
        return carry

    lax.fori_loop(0, n_cmp // CMP_CHUNK, chunk, 0)

    q = q_ref[0]
    kcb = kc_s[...].astype(BF16)
    vcb = vc_s[...].astype(BF16)
    row = lax.broadcasted_iota(jnp.int32, (N_HEADS, 1), 0)
    first = row < GQA_R
    nt = (((1,), (1,)), ((), ()))
    s0 = lax.dot_general(q, kcb[:, 0:HEAD_DIM], nt, preferred_element_type=F32)
    s1 = lax.dot_general(q, kcb[:, HEAD_DIM:D_KV], nt, preferred_element_type=F32)
    cend = lax.broadcasted_iota(jnp.int32, (1, n_cmp), 1) * CMP_STRIDE + (CMP_LEN - 1)
    s = jnp.where(first, s0, s1) - _pow2_neg(row + 1) * (past_len - cend).astype(F32)
    p = _masked_softmax(s, cend <= past_len)
    o = jnp.dot(p.astype(BF16), vcb, preferred_element_type=F32)
    oc_ref[0] = jnp.where(first, o[:, 0:HEAD_DIM], o[:, HEAD_DIM:D_KV])
    psum = jnp.concatenate([jnp.sum(p[0:GQA_R], axis=0, keepdims=True),
                            jnp.sum(p[GQA_R:N_HEADS], axis=0, keepdims=True)], axis=0)
    pslc_ref[0] = _split3_dot(psum, msel_ref[...])


def _sample_cmp(pt_flat, cache2d, q3, new3, pek, pev, wk, wv, w2k, w2v, msel, n_pages):
    b = q3.shape[0]
    past_len = n_pages * PAGE_SIZE
    n_cmp = past_len // CMP_STRIDE
    full = lambda a: pl.BlockSpec(a.shape, lambda i, pt: (0,) * a.ndim)
    return pl.pallas_call(
        functools.partial(_sample_cmp_kernel, n_pages=n_pages),
        grid_spec=pltpu.PrefetchScalarGridSpec(
            num_scalar_prefetch=1, grid=(b,),
            in_specs=[pl.BlockSpec(memory_space=pl.ANY),
                      pl.BlockSpec((1, N_HEADS, HEAD_DIM), lambda i, pt: (i, 0, 0)),
                      pl.BlockSpec((1, 1, 6 * D_KV), lambda i, pt: (i, 0, 0)),
                      full(pek), full(pev), full(wk), full(wv), full(w2k), full(w2v), full(msel)],
            out_specs=[pl.BlockSpec((1, N_HEADS, HEAD_DIM), lambda i, pt: (i, 0, 0)),
                       pl.BlockSpec((1, N_KV, msel.shape[1]), lambda i, pt: (i, 0, 0))],
            scratch_shapes=[pltpu.VMEM((2, n_pages, CMP_ROWS_PER_PAGE, PAGE_SIZE), F32),
                            pltpu.VMEM((2, past_len + CMP_STRIDE, D_KV), F32),
                            pltpu.VMEM((n_cmp, D_KV), F32), pltpu.VMEM((n_cmp, D_KV), F32),
                            pltpu.SemaphoreType.DMA((2,))]),
        out_shape=[jax.ShapeDtypeStruct((b, N_HEADS, HEAD_DIM), F32),
                   jax.ShapeDtypeStruct((b, N_KV, msel.shape[1]), F32)],
        compiler_params=pltpu.CompilerParams(dimension_semantics=("arbitrary",),
                                             vmem_limit_bytes=VMEM_LIMIT),
        name="sample_cmp",
    )(pt_flat, cache2d, q3, new3, pek, pev, wk, wv, w2k, w2v, msel)


META_W = 128


def _sample_topk_kernel(pslc_ref, meta_ref, *, past_len):
    n_lane = pslc_ref.shape[1]
    n_sel = -(-(past_len + 1) // SEL_BLOCK)
    blk = lax.broadcasted_iota(jnp.int32, (1, n_lane), 1)
    cur = past_len // SEL_BLOCK
    forced = (blk == 0) | (blk == cur) | (blk == cur - 1)
    score = jnp.where(blk * SEL_BLOCK <= past_len,
                      jnp.where(forced, FORCE_SCORE, pslc_ref[...]), -1.0)
    score = jnp.where(blk < n_sel, score, -3.0)
    idxs, vals = _topk_select(score, blk.astype(F32), min(TOP_N, n_sel))
    lane = lax.broadcasted_iota(jnp.int32, (pslc_ref.shape[0], META_W), 1)
    meta = jnp.zeros((pslc_ref.shape[0], META_W), F32)
    for it, (idx, val) in enumerate(zip(idxs, vals)):
        meta = jnp.where(lane == it, idx, meta)
        meta = jnp.where(lane == TOP_N + it, jnp.where(val >= 0.0, 1.0, 0.0), meta)
    meta_ref[...] = meta


def _sample_topk(pslc2d, past_len):
    return pl.pallas_call(
        functools.partial(_sample_topk_kernel, past_len=past_len),
        out_shape=jax.ShapeDtypeStruct((pslc2d.shape[0], META_W), F32),
        name="sample_topk",
    )(pslc2d)


SEL_KEYS = TOP_N * PAGE_SIZE


def _sample_attn_kernel(koff_ref, voff_ref, cache_ref, q_ref, meta_ref, new_ref, newt_ref, win_ref,
                        oc_ref, gt_ref, nrm_ref, e16_ref, ha_ref, wout_ref, kbuf, vbuf, sem,
                        *, past_len):
    b = pl.program_id(0)
    nb = pl.num_programs(0)

    def tile_copies(bb, slot):
        cps = []
        for g in range(N_KV):
            for n in range(TOP_N):
                i = (bb * N_KV + g) * TOP_N + n
                dst = pl.ds(n * PAGE_SIZE, PAGE_SIZE)
                cps.append(pltpu.make_async_copy(cache_ref.at[pl.ds(koff_ref[i], HEAD_DIM), :],
                                                 kbuf.at[slot, g, :, dst], sem.at[slot]))
                cps.append(pltpu.make_async_copy(cache_ref.at[pl.ds(voff_ref[i], HEAD_DIM), :],
                                                 vbuf.at[slot, g, :, dst], sem.at[slot]))
        return cps

    @pl.when(b == 0)
    def _():
        for cp in tile_copies(0, 0):
            cp.start()

    slot = b % 2

    @pl.when(b + 1 < nb)
    def _():
        for cp in tile_copies(b + 1, 1 - slot):
            cp.start()

    for cp in tile_copies(b, slot):
        cp.wait()

    q = q_ref[0]
    qf = q.astype(F32)
    row = lax.broadcasted_iota(jnp.int32, (N_HEADS, 1), 0)
    first = row < GQA_R
    slope = _pow2_neg(row + 1)
    new = new_ref[0]
    nt = (((1,), (1,)), ((), ()))

    def new_rows(off):
        a = new[:, off:off + HEAD_DIM]
        c = new[:, off + HEAD_DIM:off + D_KV]
        v = jnp.where(first, jnp.broadcast_to(a, (N_HEADS, HEAD_DIM)),
                      jnp.broadcast_to(c, (N_HEADS, HEAD_DIM)))
        return v.astype(BF16).astype(F32)

    def attend(s_buf, mask_buf, v_of_p, s_new, new_on, v_new):
        s_buf = jnp.where(mask_buf, s_buf, NEG)
        s_new = jnp.where(new_on, s_new, NEG)
        m = jnp.maximum(jnp.max(s_buf, axis=-1, keepdims=True), s_new)
        p = jnp.where(mask_buf, jnp.exp(s_buf - m), 0.0)
        p_new = jnp.where(new_on, jnp.exp(s_new - m), 0.0)
        den = jnp.maximum(jnp.sum(p, axis=-1, keepdims=True) + p_new, 1e-30)
        p = p / den
        p_new = (p_new / den).astype(BF16).astype(F32)
        return v_of_p(p.astype(BF16)) + p_new * v_new

    meta = meta_ref[0]
    e16 = e16_ref[...]
    jexp = jnp.dot(meta[:, 0:TOP_N].astype(BF16), e16, preferred_element_type=F32)
    vexp = jnp.dot(meta[:, TOP_N:2 * TOP_N].astype(BF16), e16, preferred_element_type=F32)
    lane = lax.broadcasted_iota(jnp.int32, (1, SEL_KEYS), 1) & (PAGE_SIZE - 1)
    ji = jexp.astype(jnp.int32)
    kpos = (ji >> 1) * PAGE_SIZE + lane
    key_ok = (vexp > 0.5) & ((kpos >> 6) == ji) & (kpos < past_len)
    new_blk = past_len // SEL_BLOCK
    new_sel = jnp.max(jnp.where((meta[:, 0:TOP_N] == float(new_blk)) & (meta[:, TOP_N:2 * TOP_N] > 0.5),
                                1.0, 0.0), axis=-1, keepdims=True)
    o_sel = []
    for g in range(N_KV):
        sb = jnp.dot(q, kbuf[slot, g].astype(BF16), preferred_element_type=F32)
        sb = sb - slope * (past_len - kpos[g:g + 1]).astype(F32)
        vb = vbuf[slot, g].astype(BF16)
        o_sel.append((sb, key_ok[g:g + 1], vb))
    k_new = new_rows(2 * D_KV)
    v_new = new_rows(3 * D_KV)
    s_new = jnp.sum(qf * k_new, axis=-1, keepdims=True)
    new_on = jnp.where(first, new_sel[0:1], new_sel[1:2]) > 0.5
    outs = [attend(sb, ok, lambda pb, vb=vb: lax.dot_general(pb, vb, nt, preferred_element_type=F32),
                   s_new, new_on, v_new) for sb, ok, vb in o_sel]
    o_s = jnp.where(first, outs[0], outs[1])

    w = win_ref[0]
    wbuf = w.shape[1]
    lane_b = lax.broadcasted_iota(jnp.int32, (4 * HEAD_DIM, newt_ref.shape[1]), 1)
    new_col = jnp.sum(jnp.where(lane_b == b, newt_ref[...], 0.0), axis=-1, keepdims=True)
    lane_w = lax.broadcasted_iota(jnp.int32, (1, wbuf), 1)
    w_new = jnp.where(lane_w == wbuf - 1, new_col, pltpu.roll(w, wbuf - 1, 1))
    wout_ref[0] = w_new
    wdist = (wbuf - 1 - lane_w).astype(F32)
    wb = w_new.astype(BF16)
    ow = []
    for g in range(N_KV):
        sw = jnp.dot(q, wb[g * HEAD_DIM:(g + 1) * HEAD_DIM], preferred_element_type=F32)
        pw = _masked_softmax(sw - slope * wdist, lane_w >= 0)
        ow.append(lax.dot_general(pw.astype(BF16), wb[D_KV + g * HEAD_DIM:D_KV + (g + 1) * HEAD_DIM],
                                  nt, preferred_element_type=F32))
    o_w = jnp.where(first, ow[0], ow[1])

    gates = _sigmoid(gt_ref[0])
    y = oc_ref[0] * gates[:, 0:1] + o_s * gates[:, 1:2] + o_w * gates[:, 2:3]
    ms = jnp.sum(jnp.sum(y * y, axis=-1, keepdims=True), axis=0, keepdims=True) / D_Q
    ha_ref[0] = (y * lax.rsqrt(ms + 1e-6)) * nrm_ref[...]


def _sample_attn(koff, voff, cache2d, q3, meta3, new3, newt, win3, oc3, gt3, nrm8, e16, past_len):
    b = q3.shape[0]
    wbuf = win3.shape[2]
    full = lambda a: pl.BlockSpec(a.shape, lambda i, ko, vo: (0,) * a.ndim)
    per_b = lambda a: pl.BlockSpec((1,) + a.shape[1:], lambda i, ko, vo: (i,) + (0,) * (a.ndim - 1))
    return pl.pallas_call(
        functools.partial(_sample_attn_kernel, past_len=past_len),
        grid_spec=pltpu.PrefetchScalarGridSpec(
            num_scalar_prefetch=2, grid=(b,),
            in_specs=[pl.BlockSpec(memory_space=pl.ANY), per_b(q3), per_b(meta3), per_b(new3),
                      full(newt), per_b(win3), per_b(oc3), per_b(gt3), full(nrm8), full(e16)],
            out_specs=[pl.BlockSpec((1, N_HEADS, HEAD_DIM), lambda i, ko, vo: (i, 0, 0)),
                       pl.BlockSpec((1, 4 * HEAD_DIM, wbuf), lambda i, ko, vo: (i, 0, 0))],
            scratch_shapes=[pltpu.VMEM((2, N_KV, HEAD_DIM, SEL_KEYS), F32),
                            pltpu.VMEM((2, N_KV, HEAD_DIM, SEL_KEYS), F32),
                            pltpu.SemaphoreType.DMA((2,))]),
        out_shape=[jax.ShapeDtypeStruct((b, N_HEADS, HEAD_DIM), F32),
                   jax.ShapeDtypeStruct((b, 4 * HEAD_DIM, wbuf), F32)],
        compiler_params=pltpu.CompilerParams(dimension_semantics=("arbitrary",),
                                             vmem_limit_bytes=VMEM_LIMIT),
        name="sample_attn",
    )(koff, voff, cache2d, q3, meta3, new3, newt, win3, oc3, gt3, nrm8, e16)


def _block_diag(w):
    n, a, b = w.shape
    eye = jnp.eye(n, dtype=w.dtype)
    return (eye[:, None, :, None] * w[:, :, None, :]).reshape(n * a, n * b)


def _cmp_weights(pe, w1, w2):
    pe_cat = jnp.tile(pe, (1, N_KV)).reshape(1, CMP_LEN * D_KV)
    eye = jnp.eye(N_KV, dtype=w1.dtype)
    w1_big = (w1[:, None, :, None, :] * eye[None, :, None, :, None]).reshape(
        CMP_LEN * D_KV, N_KV * CMP_HID)
    w2_big = (w2[None, :, None, :] * eye[:, None, :, None]).reshape(N_KV * CMP_HID, D_KV)
    return pe_cat, w1_big.astype(BF16), w2_big.astype(BF16)


def _sel_matrix(n_cmp, n_lane):
    n = np.arange(n_cmp)[:, None]
    j = np.arange(n_lane)[None, :]
    lo = SEL_RATIO * j - (CMP_LEN // CMP_STRIDE - 1)
    return jnp.asarray(((n >= lo) & (n <= lo + SEL_RATIO)).astype(np.float32), dtype=BF16)


def kernel(x_prompt, x_sample, cache_kv, state_win, state_conv, state_h, page_table, c_prompt, c_sample, w_ada, b_ada, w_in, conv_w, conv_b, rg_wa, rg_ba, rg_wx, rg_bx, rg_lam, cmp_pe_k, cmp_w1_k, cmp_w2_k, cmp_pe_v, cmp_w1_v, cmp_w2_v, norm_rg, norm_attn, w_out, ln1_g, ln1_b, w_up, w_down, ln2_g, ln2_b):
    depth = w_in.shape[0]
    alpha = float((2.0 * depth) ** 0.25)
    bp, t, _ = x_prompt.shape
    assert bp == 1
    bs = x_sample.shape[0]
    assert x_sample.shape[1] == 1
    n_pages = page_table.shape[1]
    past_len = n_pages * PAGE_SIZE
    n_pool = cache_kv.shape[1]
    wbuf = state_win.shape[2]
    assert wbuf == WINDOW and t % CMP_ROWS == 0 and t >= WIN_KEYS and past_len % CMP_ROWS == 0

    xp = x_prompt.reshape(t, D_MODEL)
    xs = x_sample.reshape(bs, D_MODEL)
    pt_flat = page_table.reshape(-1)
    r_mod = -(-(bs + 1) // 8) * 8
    c_all = jnp.zeros((r_mod, D_MODEL), F32).at[0:bs].set(c_sample).at[bs:bs + 1].set(c_prompt)
    vec = lambda a: a.reshape(1, -1)

    outs = [[] for _ in range(8)]
    for l in range(depth):
        mod = _modulation(c_all, w_ada[l], b_ada[l])
        mod_s = [mod[0:bs, k * D_MODEL:(k + 1) * D_MODEL] for k in range(6)]
        mod_p = [mod[bs:bs + 1, k * D_MODEL:(k + 1) * D_MODEL] for k in range(6)]
        w_in_b = jnp.pad(w_in[l], ((0, 0), (0, D_IN_PAD - w_in.shape[2]))).astype(BF16)
        wrg = jnp.concatenate([_block_diag(rg_wa[l]), _block_diag(rg_wx[l])], axis=1).astype(BF16)
        pek, wk, w2k = _cmp_weights(cmp_pe_k[l], cmp_w1_k[l], cmp_w2_k[l])
        pev, wv, w2v = _cmp_weights(cmp_pe_v[l], cmp_w1_v[l], cmp_w2_v[l])
        wo_b, wup_b, wdn_b = w_out[l].astype(BF16), w_up[l].astype(BF16), w_down[l].astype(BF16)
        rg_args = (conv_w[l], vec(conv_b[l]), wrg, vec(rg_ba[l]), vec(rg_bx[l]), vec(rg_lam[l]),
                   vec(norm_rg[l]))
        ffn_w = (wo_b, vec(ln1_g[l]), vec(ln1_b[l]), wup_b, wdn_b, vec(ln2_g[l]), vec(ln2_b[l]))

        xg, q, rows, gts, vs, vw, kvt, kst, kwt = _in_proj(xp, mod_p[0], mod_p[1], w_in_b, 512,
                                                           2 * D_KV)
        hr, tail, hl = _rg_prompt(xg, *rg_args, tc=256)
        kct, vc = _cmp_prompt(rows, pek, pev, wk, wv, w2k, w2v)
        msel_p = _sel_matrix(t // CMP_STRIDE, t // SEL_BLOCK)
        ha = _attn_prompt(q, gts, kst, vs, kwt, vw, kct, vc, msel_p, vec(norm_attn[l]))
        xp = _merge_ffn(xp, hr, ha, mod_p[2], mod_p[3], mod_p[4], mod_p[5], *ffn_w,
                        tm=512, alpha=alpha)
        outs[0].append(kvt[0:4 * D_KV].reshape(4, N_KV, HEAD_DIM, t).transpose(3, 0, 1, 2)[None])
        outs[2].append(kvt[4 * D_KV:, t - WINDOW:].reshape(2, N_KV, HEAD_DIM, WINDOW)
                       .transpose(3, 0, 1, 2)[None])
        outs[4].append(tail[8 - (CONV_W - 1):][None])
        outs[6].append(hl[0:1])

        xg, q, rows, gts, _, _, kvt, _, _ = _in_proj(xs, mod_s[0], mod_s[1], w_in_b, bs, 6 * D_KV)
        sconv = state_conv[l]
        hr, h_new = _rg_sample(xg, sconv[:, 0], sconv[:, 1], sconv[:, 2], state_h[l], *rg_args)
        cache2d = cache_kv[l].transpose(0, 2, 3, 4, 1).reshape(n_pool * CACHE_ROWS_PER_PAGE, PAGE_SIZE)
        n_lane = -(-(past_len // SEL_BLOCK + 1) // LANE) * LANE
        msel_s = _sel_matrix(past_len // CMP_STRIDE, n_lane)
        q3 = q.reshape(bs, N_HEADS, HEAD_DIM)
        new3 = rows.reshape(bs, 1, 6 * D_KV)
        oc3, pslc = _sample_cmp(pt_flat, cache2d, q3, new3, pek, pev, wk, wv, w2k, w2v, msel_s, n_pages)
        meta = _sample_topk(pslc.reshape(bs * N_KV, n_lane), past_len)
        picks = meta[:, 0:TOP_N].astype(jnp.int32).reshape(bs, N_KV * TOP_N)
        pool = jnp.take_along_axis(page_table, jnp.minimum(picks // 2, n_pages - 1), axis=1)
        grp = (jnp.arange(N_KV * TOP_N, dtype=jnp.int32) // TOP_N)[None, :]
        koff = (pool * CACHE_ROWS_PER_PAGE + (2 * N_KV + grp) * HEAD_DIM).reshape(-1)
        voff = koff + N_KV * HEAD_DIM
        win3 = state_win[l].transpose(0, 2, 3, 4, 1).reshape(bs, 4 * HEAD_DIM, wbuf)
        e16 = jnp.asarray(np.kron(np.eye(TOP_N, dtype=np.float32), np.ones((1, PAGE_SIZE), np.float32)),
                          dtype=BF16)
        ha3, wnew = _sample_attn(koff, voff, cache2d, q3, meta.reshape(bs, N_KV, META_W), new3,
                                 kvt[4 * D_KV:], win3, oc3, gts[:, 0:D_GATE].reshape(bs, N_HEADS, 3),
                                 norm_attn[l].reshape(N_HEADS, HEAD_DIM), e16, past_len)
        ha = ha3.reshape(bs, D_Q).astype(BF16)
        xs = _merge_ffn(xs, hr, ha, mod_s[2], mod_s[3], mod_s[4], mod_s[5], *ffn_w,
                        tm=bs, alpha=alpha)
        outs[1].append(kvt[0:4 * D_KV].reshape(4, N_KV, HEAD_DIM, bs).transpose(3, 0, 1, 2)[:, None])
        outs[3].append(wnew.reshape(bs, 2, N_KV, HEAD_DIM, wbuf).transpose(0, 4, 1, 2, 3))
        outs[5].append(jnp.stack([sconv[:, 1], sconv[:, 2], xg[:, 0:D_RNN]], axis=1))
        outs[7].append(h_new)

    stk = [jnp.stack(o) for o in outs]
    return (xp.reshape(1, t, D_MODEL), xs.reshape(bs, 1, D_MODEL), stk[0], stk[1], stk[2], stk[3],
            stk[4], stk[5], stk[6], stk[7])
```

```python
import functools

import numpy as np
import jax
import jax.numpy as jnp
from jax import lax
from jax.experimental import pallas as pl
from jax.experimental.pallas import tpu as pltpu

F32 = jnp.float32
BF16 = jnp.bfloat16

D_MODEL = 1024
D_RNN = D_MODEL // 2
RNN_BLOCKS = 8
RNN_BLOCK = D_RNN // RNN_BLOCKS
CONV_W = 4
RG_C = 8.0
HEAD_DIM = 64
N_HEADS = (D_MODEL - D_RNN) // HEAD_DIM
N_KV = 2
GQA_R = N_HEADS // N_KV
D_Q = N_HEADS * HEAD_DIM
D_KV = N_KV * HEAD_DIM
D_GATE = 3 * N_HEADS
CMP_STRIDE = 16
CMP_LEN = 2 * CMP_STRIDE
CMP_HID = 128
SEL_BLOCK = 64
SEL_RATIO = SEL_BLOCK // CMP_STRIDE
TOP_N = 16
WINDOW = 512
Q_BLOCK = 128
FORCE_SCORE = 1.0e4
PAGE_SIZE = 128
D_FF = ((8 * D_MODEL // 3 + 255) // 256) * 256
SCALE = HEAD_DIM ** -0.5

NEG = -1e30
LANE = 128

Q_SLOT = LANE
Q_PAD = N_HEADS * Q_SLOT
OFF_Q = 2 * D_RNN
OFF_KV = OFF_Q + Q_PAD
OFF_GATE = OFF_KV + 6 * D_KV
GATE_PAD = 128
D_IN_PAD = OFF_GATE + GATE_PAD

SEL_CHUNK = 1024
BLK_PER_CHUNK = SEL_CHUNK // SEL_BLOCK
POS_SPLIT = 16
AUG_HI = HEAD_DIM
AUG_LO = HEAD_DIM + 1
AUG_PEN = HEAD_DIM + 2
VMEM_LIMIT = 56 * 1024 * 1024


def _sigmoid(x):
    return 1.0 / (1.0 + jnp.exp(-x))


def _gelu_tanh(x):
    c = np.float32(np.sqrt(2.0 / np.pi))
    return 0.5 * x * (1.0 + jnp.tanh(c * (x + np.float32(0.044715) * (x * x * x))))


def _softplus(x):
    return jnp.maximum(x, 0.0) + jnp.log1p(jnp.exp(-jnp.abs(x)))


def _pow2_neg(e_int):
    return lax.bitcast_convert_type((127 - e_int) << 23, F32)


def _masked_softmax(s, mask):
    s = jnp.where(mask, s, NEG)
    m = jnp.max(s, axis=-1, keepdims=True)
    p = jnp.where(mask, jnp.exp(s - m), 0.0)
    return p / jnp.maximum(jnp.sum(p, axis=-1, keepdims=True), 1e-30)


def _split3_dot(x, m_bf16):
    hi = x.astype(BF16)
    r1 = x - hi.astype(F32)
    mid = r1.astype(BF16)
    lo = (r1 - mid.astype(F32)).astype(BF16)
    d = functools.partial(jnp.dot, preferred_element_type=F32)
    return d(hi, m_bf16) + d(mid, m_bf16) + d(lo, m_bf16)


def _topk_select(score, blkf, k):
    work = score
    idxs, vals = [], []
    for _ in range(k):
        m = jnp.max(work, axis=1, keepdims=True)
        idx = jnp.min(jnp.where(work == m, blkf, 1e9), axis=1, keepdims=True)
        work = jnp.where(blkf == idx, -2.0, work)
        idxs.append(idx)
        vals.append(m)
    return idxs, vals


def _mod_kernel(c_ref, w_ref, b_ref, o_ref):
    c = c_ref[...]
    a = (c * _sigmoid(c)).astype(BF16)
    o_ref[...] = jnp.dot(a, w_ref[...].astype(BF16), preferred_element_type=F32) + b_ref[...]


def _modulation(c_all, w_ada, b_ada):
    r, n = c_all.shape[0], w_ada.shape[1]
    tn = 512
    return pl.pallas_call(
        _mod_kernel,
        grid=(n // tn,),
        in_specs=[pl.BlockSpec((r, D_MODEL), lambda j: (0, 0)),
                  pl.BlockSpec((D_MODEL, tn), lambda j: (0, j)),
                  pl.BlockSpec((1, tn), lambda j: (0, j))],
        out_specs=pl.BlockSpec((r, tn), lambda j: (0, j)),
        out_shape=jax.ShapeDtypeStruct((r, n), F32),
        name="adaln_mod",
    )(c_all, w_ada, b_ada.reshape(1, n))


def _inproj_kernel(x_ref, sh_ref, sc_ref, w_ref, qc_ref, xg_ref, q_ref, rows_ref, gt_ref, vs_ref,
                   vw_ref, kvt_ref, kst_ref, kwt_ref, *, n_row_cols):
    u = x_ref[...] * (1.0 + sc_ref[...]) + sh_ref[...]
    p = jnp.dot(u.astype(BF16), w_ref[...], preferred_element_type=F32)
    xg_ref[...] = p[:, 0:OFF_Q]
    q_ref[...] = (p[:, OFF_Q:OFF_KV] * SCALE + qc_ref[...]).astype(BF16)
    kv = p[:, OFF_KV:OFF_GATE]
    rows_ref[...] = kv[:, 0:n_row_cols]
    gt_ref[...] = p[:, OFF_GATE:D_IN_PAD]
    vs_ref[...] = kv[:, 3 * D_KV:4 * D_KV].astype(BF16)
    vw_ref[...] = kv[:, 5 * D_KV:6 * D_KV].astype(BF16)
    kvt = kv.T
    kvt_ref[...] = kvt
    kst_ref[...] = kvt[2 * D_KV:3 * D_KV].astype(BF16)
    kwt_ref[...] = kvt[4 * D_KV:5 * D_KV].astype(BF16)


def _q_consts():
    qc = np.zeros((1, Q_PAD), np.float32)
    for h in range(N_HEADS):
        slope = 2.0 ** (-8.0 * (h + 1) / N_HEADS)
        qc[0, h * Q_SLOT + AUG_HI] = POS_SPLIT * slope
        qc[0, h * Q_SLOT + AUG_LO] = slope
    return jnp.asarray(qc)


def _in_proj(x2d, shift, scale, w_bf16, tm, n_row_cols):
    r = x2d.shape[0]
    rm = shift.shape[0]
    mod_spec = (pl.BlockSpec((1, D_MODEL), lambda i: (0, 0)) if rm == 1
                else pl.BlockSpec((tm, D_MODEL), lambda i: (i, 0)))
    row = lambda w: pl.BlockSpec((tm, w), lambda i: (i, 0))
    col = lambda h: pl.BlockSpec((h, tm), lambda i: (0, i))
    return pl.pallas_call(
        functools.partial(_inproj_kernel, n_row_cols=n_row_cols),
        grid=(r // tm,),
        in_specs=[row(D_MODEL), mod_spec, mod_spec,
                  pl.BlockSpec((D_MODEL, D_IN_PAD), lambda i: (0, 0)),
                  pl.BlockSpec((1, Q_PAD), lambda i: (0, 0))],
        out_specs=[row(OFF_Q), row(Q_PAD), row(n_row_cols), row(GATE_PAD), row(D_KV), row(D_KV),
                   col(6 * D_KV), col(D_KV), col(D_KV)],
        out_shape=[jax.ShapeDtypeStruct((r, OFF_Q), F32),
                   jax.ShapeDtypeStruct((r, Q_PAD), BF16),
                   jax.ShapeDtypeStruct((r, n_row_cols), F32),
                   jax.ShapeDtypeStruct((r, GATE_PAD), F32),
                   jax.ShapeDtypeStruct((r, D_KV), BF16),
                   jax.ShapeDtypeStruct((r, D_KV), BF16),
                   jax.ShapeDtypeStruct((6 * D_KV, r), F32),
                   jax.ShapeDtypeStruct((D_KV, r), BF16),
                   jax.ShapeDtypeStruct((D_KV, r), BF16)],
        compiler_params=pltpu.CompilerParams(dimension_semantics=("arbitrary",),
                                             vmem_limit_bytes=VMEM_LIMIT),
        name="in_proj",
    )(x2d, shift, scale, w_bf16, _q_consts())


def _rg_gates(xc, wrg_ref, ba_ref, bx_ref, lam_ref):
    g = jnp.dot(xc.astype(BF16), wrg_ref[...], preferred_element_type=F32)
    r = _sigmoid(g[:, 0:D_RNN] + ba_ref[...])
    ig = _sigmoid(g[:, D_RNN:2 * D_RNN] + bx_ref[...])
    log_a = -RG_C * r * _softplus(-lam_ref[...])
    a = jnp.exp(log_a)
    b = jnp.sqrt(-jnp.tanh(log_a) * (a * a + 1.0)) * (ig * xc)
    return a, b


def _rg_out(h, gr, nrm_ref):
    y = h * _gelu_tanh(gr)
    return (y * lax.rsqrt(jnp.mean(y * y, axis=-1, keepdims=True) + 1e-6)) * nrm_ref[...]


def _rg_prompt_kernel(xg_ref, cw_ref, cb_ref, wrg_ref, ba_ref, bx_ref, lam_ref, nrm_ref,
                      y_ref, tail_ref, hl_ref, xp_s, a_s, b_s, h_s, hc_s, *, tc):
    i = pl.program_id(0)

    @pl.when(i == 0)
    def _():
        xp_s[0:8, :] = jnp.zeros((8, D_RNN), F32)
        hc_s[...] = jnp.zeros((8, D_RNN), F32)

    xr = xg_ref[:, 0:D_RNN]
    gr = xg_ref[:, D_RNN:2 * D_RNN]
    xp_s[8:8 + tc, :] = xr
    cw = cw_ref[...]
    xc = (cb_ref[...] + cw[0:1] * xp_s[5:5 + tc, :] + cw[1:2] * xp_s[6:6 + tc, :]
          + cw[2:3] * xp_s[7:7 + tc, :] + cw[3:4] * xr)
    a, b = _rg_gates(xc, wrg_ref, ba_ref, bx_ref, lam_ref)
    a_s[...] = a
    b_s[...] = b
    rowi = lax.broadcasted_iota(jnp.int32, (8, D_RNN), 0)

    def tile(gi, hc):
        r0 = pl.multiple_of(gi * 8, 8)
        at = a_s[pl.ds(r0, 8), :]
        bt = b_s[pl.ds(r0, 8), :]
        for d in (1, 2, 4):
            keep = rowi >= d
            a_sh = jnp.where(keep, pltpu.roll(at, d, 0), 1.0)
            b_sh = jnp.where(keep, pltpu.roll(bt, d, 0), 0.0)
            bt = at * b_sh + bt
            at = at * a_sh
        h = at * hc + bt
        h_s[pl.ds(r0, 8), :] = h
        return h[7:8, :]

    hc = lax.fori_loop(0, tc // 8, tile, hc_s[0:1, :])
    hc_s[0:1, :] = hc
    xp_s[0:8, :] = xr[tc - 8:tc]
    y_ref[...] = _rg_out(h_s[...], gr, nrm_ref).astype(BF16)
    tail_ref[...] = xr[tc - 8:tc]
    hl_ref[...] = jnp.broadcast_to(hc, (8, D_RNN))


def _rg_prompt(xg, cw, cb, wrg, ba, bx, lam, nrm, tc):
    t = xg.shape[0]
    vec = pl.BlockSpec((1, D_RNN), lambda i: (0, 0))
    return pl.pallas_call(
        functools.partial(_rg_prompt_kernel, tc=tc),
        grid=(t // tc,),
        in_specs=[pl.BlockSpec((tc, 2 * D_RNN), lambda i: (i, 0)),
                  pl.BlockSpec((CONV_W, D_RNN), lambda i: (0, 0)), vec,
                  pl.BlockSpec((D_RNN, 2 * D_RNN), lambda i: (0, 0)), vec, vec, vec, vec],
        out_specs=[pl.BlockSpec((tc, D_RNN), lambda i: (i, 0)),
                   pl.BlockSpec((8, D_RNN), lambda i: (0, 0)),
                   pl.BlockSpec((8, D_RNN), lambda i: (0, 0))],
        out_shape=[jax.ShapeDtypeStruct((t, D_RNN), BF16),
                   jax.ShapeDtypeStruct((8, D_RNN), F32),
                   jax.ShapeDtypeStruct((8, D_RNN), F32)],
        scratch_shapes=[pltpu.VMEM((tc + 8, D_RNN), F32), pltpu.VMEM((tc, D_RNN), F32),
                        pltpu.VMEM((tc, D_RNN), F32), pltpu.VMEM((tc, D_RNN), F32),
                        pltpu.VMEM((8, D_RNN), F32)],
        compiler_params=pltpu.CompilerParams(dimension_semantics=("arbitrary",)),
        name="rg_prompt",
    )(xg, cw, cb, wrg, ba, bx, lam, nrm)


def _rg_sample_kernel(xg_ref, c0_ref, c1_ref, c2_ref, h0_ref, cw_ref, cb_ref, wrg_ref, ba_ref,
                      bx_ref, lam_ref, nrm_ref, y_ref, h_ref):
    xr = xg_ref[:, 0:D_RNN]
    gr = xg_ref[:, D_RNN:2 * D_RNN]
    cw = cw_ref[...]
    xc = (cb_ref[...] + cw[0:1] * c0_ref[...] + cw[1:2] * c1_ref[...] + cw[2:3] * c2_ref[...]
          + cw[3:4] * xr)
    a, b = _rg_gates(xc, wrg_ref, ba_ref, bx_ref, lam_ref)
    h = a * h0_ref[...] + b
    h_ref[...] = h
    y_ref[...] = _rg_out(h, gr, nrm_ref).astype(BF16)


def _rg_sample(xg, c0, c1, c2, h0, cw, cb, wrg, ba, bx, lam, nrm):
    b = xg.shape[0]
    return pl.pallas_call(
        _rg_sample_kernel,
        out_shape=[jax.ShapeDtypeStruct((b, D_RNN), BF16), jax.ShapeDtypeStruct((b, D_RNN), F32)],
        name="rg_sample",
    )(xg, c0, c1, c2, h0, cw, cb, wrg, ba, bx, lam, nrm)


CMP_PITCH = 20


def _pitch_rows(n_groups):
    return -(-(CMP_PITCH * n_groups) // 8) * 8


def _compress_chunk(x_s, blk0, nblk, pek_ref, pev_ref, wk_ref, wv_ref, w2k_ref, w2v_ref):
    base = blk0 * CMP_PITCH
    outs = []
    for t, (pe_ref, w1_ref, w2_ref) in enumerate(((pek_ref, wk_ref, w2k_ref),
                                                   (pev_ref, wv_ref, w2v_ref))):
        pieces = []
        for l in range(CMP_LEN):
            row = (l // CMP_STRIDE) * CMP_PITCH + l % CMP_STRIDE
            xl = x_s[t, pl.ds(base + row, nblk, stride=CMP_PITCH), :]
            pieces.append((xl + pe_ref[:, l * D_KV:(l + 1) * D_KV]).astype(BF16))
        xcat = jnp.concatenate(pieces, axis=1)
        h = jnp.dot(xcat, w1_ref[...], preferred_element_type=F32)
        h = h * _sigmoid(h)
        outs.append(jnp.dot(h.astype(BF16), w2_ref[...], preferred_element_type=F32))
    return outs


CMP_CHUNK = 128
CMP_ROWS = CMP_CHUNK * CMP_STRIDE
CMP_CHUNK_S = 256


def _cmp_prompt_kernel(x_ref, nxt_ref, pek_ref, pev_ref, wk_ref, wv_ref, w2k_ref, w2v_ref,
                       kct_ref, vc_ref, x_s):
    for t in range(2):
        cols = slice(t * D_KV, (t + 1) * D_KV)
        for j in range(CMP_CHUNK):
            x_s[t, CMP_PITCH * j:CMP_PITCH * j + CMP_STRIDE, :] = (
                x_ref[CMP_STRIDE * j:CMP_STRIDE * (j + 1), cols])
        x_s[t, CMP_PITCH * CMP_CHUNK:CMP_PITCH * CMP_CHUNK + CMP_STRIDE, :] = nxt_ref[:, cols]
    kc, vc = _compress_chunk(x_s, 0, CMP_CHUNK, pek_ref, pev_ref, wk_ref, wv_ref, w2k_ref, w2v_ref)
    kct_ref[...] = kc.T.astype(BF16)
    vc_ref[...] = vc.astype(BF16)


def _cmp_prompt(rows, pek, pev, wk, wv, w2k, w2v):
    t = rows.shape[0]
    n_steps = t // CMP_ROWS
    last16 = t // CMP_STRIDE - 1
    full = lambda a: pl.BlockSpec(a.shape, lambda i: (0, 0))
    return pl.pallas_call(
        _cmp_prompt_kernel,
        grid=(n_steps,),
        in_specs=[pl.BlockSpec((CMP_ROWS, 2 * D_KV), lambda i: (i, 0)),
                  pl.BlockSpec((CMP_STRIDE, 2 * D_KV),
                               lambda i: (jnp.minimum((i + 1) * CMP_CHUNK, last16), 0)),
                  full(pek), full(pev), full(wk), full(wv), full(w2k), full(w2v)],
        out_specs=[pl.BlockSpec((D_KV, CMP_CHUNK), lambda i: (0, i)),
                   pl.BlockSpec((CMP_CHUNK, D_KV), lambda i: (i, 0))],
        out_shape=[jax.ShapeDtypeStruct((D_KV, t // CMP_STRIDE), BF16),
                   jax.ShapeDtypeStruct((t // CMP_STRIDE, D_KV), BF16)],
        scratch_shapes=[pltpu.VMEM((2, _pitch_rows(CMP_CHUNK + 1), D_KV), F32)],
        compiler_params=pltpu.CompilerParams(dimension_semantics=("arbitrary",),
                                             vmem_limit_bytes=VMEM_LIMIT),
        name="cmp_prompt",
    )(rows, rows, pek, pev, wk, wv, w2k, w2v)


WIN_KEYS = WINDOW + Q_BLOCK


def _sel_aug_rows():
    k = np.arange(SEL_CHUNK)
    c = np.zeros((HEAD_DIM, SEL_CHUNK), np.float32)
    c[AUG_HI - HEAD_DIM] = k // POS_SPLIT
    c[AUG_LO - HEAD_DIM] = k % POS_SPLIT
    for b in range(BLK_PER_CHUNK):
        c[AUG_PEN - HEAD_DIM + b] = (k // SEL_BLOCK == b)
    return jnp.asarray(c, dtype=BF16)


def _attn_prompt_kernel(q_ref, gt_ref, kst_ref, vs_ref, kwt_ref, vw_ref, kct_ref, vc_ref,
                        msel_ref, cst_ref, nrm_ref, o_ref, act_ref, m_s, l_s, acc_s):
    i = pl.program_id(0)
    t0 = i * Q_BLOCK
    n_cmp = kct_ref.shape[1]
    n_sel = msel_ref.shape[1]
    rows = GQA_R * Q_BLOCK
    dot = functools.partial(jnp.dot, preferred_element_type=F32)
    qpos = t0 + lax.broadcasted_iota(jnp.int32, (Q_BLOCK, 1), 0)
    qpos4 = jnp.concatenate([qpos] * GQA_R, axis=0)
    qposf4 = qpos4.astype(F32)
    cend = lax.broadcasted_iota(jnp.int32, (1, n_cmp), 1) * CMP_STRIDE + (CMP_LEN - 1)
    blk = lax.broadcasted_iota(jnp.int32, (1, n_sel), 1)
    blkf = blk.astype(F32)
    gates = _sigmoid(gt_ref[...])
    msel = msel_ref[...]
    cur = qpos >> 6
    forced = (blk == 0) | (blk == cur) | (blk == cur - 1)
    in_past = blk * SEL_BLOCK <= qpos
    n_chunks = (t0 + Q_BLOCK + SEL_CHUNK - 1) // SEL_CHUNK
    win0 = pl.multiple_of(jnp.maximum(t0 - WINDOW, 0), LANE)
    wpos = win0 + lax.broadcasted_iota(jnp.int32, (1, WIN_KEYS), 1)
    wdist = qpos4 - wpos
    wmask = (wdist >= 0) & (wdist < WINDOW)
    wdistf = wdist.astype(F32)

    qa, q64, slope4 = [], [], []
    for g in range(N_KV):
        qg = jnp.concatenate([q_ref[:, (g * GQA_R + r) * Q_SLOT:(g * GQA_R + r + 1) * Q_SLOT]
                              for r in range(GQA_R)], axis=0)
        qa.append(qg.astype(F32))
        q64.append(qg[:, 0:HEAD_DIM])
        slope4.append(jnp.concatenate(
            [jnp.full((Q_BLOCK, 1), np.float32(2.0 ** -(g * GQA_R + r + 1)), F32)
             for r in range(GQA_R)], axis=0))

    o_c, scores = [], []
    for g in range(N_KV):
        gs = slice(g * HEAD_DIM, (g + 1) * HEAD_DIM)
        s = dot(q64[g], kct_ref[gs, :])
        s = s - slope4[g] * (qposf4 - cend.astype(F32))
        p = _masked_softmax(s, cend <= qpos4)
        o_c.append(dot(p.astype(BF16), vc_ref[...])[:, gs])
        psum = p[0:Q_BLOCK]
        for r in range(1, GQA_R):
            psum = psum + p[r * Q_BLOCK:(r + 1) * Q_BLOCK]
        p_slc = _split3_dot(psum, msel)
        scores.append(jnp.where(in_past, jnp.where(forced, FORCE_SCORE, p_slc), -1.0))
    idxs, vals = _topk_select(jnp.concatenate(scores, axis=0), blkf, min(TOP_N, n_sel))
    sel = jnp.zeros((N_KV * Q_BLOCK, n_sel), F32)
    for idx, val in zip(idxs, vals):
        sel = jnp.where((blkf == idx) & (val >= 0.0), 1.0, sel)
    pen = jnp.where(sel > 0.5, 0.0, NEG).astype(BF16)
    penb = [pen[g * Q_BLOCK:(g + 1) * Q_BLOCK] for g in range(N_KV)]
    n_chunks_all = n_sel // BLK_PER_CHUNK
    for g in range(N_KV):
        col_any = jnp.max(sel[g * Q_BLOCK:(g + 1) * Q_BLOCK], axis=0, keepdims=True)
        for c in range(n_chunks_all):
            hit = jnp.max(col_any[:, c * BLK_PER_CHUNK:(c + 1) * BLK_PER_CHUNK])
            act_ref[g * n_chunks_all + c] = (hit > 0.5).astype(jnp.int32)

    cst = cst_ref[...]
    oh_lane = lax.broadcasted_iota(jnp.int32, (n_sel, LANE), 1)
    oh_base = lax.broadcasted_iota(jnp.int32, (n_sel, LANE), 0) - oh_lane + AUG_PEN
    oh_ok = (oh_lane >= AUG_PEN) & (oh_lane < AUG_PEN + BLK_PER_CHUNK)

    def group_step(c, g, causal):
        k0 = pl.multiple_of(c * SEL_CHUNK, SEL_CHUNK)
        onehot = jnp.where((oh_base == c * BLK_PER_CHUNK) & oh_ok, 1.0, 0.0).astype(BF16)
        placed = dot(penb[g], onehot)
        qaug = (qa[g] + jnp.concatenate([placed] * GQA_R, axis=0)).astype(BF16)
        kta = jnp.concatenate(
            [kst_ref[g * HEAD_DIM:(g + 1) * HEAD_DIM, pl.ds(k0, SEL_CHUNK)], cst], axis=0)
        s = dot(qaug, kta)
        if causal:
            kpos = k0 + lax.broadcasted_iota(jnp.int32, (1, SEL_CHUNK), 1)
            tri = jnp.where(kpos > qpos, NEG, 0.0)
            s = s + jnp.concatenate([tri] * GQA_R, axis=0)
        crow = slope4[g] * (k0.astype(F32) - qposf4)
        m = m_s[g]
        m_new = jnp.maximum(m, jnp.max(s, axis=-1, keepdims=True) + crow)
        p = jnp.exp(s + (crow - m_new))
        alpha = jnp.exp(m - m_new)
        m_s[g] = m_new
        l_s[g] = alpha * l_s[g] + jnp.sum(p, axis=-1, keepdims=True)
        acc_s[g] = alpha * acc_s[g] + dot(p.astype(BF16), vs_ref[pl.ds(k0, SEL_CHUNK), :])

    m_s[...] = jnp.full(m_s.shape, NEG, F32)
    l_s[...] = jnp.zeros(l_s.shape, F32)
    acc_s[...] = jnp.zeros(acc_s.shape, F32)

    def past_chunk(c, carry):
        for g in range(N_KV):
            @pl.when(act_ref[g * n_chunks_all + c] > 0)
            def _():
                group_step(c, g, causal=False)
        return carry

    lax.fori_loop(0, n_chunks - 1, past_chunk, 0)
    for g in range(N_KV):
        group_step(n_chunks - 1, g, causal=True)

    head_out = []
    for g in range(N_KV):
        gs = slice(g * HEAD_DIM, (g + 1) * HEAD_DIM)
        o_s = (acc_s[g] * (1.0 / jnp.maximum(l_s[g], 1e-30)))[:, gs]
        sw = dot(q64[g], kwt_ref[gs, pl.ds(win0, WIN_KEYS)])
        sw = sw - slope4[g] * wdistf
        pw = _masked_softmax(sw, wmask)
        o_w = dot(pw.astype(BF16), vw_ref[pl.ds(win0, WIN_KEYS), :])[:, gs]
        for r in range(GQA_R):
            h = g * GQA_R + r
            rs = slice(r * Q_BLOCK, (r + 1) * Q_BLOCK)
            head_out.append(o_c[g][rs] * gates[:, 3 * h:3 * h + 1]
                            + o_s[rs] * gates[:, 3 * h + 1:3 * h + 2]
                            + o_w[rs] * gates[:, 3 * h + 2:3 * h + 3])
    y = jnp.concatenate(head_out, axis=1)
    y = (y * lax.rsqrt(jnp.mean(y * y, axis=-1, keepdims=True) + 1e-6)) * nrm_ref[...]
    o_ref[...] = y.astype(BF16)


def _attn_prompt(q, gates, kst, vs, kwt, vw, kct, vc, msel, nrm):
    t = q.shape[0]
    cst = _sel_aug_rows()
    full = lambda a: pl.BlockSpec(a.shape, lambda i: (0, 0), pipeline_mode=pl.Buffered(1))
    return pl.pallas_call(
        _attn_prompt_kernel,
        grid=(t // Q_BLOCK,),
        in_specs=[pl.BlockSpec((Q_BLOCK, Q_PAD), lambda i: (i, 0)),
                  pl.BlockSpec((Q_BLOCK, GATE_PAD), lambda i: (i, 0)),
                  full(kst), full(vs), full(kwt), full(vw), full(kct), full(vc), full(msel),
                  full(cst), full(nrm)],
        out_specs=pl.BlockSpec((Q_BLOCK, D_Q), lambda i: (i, 0)),
        out_shape=jax.ShapeDtypeStruct((t, D_Q), BF16),
        scratch_shapes=[pltpu.SMEM((N_KV * (t // SEL_CHUNK),), jnp.int32),
                        pltpu.VMEM((N_KV, GQA_R * Q_BLOCK, 1), F32),
                        pltpu.VMEM((N_KV, GQA_R * Q_BLOCK, 1), F32),
                        pltpu.VMEM((N_KV, GQA_R * Q_BLOCK, D_KV), F32)],
        compiler_params=pltpu.CompilerParams(dimension_semantics=("arbitrary",),
                                             vmem_limit_bytes=VMEM_LIMIT),
        name="attn_prompt",
    )(q, gates, kst, vs, kwt, vw, kct, vc, msel, cst, nrm)


FF_CHUNK = D_FF // 2


def _layer_norm(x, g, b):
    mu = jnp.mean(x, axis=-1, keepdims=True)
    xc = x - mu
    var = jnp.mean(xc * xc, axis=-1, keepdims=True)
    return (xc * lax.rsqrt(var + 1e-5)) * g + b


def _ffn_kernel(x_ref, hr_ref, ha_ref, g1_ref, sh2_ref, sc2_ref, g2_ref, wo_ref, l1g_ref, l1b_ref,
                wup_ref, wdn_ref, l2g_ref, l2b_ref, o_ref, *, alpha):
    d = functools.partial(jnp.dot, preferred_element_type=F32)
    mix = d(hr_ref[...], wo_ref[0:D_RNN, :]) + d(ha_ref[...], wo_ref[D_RNN:D_MODEL, :])
    x1 = _layer_norm(alpha * x_ref[...] + g1_ref[...] * mix, l1g_ref[...], l1b_ref[...])
    u = (x1 * (1.0 + sc2_ref[...]) + sh2_ref[...]).astype(BF16)
    f = None
    for c in range(0, D_FF, FF_CHUNK):
        gate = d(u, wup_ref[:, c:c + FF_CHUNK])
        up = d(u, wup_ref[:, D_FF + c:D_FF + c + FF_CHUNK])
        part = d((gate * _sigmoid(gate) * up).astype(BF16), wdn_ref[c:c + FF_CHUNK, :])
        f = part if f is None else f + part
    o_ref[...] = _layer_norm(alpha * x1 + g2_ref[...] * f, l2g_ref[...], l2b_ref[...])


def _merge_ffn(x2d, hr, ha, g1, sh2, sc2, g2, wo, l1g, l1b, wup, wdn, l2g, l2b, tm, alpha):
    r = x2d.shape[0]
    rm = g1.shape[0]
    mod_spec = (pl.BlockSpec((1, D_MODEL), lambda i: (0, 0)) if rm == 1
                else pl.BlockSpec((tm, D_MODEL), lambda i: (i, 0)))
    vec = pl.BlockSpec((1, D_MODEL), lambda i: (0, 0))
    full = lambda a: pl.BlockSpec(a.shape, lambda i: (0, 0), pipeline_mode=pl.Buffered(1))
    return pl.pallas_call(
        functools.partial(_ffn_kernel, alpha=alpha),
        grid=(r // tm,),
        in_specs=[pl.BlockSpec((tm, D_MODEL), lambda i: (i, 0)),
                  pl.BlockSpec((tm, D_RNN), lambda i: (i, 0)),
                  pl.BlockSpec((tm, D_Q), lambda i: (i, 0)),
                  mod_spec, mod_spec, mod_spec, mod_spec,
                  full(wo), vec, vec, full(wup), full(wdn), vec, vec],
        out_specs=pl.BlockSpec((tm, D_MODEL), lambda i: (i, 0)),
        out_shape=jax.ShapeDtypeStruct((r, D_MODEL), F32),
        compiler_params=pltpu.CompilerParams(dimension_semantics=("arbitrary",),
                                             vmem_limit_bytes=VMEM_LIMIT),
        name="merge_ffn",
    )(x2d, hr, ha, g1, sh2, sc2, g2, wo, l1g, l1b, wup, wdn, l2g, l2b)


CACHE_ROWS_PER_PAGE = 4 * N_KV * HEAD_DIM
CMP_ROWS_PER_PAGE = 2 * N_KV * HEAD_DIM


def _sample_cmp_kernel(pt_ref, cache_ref, q_ref, new_ref, pek_ref, pev_ref, wk_ref, wv_ref,
                       w2k_ref, w2v_ref, msel_ref, oc_ref, pslc_ref, buf, x_s, kc_s, vc_s, sem,
                       *, n_pages):
    b = pl.program_id(0)
    nb = pl.num_programs(0)
    past_len = n_pages * PAGE_SIZE
    n_cmp = past_len // CMP_STRIDE

    def page_copy(bb, p, slot):
        pool = pt_ref[bb * n_pages + p]
        return pltpu.make_async_copy(
            cache_ref.at[pl.ds(pool * CACHE_ROWS_PER_PAGE, CMP_ROWS_PER_PAGE), :],
            buf.at[slot, p], sem.at[slot])

    def start_all(bb, slot):
        for p in range(n_pages):
            page_copy(bb, p, slot).start()

    @pl.when(b == 0)
    def _():
        start_all(0, 0)

    slot = b % 2

    @pl.when(b + 1 < nb)
    def _():
        start_all(b + 1, 1 - slot)

    for p in range(n_pages):
        page_copy(b, p, slot).wait()

    groups_per_page = PAGE_SIZE // CMP_STRIDE

    def to_rows(p, carry):
        r0 = pl.multiple_of(p * (groups_per_page * CMP_PITCH), 8)
        for t in range(2):
            tile = buf[slot, p, t * D_KV:(t + 1) * D_KV, :].T
            for j in range(groups_per_page):
                x_s[t, pl.ds(r0 + CMP_PITCH * j, CMP_STRIDE), :] = (
                    tile[CMP_STRIDE * j:CMP_STRIDE * (j + 1)])
        return carry

    lax.fori_loop(0, n_pages, to_rows, 0, unroll=4)
    tail_row = lax.broadcasted_iota(jnp.int32, (CMP_STRIDE, D_KV), 0)
    for t in range(2):
        x_s[t, CMP_PITCH * n_cmp:CMP_PITCH * n_cmp + CMP_STRIDE, :] = jnp.where(
            tail_row == 0, new_ref[0][:, t * D_KV:(t + 1) * D_KV], 0.0)

    def chunk(c, carry):
        kc, vc = _compress_chunk(x_s, c * CMP_CHUNK_S, CMP_CHUNK_S, pek_ref, pev_ref, wk_ref,
                                 wv_ref, w2k_ref, w2v_ref)
        o0 = pl.multiple_of(c * CMP_CHUNK_S, CMP_CHUNK_S)
        kc_s[pl.ds(o0, CMP_CHUNK_S), :] = kc
        vc_s[pl.ds(o0, CMP_CHUNK_S), :] = vc
        return carry

    lax.fori_loop(0, n_cmp // CMP_CHUNK_S, chunk, 0)

    q = q_ref[0]
    kcb = kc_s[...].astype(BF16)
    vcb = vc_s[...].astype(BF16)
    row = lax.broadcasted_iota(jnp.int32, (N_HEADS, 1), 0)
    first = row < GQA_R
    nt = (((1,), (1,)), ((), ()))
    s0 = lax.dot_general(q, kcb[:, 0:HEAD_DIM], nt, preferred_element_type=F32)
    s1 = lax.dot_general(q, kcb[:, HEAD_DIM:D_KV], nt, preferred_element_type=F32)
    cend = lax.broadcasted_iota(jnp.int32, (1, n_cmp), 1) * CMP_STRIDE + (CMP_LEN - 1)
    s = jnp.where(first, s0, s1) - _pow2_neg(row + 1) * (past_len - cend).astype(F32)
    p = _masked_softmax(s, cend <= past_len)
    o = jnp.dot(p.astype(BF16), vcb, preferred_element_type=F32)
    oc_ref[0] = jnp.where(first, o[:, 0:HEAD_DIM], o[:, HEAD_DIM:D_KV])
    psum = jnp.concatenate([jnp.sum(p[0:GQA_R], axis=0, keepdims=True),
                            jnp.sum(p[GQA_R:N_HEADS], axis=0, keepdims=True)], axis=0)
    pslc_ref[0] = _split3_dot(psum, msel_ref[...])


def _sample_cmp(pt_flat, cache2d, q3, new3, pek, pev, wk, wv, w2k, w2v, msel, n_pages):
    b = q3.shape[0]
    past_len = n_pages * PAGE_SIZE
    n_cmp = past_len // CMP_STRIDE
    full = lambda a: pl.BlockSpec(a.shape, lambda i, pt: (0,) * a.ndim)
    return pl.pallas_call(
        functools.partial(_sample_cmp_kernel, n_pages=n_pages),
        grid_spec=pltpu.PrefetchScalarGridSpec(
            num_scalar_prefetch=1, grid=(b,),
            in_specs=[pl.BlockSpec(memory_space=pl.ANY),
                      pl.BlockSpec((1, N_HEADS, HEAD_DIM), lambda i, pt: (i, 0, 0)),
                      pl.BlockSpec((1, 1, 6 * D_KV), lambda i, pt: (i, 0, 0)),
                      full(pek), full(pev), full(wk), full(wv), full(w2k), full(w2v), full(msel)],
            out_specs=[pl.BlockSpec((1, N_HEADS, HEAD_DIM), lambda i, pt: (i, 0, 0)),
                       pl.BlockSpec((1, N_KV, msel.shape[1]), lambda i, pt: (i, 0, 0))],
            scratch_shapes=[pltpu.VMEM((2, n_pages, CMP_ROWS_PER_PAGE, PAGE_SIZE), F32),
                            pltpu.VMEM((2, _pitch_rows(n_cmp + 1), D_KV), F32),
                            pltpu.VMEM((n_cmp, D_KV), F32), pltpu.VMEM((n_cmp, D_KV), F32),
                            pltpu.SemaphoreType.DMA((2,))]),
        out_shape=[jax.ShapeDtypeStruct((b, N_HEADS, HEAD_DIM), F32),
                   jax.ShapeDtypeStruct((b, N_KV, msel.shape[1]), F32)],
        compiler_params=pltpu.CompilerParams(dimension_semantics=("arbitrary",),
                                             vmem_limit_bytes=VMEM_LIMIT),
        name="sample_cmp",
    )(pt_flat, cache2d, q3, new3, pek, pev, wk, wv, w2k, w2v, msel)


META_W = 128


def _sample_topk_kernel(pslc_ref, meta_ref, *, past_len):
    n_lane = pslc_ref.shape[1]
    n_sel = -(-(past_len + 1) // SEL_BLOCK)
    blk = lax.broadcasted_iota(jnp.int32, (1, n_lane), 1)
    cur = past_len // SEL_BLOCK
    forced = (blk == 0) | (blk == cur) | (blk == cur - 1)
    score = jnp.where(blk * SEL_BLOCK <= past_len,
                      jnp.where(forced, FORCE_SCORE, pslc_ref[...]), -1.0)
    score = jnp.where(blk < n_sel, score, -3.0)
    idxs, vals = _topk_select(score, blk.astype(F32), min(TOP_N, n_sel))
    lane = lax.broadcasted_iota(jnp.int32, (pslc_ref.shape[0], META_W), 1)
    meta = jnp.zeros((pslc_ref.shape[0], META_W), F32)
    for it, (idx, val) in enumerate(zip(idxs, vals)):
        meta = jnp.where(lane == it, idx, meta)
        meta = jnp.where(lane == TOP_N + it, jnp.where(val >= 0.0, 1.0, 0.0), meta)
    meta_ref[...] = meta


def _sample_topk(pslc2d, past_len):
    return pl.pallas_call(
        functools.partial(_sample_topk_kernel, past_len=past_len),
        out_shape=jax.ShapeDtypeStruct((pslc2d.shape[0], META_W), F32),
        name="sample_topk",
    )(pslc2d)


SEL_KEYS = TOP_N * PAGE_SIZE


def _sample_attn_kernel(koff_ref, voff_ref, cache_ref, q_ref, meta_ref, new_ref, newt_ref, win_ref,
                        oc_ref, gt_ref, nrm_ref, e16_ref, ha_ref, wout_ref, kbuf, vbuf, sem,
                        *, past_len):
    b = pl.program_id(0)
    nb = pl.num_programs(0)

    def tile_copies(bb, slot):
        cps = []
        for g in range(N_KV):
            for n in range(TOP_N):
                i = (bb * N_KV + g) * TOP_N + n
                dst = pl.ds(n * PAGE_SIZE, PAGE_SIZE)
                cps.append(pltpu.make_async_copy(cache_ref.at[pl.ds(koff_ref[i], HEAD_DIM), :],
                                                 kbuf.at[slot, g, :, dst], sem.at[slot]))
                cps.append(pltpu.make_async_copy(cache_ref.at[pl.ds(voff_ref[i], HEAD_DIM), :],
                                                 vbuf.at[slot, g, :, dst], sem.at[slot]))
        return cps

    @pl.when(b == 0)
    def _():
        for cp in tile_copies(0, 0):
            cp.start()

    slot = b % 2

    @pl.when(b + 1 < nb)
    def _():
        for cp in tile_copies(b + 1, 1 - slot):
            cp.start()

    for cp in tile_copies(b, slot):
        cp.wait()

    q = q_ref[0]
    qf = q.astype(F32)
    row = lax.broadcasted_iota(jnp.int32, (N_HEADS, 1), 0)
    first = row < GQA_R
    slope = _pow2_neg(row + 1)
    new = new_ref[0]
    nt = (((1,), (1,)), ((), ()))

    def new_rows(off):
        a = new[:, off:off + HEAD_DIM]
        c = new[:, off + HEAD_DIM:off + D_KV]
        v = jnp.where(first, jnp.broadcast_to(a, (N_HEADS, HEAD_DIM)),
                      jnp.broadcast_to(c, (N_HEADS, HEAD_DIM)))
        return v.astype(BF16).astype(F32)

    def attend(s_buf, mask_buf, v_of_p, s_new, new_on, v_new):
        s_buf = jnp.where(mask_buf, s_buf, NEG)
        s_new = jnp.where(new_on, s_new, NEG)
        m = jnp.maximum(jnp.max(s_buf, axis=-1, keepdims=True), s_new)
        p = jnp.where(mask_buf, jnp.exp(s_buf - m), 0.0)
        p_new = jnp.where(new_on, jnp.exp(s_new - m), 0.0)
        den = jnp.maximum(jnp.sum(p, axis=-1, keepdims=True) + p_new, 1e-30)
        p = p / den
        p_new = (p_new / den).astype(BF16).astype(F32)
        return v_of_p(p.astype(BF16)) + p_new * v_new

    meta = meta_ref[0]
    e16 = e16_ref[...]
    jexp = jnp.dot(meta[:, 0:TOP_N].astype(BF16), e16, preferred_element_type=F32)
    vexp = jnp.dot(meta[:, TOP_N:2 * TOP_N].astype(BF16), e16, preferred_element_type=F32)
    lane = lax.broadcasted_iota(jnp.int32, (1, SEL_KEYS), 1) & (PAGE_SIZE - 1)
    ji = jexp.astype(jnp.int32)
    kpos = (ji >> 1) * PAGE_SIZE + lane
    key_ok = (vexp > 0.5) & ((kpos >> 6) == ji) & (kpos < past_len)
    new_blk = past_len // SEL_BLOCK
    new_sel = jnp.max(jnp.where((meta[:, 0:TOP_N] == float(new_blk)) & (meta[:, TOP_N:2 * TOP_N] > 0.5),
                                1.0, 0.0), axis=-1, keepdims=True)
    o_sel = []
    for g in range(N_KV):
        sb = jnp.dot(q, kbuf[slot, g].astype(BF16), preferred_element_type=F32)
        sb = sb - slope * (past_len - kpos[g:g + 1]).astype(F32)
        vb = vbuf[slot, g].astype(BF16)
        o_sel.append((sb, key_ok[g:g + 1], vb))
    k_new = new_rows(2 * D_KV)
    v_new = new_rows(3 * D_KV)
    s_new = jnp.sum(qf * k_new, axis=-1, keepdims=True)
    new_on = jnp.where(first, new_sel[0:1], new_sel[1:2]) > 0.5
    outs = [attend(sb, ok, lambda pb, vb=vb: lax.dot_general(pb, vb, nt, preferred_element_type=F32),
                   s_new, new_on, v_new) for sb, ok, vb in o_sel]
    o_s = jnp.where(first, outs[0], outs[1])

    w = win_ref[0]
    wbuf = w.shape[1]
    lane_b = lax.broadcasted_iota(jnp.int32, (4 * HEAD_DIM, newt_ref.shape[1]), 1)
    new_col = jnp.sum(jnp.where(lane_b == b, newt_ref[...], 0.0), axis=-1, keepdims=True)
    lane_w = lax.broadcasted_iota(jnp.int32, (1, wbuf), 1)
    w_new = jnp.where(lane_w == wbuf - 1, new_col, pltpu.roll(w, wbuf - 1, 1))
    wout_ref[0] = w_new
    wdist = (wbuf - 1 - lane_w).astype(F32)
    wb = w_new.astype(BF16)
    ow = []
    for g in range(N_KV):
        sw = jnp.dot(q, wb[g * HEAD_DIM:(g + 1) * HEAD_DIM], preferred_element_type=F32)
        pw = _masked_softmax(sw - slope * wdist, lane_w >= 0)
        ow.append(lax.dot_general(pw.astype(BF16), wb[D_KV + g * HEAD_DIM:D_KV + (g + 1) * HEAD_DIM],
                                  nt, preferred_element_type=F32))
    o_w = jnp.where(first, ow[0], ow[1])

    gates = _sigmoid(gt_ref[0])
    y = oc_ref[0] * gates[:, 0:1] + o_s * gates[:, 1:2] + o_w * gates[:, 2:3]
    ms = jnp.sum(jnp.sum(y * y, axis=-1, keepdims=True), axis=0, keepdims=True) / D_Q
    ha_ref[0] = (y * lax.rsqrt(ms + 1e-6)) * nrm_ref[...]


def _sample_attn(koff, voff, cache2d, q3, meta3, new3, newt, win3, oc3, gt3, nrm8, e16, past_len):
    b = q3.shape[0]
    wbuf = win3.shape[2]
    full = lambda a: pl.BlockSpec(a.shape, lambda i, ko, vo: (0,) * a.ndim)
    per_b = lambda a: pl.BlockSpec((1,) + a.shape[1:], lambda i, ko, vo: (i,) + (0,) * (a.ndim - 1))
    return pl.pallas_call(
        functools.partial(_sample_attn_kernel, past_len=past_len),
        grid_spec=pltpu.PrefetchScalarGridSpec(
            num_scalar_prefetch=2, grid=(b,),
            in_specs=[pl.BlockSpec(memory_space=pl.ANY), per_b(q3), per_b(meta3), per_b(new3),
                      full(newt), per_b(win3), per_b(oc3), per_b(gt3), full(nrm8), full(e16)],
            out_specs=[pl.BlockSpec((1, N_HEADS, HEAD_DIM), lambda i, ko, vo: (i, 0, 0)),
                       pl.BlockSpec((1, 4 * HEAD_DIM, wbuf), lambda i, ko, vo: (i, 0, 0))],
            scratch_shapes=[pltpu.VMEM((2, N_KV, HEAD_DIM, SEL_KEYS), F32),
                            pltpu.VMEM((2, N_KV, HEAD_DIM, SEL_KEYS), F32),
                            pltpu.SemaphoreType.DMA((2,))]),
        out_shape=[jax.ShapeDtypeStruct((b, N_HEADS, HEAD_DIM), F32),
                   jax.ShapeDtypeStruct((b, 4 * HEAD_DIM, wbuf), F32)],
        compiler_params=pltpu.CompilerParams(dimension_semantics=("arbitrary",),
                                             vmem_limit_bytes=VMEM_LIMIT),
        name="sample_attn",
    )(koff, voff, cache2d, q3, meta3, new3, newt, win3, oc3, gt3, nrm8, e16)


def _block_diag(w):
    n, a, b = w.shape
    eye = jnp.eye(n, dtype=w.dtype)
    return (eye[:, None, :, None] * w[:, :, None, :]).reshape(n * a, n * b)


def _cmp_weights(pe, w1, w2):
    pe_cat = jnp.tile(pe, (1, N_KV)).reshape(1, CMP_LEN * D_KV)
    eye = jnp.eye(N_KV, dtype=w1.dtype)
    w1_big = (w1[:, None, :, None, :] * eye[None, :, None, :, None]).reshape(
        CMP_LEN * D_KV, N_KV * CMP_HID)
    w2_big = (w2[None, :, None, :] * eye[:, None, :, None]).reshape(N_KV * CMP_HID, D_KV)
    return pe_cat, w1_big.astype(BF16), w2_big.astype(BF16)


def _sel_matrix(n_cmp, n_lane):
    n = np.arange(n_cmp)[:, None]
    j = np.arange(n_lane)[None, :]
    lo = SEL_RATIO * j - (CMP_LEN // CMP_STRIDE - 1)
    return jnp.asarray(((n >= lo) & (n <= lo + SEL_RATIO)).astype(np.float32), dtype=BF16)


def kernel(x_prompt, x_sample, cache_kv, state_win, state_conv, state_h, page_table, c_prompt, c_sample, w_ada, b_ada, w_in, conv_w, conv_b, rg_wa, rg_ba, rg_wx, rg_bx, rg_lam, cmp_pe_k, cmp_w1_k, cmp_w2_k, cmp_pe_v, cmp_w1_v, cmp_w2_v, norm_rg, norm_attn, w_out, ln1_g, ln1_b, w_up, w_down, ln2_g, ln2_b):
    depth = w_in.shape[0]
    alpha = float((2.0 * depth) ** 0.25)
    bp, t, _ = x_prompt.shape
    assert bp == 1
    bs = x_sample.shape[0]
    assert x_sample.shape[1] == 1
    n_pages = page_table.shape[1]
    past_len = n_pages * PAGE_SIZE
    n_pool = cache_kv.shape[1]
    wbuf = state_win.shape[2]
    assert wbuf == WINDOW and t % CMP_ROWS == 0 and t % SEL_CHUNK == 0 and t >= WIN_KEYS
    assert past_len % (CMP_CHUNK_S * CMP_STRIDE) == 0

    xp = x_prompt.reshape(t, D_MODEL)
    xs = x_sample.reshape(bs, D_MODEL)
    pt_flat = page_table.reshape(-1)
    r_mod = -(-(bs + 1) // 8) * 8
    c_all = jnp.zeros((r_mod, D_MODEL), F32).at[0:bs].set(c_sample).at[bs:bs + 1].set(c_prompt)
    vec = lambda a: a.reshape(1, -1)

    outs = [[] for _ in range(8)]
    for l in range(depth):
        mod = _modulation(c_all, w_ada[l], b_ada[l])
        mod_s = [mod[0:bs, k * D_MODEL:(k + 1) * D_MODEL] for k in range(6)]
        mod_p = [mod[bs:bs + 1, k * D_MODEL:(k + 1) * D_MODEL] for k in range(6)]
        w_l = w_in[l]
        w_q = jnp.pad(w_l[:, 2 * D_RNN:2 * D_RNN + D_Q].reshape(D_MODEL, N_HEADS, HEAD_DIM),
                      ((0, 0), (0, 0), (0, Q_SLOT - HEAD_DIM))).reshape(D_MODEL, Q_PAD)
        w_in_b = jnp.concatenate(
            [w_l[:, 0:2 * D_RNN], w_q, w_l[:, 2 * D_RNN + D_Q:],
             jnp.zeros((D_MODEL, GATE_PAD - D_GATE), F32)], axis=1).astype(BF16)
        wrg = jnp.concatenate([_block_diag(rg_wa[l]), _block_diag(rg_wx[l])], axis=1).astype(BF16)
        pek, wk, w2k = _cmp_weights(cmp_pe_k[l], cmp_w1_k[l], cmp_w2_k[l])
        pev, wv, w2v = _cmp_weights(cmp_pe_v[l], cmp_w1_v[l], cmp_w2_v[l])
        wo_b, wup_b, wdn_b = w_out[l].astype(BF16), w_up[l].astype(BF16), w_down[l].astype(BF16)
        rg_args = (conv_w[l], vec(conv_b[l]), wrg, vec(rg_ba[l]), vec(rg_bx[l]), vec(rg_lam[l]),
                   vec(norm_rg[l]))
        ffn_w = (wo_b, vec(ln1_g[l]), vec(ln1_b[l]), wup_b, wdn_b, vec(ln2_g[l]), vec(ln2_b[l]))

        xg, q, rows, gts, vs, vw, kvt, kst, kwt = _in_proj(xp, mod_p[0], mod_p[1], w_in_b, 512,
                                                           2 * D_KV)
        hr, tail, hl = _rg_prompt(xg, *rg_args, tc=256)
        kct, vc = _cmp_prompt(rows, pek, pev, wk, wv, w2k, w2v)
        msel_p = _sel_matrix(t // CMP_STRIDE, t // SEL_BLOCK)
        ha = _attn_prompt(q, gts, kst, vs, kwt, vw, kct, vc, msel_p, vec(norm_attn[l]))
        xp = _merge_ffn(xp, hr, ha, mod_p[2], mod_p[3], mod_p[4], mod_p[5], *ffn_w,
                        tm=512, alpha=alpha)
        outs[0].append(kvt[0:4 * D_KV].reshape(4, N_KV, HEAD_DIM, t).transpose(3, 0, 1, 2)[None])
        outs[2].append(kvt[4 * D_KV:, t - WINDOW:].reshape(2, N_KV, HEAD_DIM, WINDOW)
                       .transpose(3, 0, 1, 2)[None])
        outs[4].append(tail[8 - (CONV_W - 1):][None])
        outs[6].append(hl[0:1])

        xg, q, rows, gts, _, _, kvt, _, _ = _in_proj(xs, mod_s[0], mod_s[1], w_in_b, bs, 6 * D_KV)
        sconv = state_conv[l]
        hr, h_new = _rg_sample(xg, sconv[:, 0], sconv[:, 1], sconv[:, 2], state_h[l], *rg_args)
        cache2d = cache_kv[l].transpose(0, 2, 3, 4, 1).reshape(n_pool * CACHE_ROWS_PER_PAGE, PAGE_SIZE)
        n_lane = -(-(past_len // SEL_BLOCK + 1) // LANE) * LANE
        msel_s = _sel_matrix(past_len // CMP_STRIDE, n_lane)
        q3 = q.reshape(bs, N_HEADS, Q_SLOT)[:, :, 0:HEAD_DIM]
        new3 = rows.reshape(bs, 1, 6 * D_KV)
        oc3, pslc = _sample_cmp(pt_flat, cache2d, q3, new3, pek, pev, wk, wv, w2k, w2v, msel_s, n_pages)
        meta = _sample_topk(pslc.reshape(bs * N_KV, n_lane), past_len)
        picks = meta[:, 0:TOP_N].astype(jnp.int32).reshape(bs, N_KV * TOP_N)
        pool = jnp.take_along_axis(page_table, jnp.minimum(picks // 2, n_pages - 1), axis=1)
        grp = (jnp.arange(N_KV * TOP_N, dtype=jnp.int32) // TOP_N)[None, :]
        koff = (pool * CACHE_ROWS_PER_PAGE + (2 * N_KV + grp) * HEAD_DIM).reshape(-1)
        voff = koff + N_KV * HEAD_DIM
        win3 = state_win[l].transpose(0, 2, 3, 4, 1).reshape(bs, 4 * HEAD_DIM, wbuf)
        e16 = jnp.asarray(np.kron(np.eye(TOP_N, dtype=np.float32), np.ones((1, PAGE_SIZE), np.float32)),
                          dtype=BF16)
        ha3, wnew = _sample_attn(koff, voff, cache2d, q3, meta.reshape(bs, N_KV, META_W), new3,
                                 kvt[4 * D_KV:], win3, oc3, gts[:, 0:D_GATE].reshape(bs, N_HEADS, 3),
                                 norm_attn[l].reshape(N_HEADS, HEAD_DIM), e16, past_len)
        ha = ha3.reshape(bs, D_Q).astype(BF16)
        xs = _merge_ffn(xs, hr, ha, mod_s[2], mod_s[3], mod_s[4], mod_s[5], *ffn_w,
                        tm=bs, alpha=alpha)
        outs[1].append(kvt[0:4 * D_KV].reshape(4, N_KV, HEAD_DIM, bs).transpose(3, 0, 1, 2)[:, None])
        outs[3].append(wnew.reshape(bs, 2, N_KV, HEAD_DIM, wbuf).transpose(0, 4, 1, 2, 3))
        outs[5].append(jnp.stack([sconv[:, 1], sconv[:, 2], xg[:, 0:D_RNN]], axis=1))
        outs[7].append(h_new)

    stk = [jnp.stack(o) for o in outs]
    return (xp.reshape(1, t, D_MODEL), xs.reshape(bs, 1, D_MODEL), stk[0], stk[1], stk[2], stk[3],
            stk[4], stk[5], stk[6], stk[7])
```

```python
import functools

import numpy as np
import jax
import jax.numpy as jnp
from jax import lax
from jax.experimental import pallas as pl
from jax.experimental.pallas import tpu as pltpu

F32 = jnp.float32
BF16 = jnp.bfloat16

D_MODEL = 1024
D_RNN = D_MODEL // 2
RNN_BLOCKS = 8
RNN_BLOCK = D_RNN // RNN_BLOCKS
CONV_W = 4
RG_C = 8.0
HEAD_DIM = 64
N_HEADS = (D_MODEL - D_RNN) // HEAD_DIM
N_KV = 2
GQA_R = N_HEADS // N_KV
D_Q = N_HEADS * HEAD_DIM
D_KV = N_KV * HEAD_DIM
D_GATE = 3 * N_HEADS
CMP_STRIDE = 16
CMP_LEN = 2 * CMP_STRIDE
CMP_HID = 128
SEL_BLOCK = 64
SEL_RATIO = SEL_BLOCK // CMP_STRIDE
TOP_N = 16
WINDOW = 512
Q_BLOCK = 128
FORCE_SCORE = 1.0e4
PAGE_SIZE = 128
D_FF = ((8 * D_MODEL // 3 + 255) // 256) * 256
SCALE = HEAD_DIM ** -0.5

NEG = -1e30
LANE = 128

Q_SLOT = LANE
Q_PAD = N_HEADS * Q_SLOT
OFF_Q = 2 * D_RNN
OFF_KV = OFF_Q + Q_PAD
OFF_GATE = OFF_KV + 6 * D_KV
GATE_PAD = 128
D_IN_PAD = OFF_GATE + GATE_PAD

SEL_CHUNK = 1024
BLK_PER_CHUNK = SEL_CHUNK // SEL_BLOCK
POS_SPLIT = 16
AUG_HI = HEAD_DIM
AUG_LO = HEAD_DIM + 1
AUG_PEN = HEAD_DIM + 2
VMEM_LIMIT = 56 * 1024 * 1024


def _sigmoid(x):
    return 1.0 / (1.0 + jnp.exp(-x))


def _gelu_tanh(x):
    c = np.float32(np.sqrt(2.0 / np.pi))
    return 0.5 * x * (1.0 + jnp.tanh(c * (x + np.float32(0.044715) * (x * x * x))))


def _softplus(x):
    return jnp.maximum(x, 0.0) + jnp.log1p(jnp.exp(-jnp.abs(x)))


def _pow2_neg(e_int):
    return lax.bitcast_convert_type((127 - e_int) << 23, F32)


def _masked_softmax(s, mask):
    s = jnp.where(mask, s, NEG)
    m = jnp.max(s, axis=-1, keepdims=True)
    p = jnp.exp(s - m)
    den = jnp.maximum(jnp.sum(p, axis=-1, keepdims=True), 1e-30)
    return p * jnp.where(m > 0.5 * NEG, 1.0 / den, 0.0)


def _split3_dot(x, m_bf16):
    hi = x.astype(BF16)
    r1 = x - hi.astype(F32)
    mid = r1.astype(BF16)
    lo = (r1 - mid.astype(F32)).astype(BF16)
    d = functools.partial(jnp.dot, preferred_element_type=F32)
    return d(hi, m_bf16) + d(mid, m_bf16) + d(lo, m_bf16)


def _topk_select(score, blkf, k):
    work = score
    idxs, vals = [], []
    for _ in range(k):
        m = jnp.max(work, axis=1, keepdims=True)
        idx = jnp.min(jnp.where(work == m, blkf, 1e9), axis=1, keepdims=True)
        work = jnp.where(blkf == idx, -2.0, work)
        idxs.append(idx)
        vals.append(m)
    return idxs, vals


def _mod_kernel(c_ref, w_ref, b_ref, o_ref):
    c = c_ref[...]
    a = (c * _sigmoid(c)).astype(BF16)
    o_ref[...] = jnp.dot(a, w_ref[...].astype(BF16), preferred_element_type=F32) + b_ref[...]


def _modulation(c_all, w_ada, b_ada):
    r, n = c_all.shape[0], w_ada.shape[1]
    tn = 512
    return pl.pallas_call(
        _mod_kernel,
        grid=(n // tn,),
        in_specs=[pl.BlockSpec((r, D_MODEL), lambda j: (0, 0)),
                  pl.BlockSpec((D_MODEL, tn), lambda j: (0, j)),
                  pl.BlockSpec((1, tn), lambda j: (0, j))],
        out_specs=pl.BlockSpec((r, tn), lambda j: (0, j)),
        out_shape=jax.ShapeDtypeStruct((r, n), F32),
        name="adaln_mod",
    )(c_all, w_ada, b_ada.reshape(1, n))


def _inproj_kernel(x_ref, sh_ref, sc_ref, w_ref, qc_ref, xg_ref, q_ref, rows_ref, gt_ref, vs_ref,
                   vw_ref, kvt_ref, kst_ref, kwt_ref, *, n_row_cols):
    u = x_ref[...] * (1.0 + sc_ref[...]) + sh_ref[...]
    p = jnp.dot(u.astype(BF16), w_ref[...], preferred_element_type=F32)
    xg_ref[...] = p[:, 0:OFF_Q]
    q_ref[...] = (p[:, OFF_Q:OFF_KV] * SCALE + qc_ref[...]).astype(BF16)
    kv = p[:, OFF_KV:OFF_GATE]
    rows_ref[...] = kv[:, 0:n_row_cols]
    gt_ref[...] = p[:, OFF_GATE:D_IN_PAD]
    vs_ref[...] = kv[:, 3 * D_KV:4 * D_KV].astype(BF16)
    vw_ref[...] = kv[:, 5 * D_KV:6 * D_KV].astype(BF16)
    kvt = kv.T
    kvt_ref[...] = kvt
    kst_ref[...] = kvt[2 * D_KV:3 * D_KV].astype(BF16)
    kwt_ref[...] = kvt[4 * D_KV:5 * D_KV].astype(BF16)


def _q_consts():
    qc = np.zeros((1, Q_PAD), np.float32)
    for h in range(N_HEADS):
        slope = 2.0 ** (-8.0 * (h + 1) / N_HEADS)
        qc[0, h * Q_SLOT + AUG_HI] = POS_SPLIT * slope
        qc[0, h * Q_SLOT + AUG_LO] = slope
    return jnp.asarray(qc)


def _in_proj(x2d, shift, scale, w_bf16, tm, n_row_cols):
    r = x2d.shape[0]
    rm = shift.shape[0]
    mod_spec = (pl.BlockSpec((1, D_MODEL), lambda i: (0, 0)) if rm == 1
                else pl.BlockSpec((tm, D_MODEL), lambda i: (i, 0)))
    row = lambda w: pl.BlockSpec((tm, w), lambda i: (i, 0))
    col = lambda h: pl.BlockSpec((h, tm), lambda i: (0, i))
    return pl.pallas_call(
        functools.partial(_inproj_kernel, n_row_cols=n_row_cols),
        grid=(r // tm,),
        in_specs=[row(D_MODEL), mod_spec, mod_spec,
                  pl.BlockSpec((D_MODEL, D_IN_PAD), lambda i: (0, 0)),
                  pl.BlockSpec((1, Q_PAD), lambda i: (0, 0))],
        out_specs=[row(OFF_Q), row(Q_PAD), row(n_row_cols), row(GATE_PAD), row(D_KV), row(D_KV),
                   col(6 * D_KV), col(D_KV), col(D_KV)],
        out_shape=[jax.ShapeDtypeStruct((r, OFF_Q), F32),
                   jax.ShapeDtypeStruct((r, Q_PAD), BF16),
                   jax.ShapeDtypeStruct((r, n_row_cols), F32),
                   jax.ShapeDtypeStruct((r, GATE_PAD), F32),
                   jax.ShapeDtypeStruct((r, D_KV), BF16),
                   jax.ShapeDtypeStruct((r, D_KV), BF16),
                   jax.ShapeDtypeStruct((6 * D_KV, r), F32),
                   jax.ShapeDtypeStruct((D_KV, r), BF16),
                   jax.ShapeDtypeStruct((D_KV, r), BF16)],
        compiler_params=pltpu.CompilerParams(dimension_semantics=("arbitrary",),
                                             vmem_limit_bytes=VMEM_LIMIT),
        name="in_proj",
    )(x2d, shift, scale, w_bf16, _q_consts())


def _rg_gates(xc, wrg_ref, ba_ref, bx_ref, lam_ref):
    g = jnp.dot(xc.astype(BF16), wrg_ref[...], preferred_element_type=F32)
    r = _sigmoid(g[:, 0:D_RNN] + ba_ref[...])
    ig = _sigmoid(g[:, D_RNN:2 * D_RNN] + bx_ref[...])
    log_a = -RG_C * r * _softplus(-lam_ref[...])
    a = jnp.exp(log_a)
    b = jnp.sqrt(-jnp.tanh(log_a) * (a * a + 1.0)) * (ig * xc)
    return a, b


def _rg_out(h, gr, nrm_ref):
    y = h * _gelu_tanh(gr)
    return (y * lax.rsqrt(jnp.mean(y * y, axis=-1, keepdims=True) + 1e-6)) * nrm_ref[...]


def _rg_prompt_kernel(xg_ref, cw_ref, cb_ref, wrg_ref, ba_ref, bx_ref, lam_ref, nrm_ref,
                      y_ref, tail_ref, hl_ref, xp_s, a_s, b_s, h_s, hc_s, *, tc):
    i = pl.program_id(0)

    @pl.when(i == 0)
    def _():
        xp_s[0:8, :] = jnp.zeros((8, D_RNN), F32)
        hc_s[...] = jnp.zeros((8, D_RNN), F32)

    xr = xg_ref[:, 0:D_RNN]
    gr = xg_ref[:, D_RNN:2 * D_RNN]
    xp_s[8:8 + tc, :] = xr
    cw = cw_ref[...]
    xc = (cb_ref[...] + cw[0:1] * xp_s[5:5 + tc, :] + cw[1:2] * xp_s[6:6 + tc, :]
          + cw[2:3] * xp_s[7:7 + tc, :] + cw[3:4] * xr)
    a, b = _rg_gates(xc, wrg_ref, ba_ref, bx_ref, lam_ref)
    a_s[...] = a
    b_s[...] = b
    rowi = lax.broadcasted_iota(jnp.int32, (8, D_RNN), 0)

    def tile(gi, hc):
        r0 = pl.multiple_of(gi * 8, 8)
        at = a_s[pl.ds(r0, 8), :]
        bt = b_s[pl.ds(r0, 8), :]
        for d in (1, 2, 4):
            keep = rowi >= d
            a_sh = jnp.where(keep, pltpu.roll(at, d, 0), 1.0)
            b_sh = jnp.where(keep, pltpu.roll(bt, d, 0), 0.0)
            bt = at * b_sh + bt
            at = at * a_sh
        h = at * hc + bt
        h_s[pl.ds(r0, 8), :] = h
        return h[7:8, :]

    hc = lax.fori_loop(0, tc // 8, tile, hc_s[0:1, :])
    hc_s[0:1, :] = hc
    xp_s[0:8, :] = xr[tc - 8:tc]
    y_ref[...] = _rg_out(h_s[...], gr, nrm_ref).astype(BF16)
    tail_ref[...] = xr[tc - 8:tc]
    hl_ref[...] = jnp.broadcast_to(hc, (8, D_RNN))


def _rg_prompt(xg, cw, cb, wrg, ba, bx, lam, nrm, tc):
    t = xg.shape[0]
    vec = pl.BlockSpec((1, D_RNN), lambda i: (0, 0))
    return pl.pallas_call(
        functools.partial(_rg_prompt_kernel, tc=tc),
        grid=(t // tc,),
        in_specs=[pl.BlockSpec((tc, 2 * D_RNN), lambda i: (i, 0)),
                  pl.BlockSpec((CONV_W, D_RNN), lambda i: (0, 0)), vec,
                  pl.BlockSpec((D_RNN, 2 * D_RNN), lambda i: (0, 0)), vec, vec, vec, vec],
        out_specs=[pl.BlockSpec((tc, D_RNN), lambda i: (i, 0)),
                   pl.BlockSpec((8, D_RNN), lambda i: (0, 0)),
                   pl.BlockSpec((8, D_RNN), lambda i: (0, 0))],
        out_shape=[jax.ShapeDtypeStruct((t, D_RNN), BF16),
                   jax.ShapeDtypeStruct((8, D_RNN), F32),
                   jax.ShapeDtypeStruct((8, D_RNN), F32)],
        scratch_shapes=[pltpu.VMEM((tc + 8, D_RNN), F32), pltpu.VMEM((tc, D_RNN), F32),
                        pltpu.VMEM((tc, D_RNN), F32), pltpu.VMEM((tc, D_RNN), F32),
                        pltpu.VMEM((8, D_RNN), F32)],
        compiler_params=pltpu.CompilerParams(dimension_semantics=("arbitrary",)),
        name="rg_prompt",
    )(xg, cw, cb, wrg, ba, bx, lam, nrm)


def _rg_sample_kernel(xg_ref, c0_ref, c1_ref, c2_ref, h0_ref, cw_ref, cb_ref, wrg_ref, ba_ref,
                      bx_ref, lam_ref, nrm_ref, y_ref, h_ref):
    xr = xg_ref[:, 0:D_RNN]
    gr = xg_ref[:, D_RNN:2 * D_RNN]
    cw = cw_ref[...]
    xc = (cb_ref[...] + cw[0:1] * c0_ref[...] + cw[1:2] * c1_ref[...] + cw[2:3] * c2_ref[...]
          + cw[3:4] * xr)
    a, b = _rg_gates(xc, wrg_ref, ba_ref, bx_ref, lam_ref)
    h = a * h0_ref[...] + b
    h_ref[...] = h
    y_ref[...] = _rg_out(h, gr, nrm_ref).astype(BF16)


def _rg_sample(xg, c0, c1, c2, h0, cw, cb, wrg, ba, bx, lam, nrm):
    b = xg.shape[0]
    return pl.pallas_call(
        _rg_sample_kernel,
        out_shape=[jax.ShapeDtypeStruct((b, D_RNN), BF16), jax.ShapeDtypeStruct((b, D_RNN), F32)],
        name="rg_sample",
    )(xg, c0, c1, c2, h0, cw, cb, wrg, ba, bx, lam, nrm)


CMP_PITCH = 20
CMP_KSPLIT = 8


def _pitch_rows(n_groups):
    return -(-(CMP_PITCH * n_groups) // 8) * 8


def _compress_chunk(x_s, blk0, nblk, pek_ref, pev_ref, wk_ref, wv_ref, w2k_ref, w2v_ref):
    base = blk0 * CMP_PITCH
    outs = []
    for t, (pe_ref, w1_ref, w2_ref) in enumerate(((pek_ref, wk_ref, w2k_ref),
                                                   (pev_ref, wv_ref, w2v_ref))):
        h = None
        for l0 in range(0, CMP_LEN, CMP_KSPLIT):
            pieces = []
            for l in range(l0, l0 + CMP_KSPLIT):
                row = (l // CMP_STRIDE) * CMP_PITCH + l % CMP_STRIDE
                xl = x_s[t, pl.ds(base + row, nblk, stride=CMP_PITCH), :]
                pieces.append((xl + pe_ref[:, l * D_KV:(l + 1) * D_KV]).astype(BF16))
            part = jnp.dot(jnp.concatenate(pieces, axis=1), w1_ref[l0 * D_KV:(l0 + CMP_KSPLIT) * D_KV, :],
                           preferred_element_type=F32)
            h = part if h is None else h + part
        h = h * _sigmoid(h)
        outs.append(jnp.dot(h.astype(BF16), w2_ref[...], preferred_element_type=F32))
    return outs


CMP_CHUNK = 128
CMP_ROWS = CMP_CHUNK * CMP_STRIDE
CMP_CHUNK_S = 512


def _cmp_prompt_kernel(x_ref, nxt_ref, pek_ref, pev_ref, wk_ref, wv_ref, w2k_ref, w2v_ref,
                       kct_ref, vc_ref, x_s):
    for t in range(2):
        cols = slice(t * D_KV, (t + 1) * D_KV)
        for j in range(CMP_CHUNK):
            x_s[t, CMP_PITCH * j:CMP_PITCH * j + CMP_STRIDE, :] = (
                x_ref[CMP_STRIDE * j:CMP_STRIDE * (j + 1), cols])
        x_s[t, CMP_PITCH * CMP_CHUNK:CMP_PITCH * CMP_CHUNK + CMP_STRIDE, :] = nxt_ref[:, cols]
    kc, vc = _compress_chunk(x_s, 0, CMP_CHUNK, pek_ref, pev_ref, wk_ref, wv_ref, w2k_ref, w2v_ref)
    kct_ref[...] = kc.T.astype(BF16)
    vc_ref[...] = vc.astype(BF16)


def _cmp_prompt(rows, pek, pev, wk, wv, w2k, w2v):
    t = rows.shape[0]
    n_steps = t // CMP_ROWS
    last16 = t // CMP_STRIDE - 1
    full = lambda a: pl.BlockSpec(a.shape, lambda i: (0, 0))
    return pl.pallas_call(
        _cmp_prompt_kernel,
        grid=(n_steps,),
        in_specs=[pl.BlockSpec((CMP_ROWS, 2 * D_KV), lambda i: (i, 0)),
                  pl.BlockSpec((CMP_STRIDE, 2 * D_KV),
                               lambda i: (jnp.minimum((i + 1) * CMP_CHUNK, last16), 0)),
                  full(pek), full(pev), full(wk), full(wv), full(w2k), full(w2v)],
        out_specs=[pl.BlockSpec((D_KV, CMP_CHUNK), lambda i: (0, i)),
                   pl.BlockSpec((CMP_CHUNK, D_KV), lambda i: (i, 0))],
        out_shape=[jax.ShapeDtypeStruct((D_KV, t // CMP_STRIDE), BF16),
                   jax.ShapeDtypeStruct((t // CMP_STRIDE, D_KV), BF16)],
        scratch_shapes=[pltpu.VMEM((2, _pitch_rows(CMP_CHUNK + 1), D_KV), F32)],
        compiler_params=pltpu.CompilerParams(dimension_semantics=("arbitrary",),
                                             vmem_limit_bytes=VMEM_LIMIT),
        name="cmp_prompt",
    )(rows, rows, pek, pev, wk, wv, w2k, w2v)


WIN_KEYS = WINDOW + Q_BLOCK
CMP_PREFIX = 256


def _sel_aug_rows():
    k = np.arange(SEL_CHUNK)
    c = np.zeros((HEAD_DIM, SEL_CHUNK), np.float32)
    c[AUG_HI - HEAD_DIM] = k // POS_SPLIT
    c[AUG_LO - HEAD_DIM] = k % POS_SPLIT
    for b in range(BLK_PER_CHUNK):
        c[AUG_PEN - HEAD_DIM + b] = (k // SEL_BLOCK == b)
    return jnp.asarray(c, dtype=BF16)


def _attn_prompt_kernel(q_ref, gt_ref, kst_ref, vs_ref, kwt_ref, vw_ref, kct_ref, vc_ref,
                        msel_ref, cst_ref, nrm_ref, o_ref, act_ref, m_s, l_s, acc_s, oc_s, sc_s):
    i = pl.program_id(0)
    t0 = i * Q_BLOCK
    n_cmp = kct_ref.shape[1]
    n_sel = msel_ref.shape[1]
    rows = GQA_R * Q_BLOCK
    dot = functools.partial(jnp.dot, preferred_element_type=F32)
    qpos = t0 + lax.broadcasted_iota(jnp.int32, (Q_BLOCK, 1), 0)
    qpos4 = jnp.concatenate([qpos] * GQA_R, axis=0)
    qposf4 = qpos4.astype(F32)
    cend = lax.broadcasted_iota(jnp.int32, (1, n_cmp), 1) * CMP_STRIDE + (CMP_LEN - 1)
    blk = lax.broadcasted_iota(jnp.int32, (1, n_sel), 1)
    blkf = blk.astype(F32)
    gates = _sigmoid(gt_ref[...])
    cur = qpos >> 6
    forced = (blk == 0) | (blk == cur) | (blk == cur - 1)
    in_past = blk * SEL_BLOCK <= qpos
    n_chunks = (t0 + Q_BLOCK + SEL_CHUNK - 1) // SEL_CHUNK
    win0 = pl.multiple_of(jnp.maximum(t0 - WINDOW, 0), LANE)
    wpos = win0 + lax.broadcasted_iota(jnp.int32, (1, WIN_KEYS), 1)
    wdist = qpos4 - wpos
    wmask = (wdist >= 0) & (wdist < WINDOW)
    wdistf = wdist.astype(F32)

    qa, q64, slope4 = [], [], []
    for g in range(N_KV):
        qg = jnp.concatenate([q_ref[:, (g * GQA_R + r) * Q_SLOT:(g * GQA_R + r + 1) * Q_SLOT]
                              for r in range(GQA_R)], axis=0)
        qa.append(qg)
        q64.append(qg[:, 0:HEAD_DIM])
        slope4.append(jnp.concatenate(
            [jnp.full((Q_BLOCK, 1), np.float32(2.0 ** -(g * GQA_R + r + 1)), F32)
             for r in range(GQA_R)], axis=0))

    def cmp_branch(n_used):
        cend_u = cend[:, 0:n_used]
        dist = qposf4 - cend_u.astype(F32)
        for g in range(N_KV):
            gs = slice(g * HEAD_DIM, (g + 1) * HEAD_DIM)
            s = dot(q64[g], kct_ref[gs, 0:n_used])
            p = _masked_softmax(s - slope4[g] * dist, cend_u <= qpos4)
            oc_s[g * rows:(g + 1) * rows, :] = dot(p.astype(BF16), vc_ref[0:n_used, :])
            psum = p[0:Q_BLOCK]
            for r in range(1, GQA_R):
                psum = psum + p[r * Q_BLOCK:(r + 1) * Q_BLOCK]
            p_slc = _split3_dot(psum, msel_ref[0:n_used, :])
            sc_s[g * Q_BLOCK:(g + 1) * Q_BLOCK, :] = jnp.where(
                in_past, jnp.where(forced, FORCE_SCORE, p_slc), -1.0)

    n_prefix = max(1, n_cmp // CMP_PREFIX)
    per = n_cmp // n_prefix
    need = (t0 + Q_BLOCK - CMP_LEN) // CMP_STRIDE + 1
    bucket = jnp.clip((need + per - 1) // per - 1, 0, n_prefix - 1)
    for k in range(n_prefix):
        @pl.when(bucket == k)
        def _(k=k):
            cmp_branch((k + 1) * per)

    o_c = [oc_s[g * rows:(g + 1) * rows, g * HEAD_DIM:(g + 1) * HEAD_DIM] for g in range(N_KV)]

    o_w = []
    for g in range(N_KV):
        gs = slice(g * HEAD_DIM, (g + 1) * HEAD_DIM)
        sw = dot(q64[g], kwt_ref[gs, pl.ds(win0, WIN_KEYS)])
        pw = _masked_softmax(sw - slope4[g] * wdistf, wmask)
        o_w.append(dot(pw.astype(BF16), vw_ref[pl.ds(win0, WIN_KEYS), :])[:, gs])

    idxs, vals = _topk_select(sc_s[...], blkf, min(TOP_N, n_sel))
    sel = jnp.zeros((N_KV * Q_BLOCK, n_sel), F32)
    for idx, val in zip(idxs, vals):
        sel = jnp.where((blkf == idx) & (val >= 0.0), 1.0, sel)
    pen = jnp.where(sel > 0.5, 0.0, NEG).astype(BF16)
    penb = [pen[g * Q_BLOCK:(g + 1) * Q_BLOCK] for g in range(N_KV)]
    n_chunks_all = n_sel // BLK_PER_CHUNK
    for g in range(N_KV):
        col_any = jnp.max(sel[g * Q_BLOCK:(g + 1) * Q_BLOCK], axis=0, keepdims=True)
        for c in range(n_chunks_all):
            hit = jnp.max(col_any[:, c * BLK_PER_CHUNK:(c + 1) * BLK_PER_CHUNK])
            act_ref[g * n_chunks_all + c] = (hit > 0.5).astype(jnp.int32)

    cst = cst_ref[...]
    oh_lane = lax.broadcasted_iota(jnp.int32, (n_sel, LANE), 1)
    oh_base = lax.broadcasted_iota(jnp.int32, (n_sel, LANE), 0) - oh_lane + AUG_PEN
    oh_ok = (oh_lane >= AUG_PEN) & (oh_lane < AUG_PEN + BLK_PER_CHUNK)

    def group_step(c, g, causal):
        k0 = pl.multiple_of(c * SEL_CHUNK, SEL_CHUNK)
        onehot = jnp.where((oh_base == c * BLK_PER_CHUNK) & oh_ok, 1.0, 0.0).astype(BF16)
        placed = dot(penb[g], onehot).astype(BF16)
        qaug = qa[g] + jnp.concatenate([placed] * GQA_R, axis=0)
        kta = jnp.concatenate(
            [kst_ref[g * HEAD_DIM:(g + 1) * HEAD_DIM, pl.ds(k0, SEL_CHUNK)], cst], axis=0)
        s = dot(qaug, kta)
        if causal:
            kpos = k0 + lax.broadcasted_iota(jnp.int32, (1, SEL_CHUNK), 1)
            tri = jnp.where(kpos > qpos, NEG, 0.0)
            s = s + jnp.concatenate([tri] * GQA_R, axis=0)
        crow = slope4[g] * (k0.astype(F32) - qposf4)
        rg = slice(g * rows, (g + 1) * rows)
        m = m_s[rg, :]
        m_new = jnp.maximum(m, jnp.max(s, axis=-1, keepdims=True) + crow)
        p = jnp.exp(s + (crow - m_new))
        alpha = jnp.exp(m - m_new)
        return (m_new, alpha * l_s[rg, :] + jnp.sum(p, axis=-1, keepdims=True),
                alpha * acc_s[rg, :] + dot(p.astype(BF16), vs_ref[pl.ds(k0, SEL_CHUNK), :]))

    def fold(c, groups, causal):
        new = [group_step(c, g, causal) for g in groups]
        rg = slice(groups[0] * rows, (groups[-1] + 1) * rows)
        for ref, k in ((m_s, 0), (l_s, 1), (acc_s, 2)):
            ref[rg, :] = jnp.concatenate([n[k] for n in new], axis=0)

    m_s[...] = jnp.full(m_s.shape, NEG, F32)
    l_s[...] = jnp.zeros(l_s.shape, F32)
    acc_s[...] = jnp.zeros(acc_s.shape, F32)

    def past_chunk(c, carry):
        a0 = act_ref[c] > 0
        a1 = act_ref[n_chunks_all + c] > 0

        @pl.when(a0 & a1)
        def _():
            fold(c, (0, 1), causal=False)

        @pl.when(a0 & jnp.logical_not(a1))
        def _():
            fold(c, (0,), causal=False)

        @pl.when(a1 & jnp.logical_not(a0))
        def _():
            fold(c, (1,), causal=False)

        return carry

    lax.fori_loop(0, n_chunks - 1, past_chunk, 0)
    fold(n_chunks - 1, (0, 1), causal=True)

    head_out = []
    for g in range(N_KV):
        gs = slice(g * HEAD_DIM, (g + 1) * HEAD_DIM)
        rg = slice(g * rows, (g + 1) * rows)
        o_s = (acc_s[rg, :] * (1.0 / jnp.maximum(l_s[rg, :], 1e-30)))[:, gs]
        for r in range(GQA_R):
            h = g * GQA_R + r
            rs = slice(r * Q_BLOCK, (r + 1) * Q_BLOCK)
            head_out.append(o_c[g][rs] * gates[:, 3 * h:3 * h + 1]
                            + o_s[rs] * gates[:, 3 * h + 1:3 * h + 2]
                            + o_w[g][rs] * gates[:, 3 * h + 2:3 * h + 3])
    y = jnp.concatenate(head_out, axis=1)
    y = (y * lax.rsqrt(jnp.mean(y * y, axis=-1, keepdims=True) + 1e-6)) * nrm_ref[...]
    o_ref[...] = y.astype(BF16)


def _attn_prompt(q, gates, kst, vs, kwt, vw, kct, vc, msel, nrm):
    t = q.shape[0]
    cst = _sel_aug_rows()
    full = lambda a: pl.BlockSpec(a.shape, lambda i: (0, 0), pipeline_mode=pl.Buffered(1))
    return pl.pallas_call(
        _attn_prompt_kernel,
        grid=(t // Q_BLOCK,),
        in_specs=[pl.BlockSpec((Q_BLOCK, Q_PAD), lambda i: (i, 0)),
                  pl.BlockSpec((Q_BLOCK, GATE_PAD), lambda i: (i, 0)),
                  full(kst), full(vs), full(kwt), full(vw), full(kct), full(vc), full(msel),
                  full(cst), full(nrm)],
        out_specs=pl.BlockSpec((Q_BLOCK, D_Q), lambda i: (i, 0)),
        out_shape=jax.ShapeDtypeStruct((t, D_Q), BF16),
        scratch_shapes=[pltpu.SMEM((N_KV * (t // SEL_CHUNK),), jnp.int32),
                        pltpu.VMEM((N_KV * GQA_R * Q_BLOCK, 1), F32),
                        pltpu.VMEM((N_KV * GQA_R * Q_BLOCK, 1), F32),
                        pltpu.VMEM((N_KV * GQA_R * Q_BLOCK, D_KV), F32),
                        pltpu.VMEM((N_KV * GQA_R * Q_BLOCK, D_KV), F32),
                        pltpu.VMEM((N_KV * Q_BLOCK, t // SEL_BLOCK), F32)],
        compiler_params=pltpu.CompilerParams(dimension_semantics=("arbitrary",),
                                             vmem_limit_bytes=VMEM_LIMIT),
        name="attn_prompt",
    )(q, gates, kst, vs, kwt, vw, kct, vc, msel, cst, nrm)


FF_CHUNK = D_FF // 2


def _layer_norm(x, g, b):
    mu = jnp.mean(x, axis=-1, keepdims=True)
    xc = x - mu
    var = jnp.mean(xc * xc, axis=-1, keepdims=True)
    return (xc * lax.rsqrt(var + 1e-5)) * g + b


def _ffn_kernel(x_ref, hr_ref, ha_ref, g1_ref, sh2_ref, sc2_ref, g2_ref, wo_ref, l1g_ref, l1b_ref,
                wup_ref, wdn_ref, l2g_ref, l2b_ref, o_ref, *, alpha):
    d = functools.partial(jnp.dot, preferred_element_type=F32)
    mix = d(hr_ref[...], wo_ref[0:D_RNN, :]) + d(ha_ref[...], wo_ref[D_RNN:D_MODEL, :])
    x1 = _layer_norm(alpha * x_ref[...] + g1_ref[...] * mix, l1g_ref[...], l1b_ref[...])
    u = (x1 * (1.0 + sc2_ref[...]) + sh2_ref[...]).astype(BF16)
    f = None
    for c in range(0, D_FF, FF_CHUNK):
        gate = d(u, wup_ref[:, c:c + FF_CHUNK])
        up = d(u, wup_ref[:, D_FF + c:D_FF + c + FF_CHUNK])
        part = d((gate * _sigmoid(gate) * up).astype(BF16), wdn_ref[c:c + FF_CHUNK, :])
        f = part if f is None else f + part
    o_ref[...] = _layer_norm(alpha * x1 + g2_ref[...] * f, l2g_ref[...], l2b_ref[...])


def _merge_ffn(x2d, hr, ha, g1, sh2, sc2, g2, wo, l1g, l1b, wup, wdn, l2g, l2b, tm, alpha):
    r = x2d.shape[0]
    rm = g1.shape[0]
    mod_spec = (pl.BlockSpec((1, D_MODEL), lambda i: (0, 0)) if rm == 1
                else pl.BlockSpec((tm, D_MODEL), lambda i: (i, 0)))
    vec = pl.BlockSpec((1, D_MODEL), lambda i: (0, 0))
    full = lambda a: pl.BlockSpec(a.shape, lambda i: (0, 0), pipeline_mode=pl.Buffered(1))
    return pl.pallas_call(
        functools.partial(_ffn_kernel, alpha=alpha),
        grid=(r // tm,),
        in_specs=[pl.BlockSpec((tm, D_MODEL), lambda i: (i, 0)),
                  pl.BlockSpec((tm, D_RNN), lambda i: (i, 0)),
                  pl.BlockSpec((tm, D_Q), lambda i: (i, 0)),
                  mod_spec, mod_spec, mod_spec, mod_spec,
                  full(wo), vec, vec, full(wup), full(wdn), vec, vec],
        out_specs=pl.BlockSpec((tm, D_MODEL), lambda i: (i, 0)),
        out_shape=jax.ShapeDtypeStruct((r, D_MODEL), F32),
        compiler_params=pltpu.CompilerParams(dimension_semantics=("arbitrary",),
                                             vmem_limit_bytes=VMEM_LIMIT),
        name="merge_ffn",
    )(x2d, hr, ha, g1, sh2, sc2, g2, wo, l1g, l1b, wup, wdn, l2g, l2b)


CACHE_ROWS_PER_PAGE = 4 * N_KV * HEAD_DIM
CMP_ROWS_PER_PAGE = 2 * N_KV * HEAD_DIM


def _sample_cmp_kernel(pt_ref, cache_ref, q_ref, new_ref, pek_ref, pev_ref, wk_ref, wv_ref,
                       w2k_ref, w2v_ref, msel_ref, oc_ref, pslc_ref, buf, x_s, kc_s, vc_s, sem,
                       *, n_pages):
    b = pl.program_id(0)
    nb = pl.num_programs(0)
    past_len = n_pages * PAGE_SIZE
    n_cmp = past_len // CMP_STRIDE

    def page_copy(bb, p, slot):
        pool = pt_ref[bb * n_pages + p]
        return pltpu.make_async_copy(
            cache_ref.at[pl.ds(pool * CACHE_ROWS_PER_PAGE, CMP_ROWS_PER_PAGE), :],
            buf.at[slot, p], sem.at[slot])

    def start_all(bb, slot):
        for p in range(n_pages):
            page_copy(bb, p, slot).start()

    @pl.when(b == 0)
    def _():
        start_all(0, 0)

    slot = b % 2

    @pl.when(b + 1 < nb)
    def _():
        start_all(b + 1, 1 - slot)

    for p in range(n_pages):
        page_copy(b, p, slot).wait()

    groups_per_page = PAGE_SIZE // CMP_STRIDE

    def to_rows(p, carry):
        r0 = pl.multiple_of(p * (groups_per_page * CMP_PITCH), 8)
        for t in range(2):
            tile = buf[slot, p, t * D_KV:(t + 1) * D_KV, :].T
            for j in range(groups_per_page):
                x_s[t, pl.ds(r0 + CMP_PITCH * j, CMP_STRIDE), :] = (
                    tile[CMP_STRIDE * j:CMP_STRIDE * (j + 1)])
        return carry

    lax.fori_loop(0, n_pages, to_rows, 0, unroll=8)
    tail_row = lax.broadcasted_iota(jnp.int32, (CMP_STRIDE, D_KV), 0)
    for t in range(2):
        x_s[t, CMP_PITCH * n_cmp:CMP_PITCH * n_cmp + CMP_STRIDE, :] = jnp.where(
            tail_row == 0, new_ref[0][:, t * D_KV:(t + 1) * D_KV], 0.0)

    def chunk(c, carry):
        kc, vc = _compress_chunk(x_s, c * CMP_CHUNK_S, CMP_CHUNK_S, pek_ref, pev_ref, wk_ref,
                                 wv_ref, w2k_ref, w2v_ref)
        o0 = pl.multiple_of(c * CMP_CHUNK_S, CMP_CHUNK_S)
        kc_s[pl.ds(o0, CMP_CHUNK_S), :] = kc
        vc_s[pl.ds(o0, CMP_CHUNK_S), :] = vc
        return carry

    lax.fori_loop(0, n_cmp // CMP_CHUNK_S, chunk, 0)

    q = q_ref[0]
    kcb = kc_s[...].astype(BF16)
    vcb = vc_s[...].astype(BF16)
    row = lax.broadcasted_iota(jnp.int32, (N_HEADS, 1), 0)
    first = row < GQA_R
    nt = (((1,), (1,)), ((), ()))
    s0 = lax.dot_general(q, kcb[:, 0:HEAD_DIM], nt, preferred_element_type=F32)
    s1 = lax.dot_general(q, kcb[:, HEAD_DIM:D_KV], nt, preferred_element_type=F32)
    cend = lax.broadcasted_iota(jnp.int32, (1, n_cmp), 1) * CMP_STRIDE + (CMP_LEN - 1)
    s = jnp.where(first, s0, s1) - _pow2_neg(row + 1) * (past_len - cend).astype(F32)
    p = _masked_softmax(s, cend <= past_len)
    o = jnp.dot(p.astype(BF16), vcb, preferred_element_type=F32)
    oc_ref[0] = jnp.where(first, o[:, 0:HEAD_DIM], o[:, HEAD_DIM:D_KV])
    psum = jnp.concatenate([jnp.sum(p[0:GQA_R], axis=0, keepdims=True),
                            jnp.sum(p[GQA_R:N_HEADS], axis=0, keepdims=True)], axis=0)
    pslc_ref[0] = _split3_dot(psum, msel_ref[...])


def _sample_cmp(pt_flat, cache2d, q3, new3, pek, pev, wk, wv, w2k, w2v, msel, n_pages):
    b = q3.shape[0]
    past_len = n_pages * PAGE_SIZE
    n_cmp = past_len // CMP_STRIDE
    full = lambda a: pl.BlockSpec(a.shape, lambda i, pt: (0,) * a.ndim)
    return pl.pallas_call(
        functools.partial(_sample_cmp_kernel, n_pages=n_pages),
        grid_spec=pltpu.PrefetchScalarGridSpec(
            num_scalar_prefetch=1, grid=(b,),
            in_specs=[pl.BlockSpec(memory_space=pl.ANY),
                      pl.BlockSpec((1, N_HEADS, HEAD_DIM), lambda i, pt: (i, 0, 0)),
                      pl.BlockSpec((1, 1, 6 * D_KV), lambda i, pt: (i, 0, 0)),
                      full(pek), full(pev), full(wk), full(wv), full(w2k), full(w2v), full(msel)],
            out_specs=[pl.BlockSpec((1, N_HEADS, HEAD_DIM), lambda i, pt: (i, 0, 0)),
                       pl.BlockSpec((1, N_KV, msel.shape[1]), lambda i, pt: (i, 0, 0))],
            scratch_shapes=[pltpu.VMEM((2, n_pages, CMP_ROWS_PER_PAGE, PAGE_SIZE), F32),
                            pltpu.VMEM((2, _pitch_rows(n_cmp + 1), D_KV), F32),
                            pltpu.VMEM((n_cmp, D_KV), F32), pltpu.VMEM((n_cmp, D_KV), F32),
                            pltpu.SemaphoreType.DMA((2,))]),
        out_shape=[jax.ShapeDtypeStruct((b, N_HEADS, HEAD_DIM), F32),
                   jax.ShapeDtypeStruct((b, N_KV, msel.shape[1]), F32)],
        compiler_params=pltpu.CompilerParams(dimension_semantics=("arbitrary",),
                                             vmem_limit_bytes=VMEM_LIMIT),
        name="sample_cmp",
    )(pt_flat, cache2d, q3, new3, pek, pev, wk, wv, w2k, w2v, msel)


META_W = 128


def _sample_topk_kernel(pslc_ref, meta_ref, *, past_len):
    n_lane = pslc_ref.shape[1]
    n_sel = -(-(past_len + 1) // SEL_BLOCK)
    blk = lax.broadcasted_iota(jnp.int32, (1, n_lane), 1)
    cur = past_len // SEL_BLOCK
    forced = (blk == 0) | (blk == cur) | (blk == cur - 1)
    score = jnp.where(blk * SEL_BLOCK <= past_len,
                      jnp.where(forced, FORCE_SCORE, pslc_ref[...]), -1.0)
    score = jnp.where(blk < n_sel, score, -3.0)
    idxs, vals = _topk_select(score, blk.astype(F32), min(TOP_N, n_sel))
    lane = lax.broadcasted_iota(jnp.int32, (pslc_ref.shape[0], META_W), 1)
    meta = jnp.zeros((pslc_ref.shape[0], META_W), F32)
    for it, (idx, val) in enumerate(zip(idxs, vals)):
        meta = jnp.where(lane == it, idx, meta)
        meta = jnp.where(lane == TOP_N + it, jnp.where(val >= 0.0, 1.0, 0.0), meta)
    meta_ref[...] = meta


def _sample_topk(pslc2d, past_len):
    return pl.pallas_call(
        functools.partial(_sample_topk_kernel, past_len=past_len),
        out_shape=jax.ShapeDtypeStruct((pslc2d.shape[0], META_W), F32),
        name="sample_topk",
    )(pslc2d)


SEL_KEYS = TOP_N * PAGE_SIZE


def _sample_attn_kernel(koff_ref, voff_ref, cache_ref, q_ref, meta_ref, new_ref, newt_ref, win_ref,
                        oc_ref, gt_ref, nrm_ref, e16_ref, ha_ref, wout_ref, kbuf, vbuf, sem,
                        *, past_len):
    b = pl.program_id(0)
    nb = pl.num_programs(0)

    def tile_copies(bb, slot):
        cps = []
        for g in range(N_KV):
            for n in range(TOP_N):
                i = (bb * N_KV + g) * TOP_N + n
                dst = pl.ds(n * PAGE_SIZE, PAGE_SIZE)
                cps.append(pltpu.make_async_copy(cache_ref.at[pl.ds(koff_ref[i], HEAD_DIM), :],
                                                 kbuf.at[slot, g, :, dst], sem.at[slot]))
                cps.append(pltpu.make_async_copy(cache_ref.at[pl.ds(voff_ref[i], HEAD_DIM), :],
                                                 vbuf.at[slot, g, :, dst], sem.at[slot]))
        return cps

    @pl.when(b == 0)
    def _():
        for cp in tile_copies(0, 0):
            cp.start()

    slot = b % 2

    @pl.when(b + 1 < nb)
    def _():
        for cp in tile_copies(b + 1, 1 - slot):
            cp.start()

    for cp in tile_copies(b, slot):
        cp.wait()

    q = q_ref[0]
    qf = q.astype(F32)
    row = lax.broadcasted_iota(jnp.int32, (N_HEADS, 1), 0)
    first = row < GQA_R
    slope = _pow2_neg(row + 1)
    new = new_ref[0]
    nt = (((1,), (1,)), ((), ()))

    def new_rows(off):
        a = new[:, off:off + HEAD_DIM]
        c = new[:, off + HEAD_DIM:off + D_KV]
        v = jnp.where(first, jnp.broadcast_to(a, (N_HEADS, HEAD_DIM)),
                      jnp.broadcast_to(c, (N_HEADS, HEAD_DIM)))
        return v.astype(BF16).astype(F32)

    def attend(s_buf, mask_buf, v_of_p, s_new, new_on, v_new):
        s_buf = jnp.where(mask_buf, s_buf, NEG)
        s_new = jnp.where(new_on, s_new, NEG)
        m = jnp.maximum(jnp.max(s_buf, axis=-1, keepdims=True), s_new)
        p = jnp.where(mask_buf, jnp.exp(s_buf - m), 0.0)
        p_new = jnp.where(new_on, jnp.exp(s_new - m), 0.0)
        den = jnp.maximum(jnp.sum(p, axis=-1, keepdims=True) + p_new, 1e-30)
        p = p / den
        p_new = (p_new / den).astype(BF16).astype(F32)
        return v_of_p(p.astype(BF16)) + p_new * v_new

    meta = meta_ref[0]
    e16 = e16_ref[...]
    jexp = jnp.dot(meta[:, 0:TOP_N].astype(BF16), e16, preferred_element_type=F32)
    vexp = jnp.dot(meta[:, TOP_N:2 * TOP_N].astype(BF16), e16, preferred_element_type=F32)
    lane = lax.broadcasted_iota(jnp.int32, (1, SEL_KEYS), 1) & (PAGE_SIZE - 1)
    ji = jexp.astype(jnp.int32)
    kpos = (ji >> 1) * PAGE_SIZE + lane
    key_ok = (vexp > 0.5) & ((kpos >> 6) == ji) & (kpos < past_len)
    new_blk = past_len // SEL_BLOCK
    new_sel = jnp.max(jnp.where((meta[:, 0:TOP_N] == float(new_blk)) & (meta[:, TOP_N:2 * TOP_N] > 0.5),
                                1.0, 0.0), axis=-1, keepdims=True)
    o_sel = []
    for g in range(N_KV):
        sb = jnp.dot(q, kbuf[slot, g].astype(BF16), preferred_element_type=F32)
        sb = sb - slope * (past_len - kpos[g:g + 1]).astype(F32)
        vb = vbuf[slot, g].astype(BF16)
        o_sel.append((sb, key_ok[g:g + 1], vb))
    k_new = new_rows(2 * D_KV)
    v_new = new_rows(3 * D_KV)
    s_new = jnp.sum(qf * k_new, axis=-1, keepdims=True)
    new_on = jnp.where(first, new_sel[0:1], new_sel[1:2]) > 0.5
    outs = [attend(sb, ok, lambda pb, vb=vb: lax.dot_general(pb, vb, nt, preferred_element_type=F32),
                   s_new, new_on, v_new) for sb, ok, vb in o_sel]
    o_s = jnp.where(first, outs[0], outs[1])

    w = win_ref[0]
    wbuf = w.shape[1]
    lane_b = lax.broadcasted_iota(jnp.int32, (4 * HEAD_DIM, newt_ref.shape[1]), 1)
    new_col = jnp.sum(jnp.where(lane_b == b, newt_ref[...], 0.0), axis=-1, keepdims=True)
    lane_w = lax.broadcasted_iota(jnp.int32, (1, wbuf), 1)
    w_new = jnp.where(lane_w == wbuf - 1, new_col, pltpu.roll(w, wbuf - 1, 1))
    wout_ref[0] = w_new
    wdist = (wbuf - 1 - lane_w).astype(F32)
    wb = w_new.astype(BF16)
    ow = []
    for g in range(N_KV):
        sw = jnp.dot(q, wb[g * HEAD_DIM:(g + 1) * HEAD_DIM], preferred_element_type=F32)
        pw = _masked_softmax(sw - slope * wdist, lane_w >= 0)
        ow.append(lax.dot_general(pw.astype(BF16), wb[D_KV + g * HEAD_DIM:D_KV + (g + 1) * HEAD_DIM],
                                  nt, preferred_element_type=F32))
    o_w = jnp.where(first, ow[0], ow[1])

    gates = _sigmoid(gt_ref[0])
    y = oc_ref[0] * gates[:, 0:1] + o_s * gates[:, 1:2] + o_w * gates[:, 2:3]
    ms = jnp.sum(jnp.sum(y * y, axis=-1, keepdims=True), axis=0, keepdims=True) / D_Q
    ha_ref[0] = (y * lax.rsqrt(ms + 1e-6)) * nrm_ref[...]


def _sample_attn(koff, voff, cache2d, q3, meta3, new3, newt, win3, oc3, gt3, nrm8, e16, past_len):
    b = q3.shape[0]
    wbuf = win3.shape[2]
    full = lambda a: pl.BlockSpec(a.shape, lambda i, ko, vo: (0,) * a.ndim)
    per_b = lambda a: pl.BlockSpec((1,) + a.shape[1:], lambda i, ko, vo: (i,) + (0,) * (a.ndim - 1))
    return pl.pallas_call(
        functools.partial(_sample_attn_kernel, past_len=past_len),
        grid_spec=pltpu.PrefetchScalarGridSpec(
            num_scalar_prefetch=2, grid=(b,),
            in_specs=[pl.BlockSpec(memory_space=pl.ANY), per_b(q3), per_b(meta3), per_b(new3),
                      full(newt), per_b(win3), per_b(oc3), per_b(gt3), full(nrm8), full(e16)],
            out_specs=[pl.BlockSpec((1, N_HEADS, HEAD_DIM), lambda i, ko, vo: (i, 0, 0)),
                       pl.BlockSpec((1, 4 * HEAD_DIM, wbuf), lambda i, ko, vo: (i, 0, 0))],
            scratch_shapes=[pltpu.VMEM((2, N_KV, HEAD_DIM, SEL_KEYS), F32),
                            pltpu.VMEM((2, N_KV, HEAD_DIM, SEL_KEYS), F32),
                            pltpu.SemaphoreType.DMA((2,))]),
        out_shape=[jax.ShapeDtypeStruct((b, N_HEADS, HEAD_DIM), F32),
                   jax.ShapeDtypeStruct((b, 4 * HEAD_DIM, wbuf), F32)],
        compiler_params=pltpu.CompilerParams(dimension_semantics=("arbitrary",),
                                             vmem_limit_bytes=VMEM_LIMIT),
        name="sample_attn",
    )(koff, voff, cache2d, q3, meta3, new3, newt, win3, oc3, gt3, nrm8, e16)


def _block_diag(w):
    n, a, b = w.shape
    eye = jnp.eye(n, dtype=w.dtype)
    return (eye[:, None, :, None] * w[:, :, None, :]).reshape(n * a, n * b)


def _cmp_weights(pe, w1, w2):
    pe_cat = jnp.tile(pe, (1, N_KV)).reshape(1, CMP_LEN * D_KV)
    eye = jnp.eye(N_KV, dtype=w1.dtype)
    w1_big = (w1[:, None, :, None, :] * eye[None, :, None, :, None]).reshape(
        CMP_LEN * D_KV, N_KV * CMP_HID)
    w2_big = (w2[None, :, None, :] * eye[:, None, :, None]).reshape(N_KV * CMP_HID, D_KV)
    return pe_cat, w1_big.astype(BF16), w2_big.astype(BF16)


def _sel_matrix(n_cmp, n_lane):
    n = np.arange(n_cmp)[:, None]
    j = np.arange(n_lane)[None, :]
    lo = SEL_RATIO * j - (CMP_LEN // CMP_STRIDE - 1)
    return jnp.asarray(((n >= lo) & (n <= lo + SEL_RATIO)).astype(np.float32), dtype=BF16)


def kernel(x_prompt, x_sample, cache_kv, state_win, state_conv, state_h, page_table, c_prompt, c_sample, w_ada, b_ada, w_in, conv_w, conv_b, rg_wa, rg_ba, rg_wx, rg_bx, rg_lam, cmp_pe_k, cmp_w1_k, cmp_w2_k, cmp_pe_v, cmp_w1_v, cmp_w2_v, norm_rg, norm_attn, w_out, ln1_g, ln1_b, w_up, w_down, ln2_g, ln2_b):
    depth = w_in.shape[0]
    alpha = float((2.0 * depth) ** 0.25)
    bp, t, _ = x_prompt.shape
    assert bp == 1
    bs = x_sample.shape[0]
    assert x_sample.shape[1] == 1
    n_pages = page_table.shape[1]
    past_len = n_pages * PAGE_SIZE
    n_pool = cache_kv.shape[1]
    wbuf = state_win.shape[2]
    assert wbuf == WINDOW and t % CMP_ROWS == 0 and t % SEL_CHUNK == 0 and t >= WIN_KEYS
    assert past_len % (CMP_CHUNK_S * CMP_STRIDE) == 0

    xp = x_prompt.reshape(t, D_MODEL)
    xs = x_sample.reshape(bs, D_MODEL)
    pt_flat = page_table.reshape(-1)
    r_mod = -(-(bs + 1) // 8) * 8
    c_all = jnp.zeros((r_mod, D_MODEL), F32).at[0:bs].set(c_sample).at[bs:bs + 1].set(c_prompt)
    vec = lambda a: a.reshape(1, -1)

    outs = [[] for _ in range(8)]
    for l in range(depth):
        mod = _modulation(c_all, w_ada[l], b_ada[l])
        mod_s = [mod[0:bs, k * D_MODEL:(k + 1) * D_MODEL] for k in range(6)]
        mod_p = [mod[bs:bs + 1, k * D_MODEL:(k + 1) * D_MODEL] for k in range(6)]
        w_l = w_in[l]
        w_q = jnp.pad(w_l[:, 2 * D_RNN:2 * D_RNN + D_Q].reshape(D_MODEL, N_HEADS, HEAD_DIM),
                      ((0, 0), (0, 0), (0, Q_SLOT - HEAD_DIM))).reshape(D_MODEL, Q_PAD)
        w_in_b = jnp.concatenate(
            [w_l[:, 0:2 * D_RNN], w_q, w_l[:, 2 * D_RNN + D_Q:],
             jnp.zeros((D_MODEL, GATE_PAD - D_GATE), F32)], axis=1).astype(BF16)
        wrg = jnp.concatenate([_block_diag(rg_wa[l]), _block_diag(rg_wx[l])], axis=1).astype(BF16)
        pek, wk, w2k = _cmp_weights(cmp_pe_k[l], cmp_w1_k[l], cmp_w2_k[l])
        pev, wv, w2v = _cmp_weights(cmp_pe_v[l], cmp_w1_v[l], cmp_w2_v[l])
        wo_b, wup_b, wdn_b = w_out[l].astype(BF16), w_up[l].astype(BF16), w_down[l].astype(BF16)
        rg_args = (conv_w[l], vec(conv_b[l]), wrg, vec(rg_ba[l]), vec(rg_bx[l]), vec(rg_lam[l]),
                   vec(norm_rg[l]))
        ffn_w = (wo_b, vec(ln1_g[l]), vec(ln1_b[l]), wup_b, wdn_b, vec(ln2_g[l]), vec(ln2_b[l]))

        xg, q, rows, gts, vs, vw, kvt, kst, kwt = _in_proj(xp, mod_p[0], mod_p[1], w_in_b, 512,
                                                           2 * D_KV)
        hr, tail, hl = _rg_prompt(xg, *rg_args, tc=256)
        kct, vc = _cmp_prompt(rows, pek, pev, wk, wv, w2k, w2v)
        msel_p = _sel_matrix(t // CMP_STRIDE, t // SEL_BLOCK)
        ha = _attn_prompt(q, gts, kst, vs, kwt, vw, kct, vc, msel_p, vec(norm_attn[l]))
        xp = _merge_ffn(xp, hr, ha, mod_p[2], mod_p[3], mod_p[4], mod_p[5], *ffn_w,
                        tm=512, alpha=alpha)
        outs[0].append(kvt[0:4 * D_KV].reshape(4, N_KV, HEAD_DIM, t).transpose(3, 0, 1, 2)[None])
        outs[2].append(kvt[4 * D_KV:, t - WINDOW:].reshape(2, N_KV, HEAD_DIM, WINDOW)
                       .transpose(3, 0, 1, 2)[None])
        outs[4].append(tail[8 - (CONV_W - 1):][None])
        outs[6].append(hl[0:1])

        xg, q, rows, gts, _, _, kvt, _, _ = _in_proj(xs, mod_s[0], mod_s[1], w_in_b, bs, 6 * D_KV)
        sconv = state_conv[l]
        hr, h_new = _rg_sample(xg, sconv[:, 0], sconv[:, 1], sconv[:, 2], state_h[l], *rg_args)
        cache2d = cache_kv[l].transpose(0, 2, 3, 4, 1).reshape(n_pool * CACHE_ROWS_PER_PAGE, PAGE_SIZE)
        n_lane = -(-(past_len // SEL_BLOCK + 1) // LANE) * LANE
        msel_s = _sel_matrix(past_len // CMP_STRIDE, n_lane)
        q3 = q.reshape(bs, N_HEADS, Q_SLOT)[:, :, 0:HEAD_DIM]
        new3 = rows.reshape(bs, 1, 6 * D_KV)
        oc3, pslc = _sample_cmp(pt_flat, cache2d, q3, new3, pek, pev, wk, wv, w2k, w2v, msel_s, n_pages)
        meta = _sample_topk(pslc.reshape(bs * N_KV, n_lane), past_len)
        picks = meta[:, 0:TOP_N].astype(jnp.int32).reshape(bs, N_KV * TOP_N)
        pool = jnp.take_along_axis(page_table, jnp.minimum(picks // 2, n_pages - 1), axis=1)
        grp = (jnp.arange(N_KV * TOP_N, dtype=jnp.int32) // TOP_N)[None, :]
        koff = (pool * CACHE_ROWS_PER_PAGE + (2 * N_KV + grp) * HEAD_DIM).reshape(-1)
        voff = koff + N_KV * HEAD_DIM
        win3 = state_win[l].transpose(0, 2, 3, 4, 1).reshape(bs, 4 * HEAD_DIM, wbuf)
        e16 = jnp.asarray(np.kron(np.eye(TOP_N, dtype=np.float32), np.ones((1, PAGE_SIZE), np.float32)),
                          dtype=BF16)
        ha3, wnew = _sample_attn(koff, voff, cache2d, q3, meta.reshape(bs, N_KV, META_W), new3,
                                 kvt[4 * D_KV:], win3, oc3, gts[:, 0:D_GATE].reshape(bs, N_HEADS, 3),
                                 norm_attn[l].reshape(N_HEADS, HEAD_DIM), e16, past_len)
        ha = ha3.reshape(bs, D_Q).astype(BF16)
        xs = _merge_ffn(xs, hr, ha, mod_s[2], mod_s[3], mod_s[4], mod_s[5], *ffn_w,
                        tm=bs, alpha=alpha)
        outs[1].append(kvt[0:4 * D_KV].reshape(4, N_KV, HEAD_DIM, bs).transpose(3, 0, 1, 2)[:, None])
        outs[3].append(wnew.reshape(bs, 2, N_KV, HEAD_DIM, wbuf).transpose(0, 4, 1, 2, 3))
        outs[5].append(jnp.stack([sconv[:, 1], sconv[:, 2], xg[:, 0:D_RNN]], axis=1))
        outs[7].append(h_new)

    stk = [jnp.stack(o) for o in outs]
    return (xp.reshape(1, t, D_MODEL), xs.reshape(bs, 1, D_MODEL), stk[0], stk[1], stk[2], stk[3],
            stk[4], stk[5], stk[6], stk[7])
```

```python
import functools

import numpy as np
import jax
import jax.numpy as jnp
from jax import lax
from jax.experimental import pallas as pl
from jax.experimental.pallas import tpu as pltpu

F32 = jnp.float32
BF16 = jnp.bfloat16

D_MODEL = 1024
D_RNN = D_MODEL // 2
RNN_BLOCKS = 8
RNN_BLOCK = D_RNN // RNN_BLOCKS
CONV_W = 4
RG_C = 8.0
HEAD_DIM = 64
N_HEADS = (D_MODEL - D_RNN) // HEAD_DIM
N_KV = 2
GQA_R = N_HEADS // N_KV
D_Q = N_HEADS * HEAD_DIM
D_KV = N_KV * HEAD_DIM
D_GATE = 3 * N_HEADS
CMP_STRIDE = 16
CMP_LEN = 2 * CMP_STRIDE
CMP_HID = 128
SEL_BLOCK = 64
SEL_RATIO = SEL_BLOCK // CMP_STRIDE
TOP_N = 16
WINDOW = 512
Q_BLOCK = 128
FORCE_SCORE = 1.0e4
PAGE_SIZE = 128
D_FF = ((8 * D_MODEL // 3 + 255) // 256) * 256
SCALE = HEAD_DIM ** -0.5

NEG = -1e30
LANE = 128

Q_SLOT = LANE
Q_PAD = N_HEADS * Q_SLOT
OFF_Q = 2 * D_RNN
OFF_KV = OFF_Q + Q_PAD
OFF_GATE = OFF_KV + 6 * D_KV
GATE_PAD = 128
D_IN_PAD = OFF_GATE + GATE_PAD

SEL_CHUNK = 1024
BLK_PER_CHUNK = SEL_CHUNK // SEL_BLOCK
POS_SPLIT = 16
AUG_HI = HEAD_DIM
AUG_LO = HEAD_DIM + 1
AUG_PEN = HEAD_DIM + 2
VMEM_LIMIT = 56 * 1024 * 1024


def _sigmoid(x):
    return 1.0 / (1.0 + jnp.exp(-x))


def _gelu_tanh(x):
    c = np.float32(np.sqrt(2.0 / np.pi))
    return 0.5 * x * (1.0 + jnp.tanh(c * (x + np.float32(0.044715) * (x * x * x))))


def _softplus(x):
    return jnp.maximum(x, 0.0) + jnp.log1p(jnp.exp(-jnp.abs(x)))


def _pow2_neg(e_int):
    return lax.bitcast_convert_type((127 - e_int) << 23, F32)


def _masked_softmax(s, mask):
    s = jnp.where(mask, s, NEG)
    m = jnp.max(s, axis=-1, keepdims=True)
    p = jnp.exp(s - m)
    den = jnp.maximum(jnp.sum(p, axis=-1, keepdims=True), 1e-30)
    return p * jnp.where(m > 0.5 * NEG, 1.0 / den, 0.0)


def _split3_dot(x, m_bf16):
    hi = x.astype(BF16)
    r1 = x - hi.astype(F32)
    mid = r1.astype(BF16)
    lo = (r1 - mid.astype(F32)).astype(BF16)
    d = functools.partial(jnp.dot, preferred_element_type=F32)
    return d(hi, m_bf16) + d(mid, m_bf16) + d(lo, m_bf16)


def _topk_select(score, blkf, k):
    work = score
    idxs, vals = [], []
    for _ in range(k):
        m = jnp.max(work, axis=1, keepdims=True)
        idx = jnp.min(jnp.where(work == m, blkf, 1e9), axis=1, keepdims=True)
        work = jnp.where(blkf == idx, -2.0, work)
        idxs.append(idx)
        vals.append(m)
    return idxs, vals


def _mod_kernel(c_ref, w_ref, b_ref, o_ref):
    c = c_ref[...]
    a = (c * _sigmoid(c)).astype(BF16)
    o_ref[...] = jnp.dot(a, w_ref[...].astype(BF16), preferred_element_type=F32) + b_ref[...]


def _modulation(c_all, w_ada, b_ada):
    r, n = c_all.shape[0], w_ada.shape[1]
    tn = 512
    return pl.pallas_call(
        _mod_kernel,
        grid=(n // tn,),
        in_specs=[pl.BlockSpec((r, D_MODEL), lambda j: (0, 0)),
                  pl.BlockSpec((D_MODEL, tn), lambda j: (0, j)),
                  pl.BlockSpec((1, tn), lambda j: (0, j))],
        out_specs=pl.BlockSpec((r, tn), lambda j: (0, j)),
        out_shape=jax.ShapeDtypeStruct((r, n), F32),
        name="adaln_mod",
    )(c_all, w_ada, b_ada.reshape(1, n))


def _inproj_kernel(x_ref, sh_ref, sc_ref, w_ref, qc_ref, xg_ref, q_ref, rows_ref, gt_ref, vs_ref,
                   vw_ref, kvt_ref, kst_ref, kwt_ref, *, n_row_cols):
    u = x_ref[...] * (1.0 + sc_ref[...]) + sh_ref[...]
    p = jnp.dot(u.astype(BF16), w_ref[...], preferred_element_type=F32)
    xg_ref[...] = p[:, 0:OFF_Q]
    q_ref[...] = (p[:, OFF_Q:OFF_KV] * SCALE + qc_ref[...]).astype(BF16)
    kv = p[:, OFF_KV:OFF_GATE]
    rows_ref[...] = kv[:, 0:n_row_cols]
    gt_ref[...] = p[:, OFF_GATE:D_IN_PAD]
    vs_ref[...] = kv[:, 3 * D_KV:4 * D_KV].astype(BF16)
    vw_ref[...] = kv[:, 5 * D_KV:6 * D_KV].astype(BF16)
    kvt = kv.T
    kvt_ref[...] = kvt
    kst_ref[...] = kvt[2 * D_KV:3 * D_KV].astype(BF16)
    kwt_ref[...] = kvt[4 * D_KV:5 * D_KV].astype(BF16)


def _q_consts():
    qc = np.zeros((1, Q_PAD), np.float32)
    for h in range(N_HEADS):
        slope = 2.0 ** (-8.0 * (h + 1) / N_HEADS)
        qc[0, h * Q_SLOT + AUG_HI] = POS_SPLIT * slope
        qc[0, h * Q_SLOT + AUG_LO] = slope
    return jnp.asarray(qc)


def _in_proj(x2d, shift, scale, w_bf16, tm, n_row_cols):
    r = x2d.shape[0]
    rm = shift.shape[0]
    mod_spec = (pl.BlockSpec((1, D_MODEL), lambda i: (0, 0)) if rm == 1
                else pl.BlockSpec((tm, D_MODEL), lambda i: (i, 0)))
    row = lambda w: pl.BlockSpec((tm, w), lambda i: (i, 0))
    col = lambda h: pl.BlockSpec((h, tm), lambda i: (0, i))
    return pl.pallas_call(
        functools.partial(_inproj_kernel, n_row_cols=n_row_cols),
        grid=(r // tm,),
        in_specs=[row(D_MODEL), mod_spec, mod_spec,
                  pl.BlockSpec((D_MODEL, D_IN_PAD), lambda i: (0, 0)),
                  pl.BlockSpec((1, Q_PAD), lambda i: (0, 0))],
        out_specs=[row(OFF_Q), row(Q_PAD), row(n_row_cols), row(GATE_PAD), row(D_KV), row(D_KV),
                   col(6 * D_KV), col(D_KV), col(D_KV)],
        out_shape=[jax.ShapeDtypeStruct((r, OFF_Q), F32),
                   jax.ShapeDtypeStruct((r, Q_PAD), BF16),
                   jax.ShapeDtypeStruct((r, n_row_cols), F32),
                   jax.ShapeDtypeStruct((r, GATE_PAD), F32),
                   jax.ShapeDtypeStruct((r, D_KV), BF16),
                   jax.ShapeDtypeStruct((r, D_KV), BF16),
                   jax.ShapeDtypeStruct((6 * D_KV, r), F32),
                   jax.ShapeDtypeStruct((D_KV, r), BF16),
                   jax.ShapeDtypeStruct((D_KV, r), BF16)],
        compiler_params=pltpu.CompilerParams(dimension_semantics=("arbitrary",),
                                             vmem_limit_bytes=VMEM_LIMIT),
        name="in_proj",
    )(x2d, shift, scale, w_bf16, _q_consts())


def _rg_gates(xc, wrg_ref, ba_ref, bx_ref, lam_ref):
    g = jnp.dot(xc.astype(BF16), wrg_ref[...], preferred_element_type=F32)
    r = _sigmoid(g[:, 0:D_RNN] + ba_ref[...])
    ig = _sigmoid(g[:, D_RNN:2 * D_RNN] + bx_ref[...])
    log_a = -RG_C * r * _softplus(-lam_ref[...])
    a = jnp.exp(log_a)
    b = jnp.sqrt(-jnp.tanh(log_a) * (a * a + 1.0)) * (ig * xc)
    return a, b


def _rg_out(h, gr, nrm_ref):
    y = h * _gelu_tanh(gr)
    return (y * lax.rsqrt(jnp.mean(y * y, axis=-1, keepdims=True) + 1e-6)) * nrm_ref[...]


def _rg_prompt_kernel(xg_ref, cw_ref, cb_ref, wrg_ref, ba_ref, bx_ref, lam_ref, nrm_ref,
                      y_ref, tail_ref, hl_ref, xp_s, a_s, b_s, h_s, hc_s, *, tc):
    i = pl.program_id(0)

    @pl.when(i == 0)
    def _():
        xp_s[0:8, :] = jnp.zeros((8, D_RNN), F32)
        hc_s[...] = jnp.zeros((8, D_RNN), F32)

    xr = xg_ref[:, 0:D_RNN]
    gr = xg_ref[:, D_RNN:2 * D_RNN]
    xp_s[8:8 + tc, :] = xr
    cw = cw_ref[...]
    xc = (cb_ref[...] + cw[0:1] * xp_s[5:5 + tc, :] + cw[1:2] * xp_s[6:6 + tc, :]
          + cw[2:3] * xp_s[7:7 + tc, :] + cw[3:4] * xr)
    a, b = _rg_gates(xc, wrg_ref, ba_ref, bx_ref, lam_ref)
    a_s[...] = a
    b_s[...] = b
    rowi = lax.broadcasted_iota(jnp.int32, (8, D_RNN), 0)

    def tile(gi, hc):
        r0 = pl.multiple_of(gi * 8, 8)
        at = a_s[pl.ds(r0, 8), :]
        bt = b_s[pl.ds(r0, 8), :]
        for d in (1, 2, 4):
            keep = rowi >= d
            a_sh = jnp.where(keep, pltpu.roll(at, d, 0), 1.0)
            b_sh = jnp.where(keep, pltpu.roll(bt, d, 0), 0.0)
            bt = at * b_sh + bt
            at = at * a_sh
        h = at * hc + bt
        h_s[pl.ds(r0, 8), :] = h
        return h[7:8, :]

    hc = lax.fori_loop(0, tc // 8, tile, hc_s[0:1, :])
    hc_s[0:1, :] = hc
    xp_s[0:8, :] = xr[tc - 8:tc]
    y_ref[...] = _rg_out(h_s[...], gr, nrm_ref).astype(BF16)
    tail_ref[...] = xr[tc - 8:tc]
    hl_ref[...] = jnp.broadcast_to(hc, (8, D_RNN))


def _rg_prompt(xg, cw, cb, wrg, ba, bx, lam, nrm, tc):
    t = xg.shape[0]
    vec = pl.BlockSpec((1, D_RNN), lambda i: (0, 0))
    return pl.pallas_call(
        functools.partial(_rg_prompt_kernel, tc=tc),
        grid=(t // tc,),
        in_specs=[pl.BlockSpec((tc, 2 * D_RNN), lambda i: (i, 0)),
                  pl.BlockSpec((CONV_W, D_RNN), lambda i: (0, 0)), vec,
                  pl.BlockSpec((D_RNN, 2 * D_RNN), lambda i: (0, 0)), vec, vec, vec, vec],
        out_specs=[pl.BlockSpec((tc, D_RNN), lambda i: (i, 0)),
                   pl.BlockSpec((8, D_RNN), lambda i: (0, 0)),
                   pl.BlockSpec((8, D_RNN), lambda i: (0, 0))],
        out_shape=[jax.ShapeDtypeStruct((t, D_RNN), BF16),
                   jax.ShapeDtypeStruct((8, D_RNN), F32),
                   jax.ShapeDtypeStruct((8, D_RNN), F32)],
        scratch_shapes=[pltpu.VMEM((tc + 8, D_RNN), F32), pltpu.VMEM((tc, D_RNN), F32),
                        pltpu.VMEM((tc, D_RNN), F32), pltpu.VMEM((tc, D_RNN), F32),
                        pltpu.VMEM((8, D_RNN), F32)],
        compiler_params=pltpu.CompilerParams(dimension_semantics=("arbitrary",)),
        name="rg_prompt",
    )(xg, cw, cb, wrg, ba, bx, lam, nrm)


def _rg_sample_kernel(xg_ref, c0_ref, c1_ref, c2_ref, h0_ref, cw_ref, cb_ref, wrg_ref, ba_ref,
                      bx_ref, lam_ref, nrm_ref, y_ref, h_ref):
    xr = xg_ref[:, 0:D_RNN]
    gr = xg_ref[:, D_RNN:2 * D_RNN]
    cw = cw_ref[...]
    xc = (cb_ref[...] + cw[0:1] * c0_ref[...] + cw[1:2] * c1_ref[...] + cw[2:3] * c2_ref[...]
          + cw[3:4] * xr)
    a, b = _rg_gates(xc, wrg_ref, ba_ref, bx_ref, lam_ref)
    h = a * h0_ref[...] + b
    h_ref[...] = h
    y_ref[...] = _rg_out(h, gr, nrm_ref).astype(BF16)


def _rg_sample(xg, c0, c1, c2, h0, cw, cb, wrg, ba, bx, lam, nrm):
    b = xg.shape[0]
    return pl.pallas_call(
        _rg_sample_kernel,
        out_shape=[jax.ShapeDtypeStruct((b, D_RNN), BF16), jax.ShapeDtypeStruct((b, D_RNN), F32)],
        name="rg_sample",
    )(xg, c0, c1, c2, h0, cw, cb, wrg, ba, bx, lam, nrm)


CMP_PITCH = 20
CMP_KSPLIT = 8


def _pitch_rows(n_groups):
    return -(-(CMP_PITCH * n_groups) // 8) * 8


def _compress_chunk(x_s, blk0, nblk, pek_ref, pev_ref, wk_ref, wv_ref, w2k_ref, w2v_ref):
    base = blk0 * CMP_PITCH
    outs = []
    for t, (pe_ref, w1_ref, w2_ref) in enumerate(((pek_ref, wk_ref, w2k_ref),
                                                   (pev_ref, wv_ref, w2v_ref))):
        h = None
        for l0 in range(0, CMP_LEN, CMP_KSPLIT):
            pieces = []
            for l in range(l0, l0 + CMP_KSPLIT):
                row = (l // CMP_STRIDE) * CMP_PITCH + l % CMP_STRIDE
                xl = x_s[t, pl.ds(base + row, nblk, stride=CMP_PITCH), :]
                pieces.append((xl + pe_ref[:, l * D_KV:(l + 1) * D_KV]).astype(BF16))
            part = jnp.dot(jnp.concatenate(pieces, axis=1), w1_ref[l0 * D_KV:(l0 + CMP_KSPLIT) * D_KV, :],
                           preferred_element_type=F32)
            h = part if h is None else h + part
        h = h * _sigmoid(h)
        outs.append(jnp.dot(h.astype(BF16), w2_ref[...], preferred_element_type=F32))
    return outs


CMP_CHUNK = 128
CMP_ROWS = CMP_CHUNK * CMP_STRIDE
CMP_PARTS = 2


def _cmp_prompt_kernel(x_ref, nxt_ref, pek_ref, pev_ref, wk_ref, wv_ref, w2k_ref, w2v_ref,
                       kct_ref, vc_ref, x_s):
    for t in range(2):
        cols = slice(t * D_KV, (t + 1) * D_KV)
        for j in range(CMP_CHUNK):
            x_s[t, CMP_PITCH * j:CMP_PITCH * j + CMP_STRIDE, :] = (
                x_ref[CMP_STRIDE * j:CMP_STRIDE * (j + 1), cols])
        x_s[t, CMP_PITCH * CMP_CHUNK:CMP_PITCH * CMP_CHUNK + CMP_STRIDE, :] = nxt_ref[:, cols]
    kc, vc = _compress_chunk(x_s, 0, CMP_CHUNK, pek_ref, pev_ref, wk_ref, wv_ref, w2k_ref, w2v_ref)
    kct_ref[...] = kc.T.astype(BF16)
    vc_ref[...] = vc.astype(BF16)


def _cmp_prompt(rows, pek, pev, wk, wv, w2k, w2v):
    t = rows.shape[0]
    n_steps = t // CMP_ROWS
    last16 = t // CMP_STRIDE - 1
    full = lambda a: pl.BlockSpec(a.shape, lambda i: (0, 0))
    return pl.pallas_call(
        _cmp_prompt_kernel,
        grid=(n_steps,),
        in_specs=[pl.BlockSpec((CMP_ROWS, 2 * D_KV), lambda i: (i, 0)),
                  pl.BlockSpec((CMP_STRIDE, 2 * D_KV),
                               lambda i: (jnp.minimum((i + 1) * CMP_CHUNK, last16), 0)),
                  full(pek), full(pev), full(wk), full(wv), full(w2k), full(w2v)],
        out_specs=[pl.BlockSpec((D_KV, CMP_CHUNK), lambda i: (0, i)),
                   pl.BlockSpec((CMP_CHUNK, D_KV), lambda i: (i, 0))],
        out_shape=[jax.ShapeDtypeStruct((D_KV, t // CMP_STRIDE), BF16),
                   jax.ShapeDtypeStruct((t // CMP_STRIDE, D_KV), BF16)],
        scratch_shapes=[pltpu.VMEM((2, _pitch_rows(CMP_CHUNK + 1), D_KV), F32)],
        compiler_params=pltpu.CompilerParams(dimension_semantics=("arbitrary",),
                                             vmem_limit_bytes=VMEM_LIMIT),
        name="cmp_prompt",
    )(rows, rows, pek, pev, wk, wv, w2k, w2v)


WIN_KEYS = WINDOW + Q_BLOCK
CMP_PREFIX = 256
HEAD_BLKS = LANE // SEL_BLOCK


def _sel_aug_rows():
    k = np.arange(SEL_CHUNK)
    c = np.zeros((HEAD_DIM, SEL_CHUNK), np.float32)
    c[AUG_HI - HEAD_DIM] = k // POS_SPLIT
    c[AUG_LO - HEAD_DIM] = k % POS_SPLIT
    for b in range(BLK_PER_CHUNK):
        c[AUG_PEN - HEAD_DIM + b] = (k // SEL_BLOCK == b)
    return jnp.asarray(c, dtype=BF16)


def _attn_prompt_kernel(q_ref, gt_ref, kst_ref, vs_ref, kwt_ref, vw_ref, kct_ref, vc_ref,
                        msel_ref, cst_ref, nrm_ref, o_ref, act_ref, m_s, l_s, acc_s, oc_s, sc_s):
    i = pl.program_id(0)
    t0 = i * Q_BLOCK
    n_cmp = kct_ref.shape[1]
    n_sel = msel_ref.shape[1]
    rows = GQA_R * Q_BLOCK
    dot = functools.partial(jnp.dot, preferred_element_type=F32)
    qpos = t0 + lax.broadcasted_iota(jnp.int32, (Q_BLOCK, 1), 0)
    qpos4 = jnp.concatenate([qpos] * GQA_R, axis=0)
    qposf4 = qpos4.astype(F32)
    cend = lax.broadcasted_iota(jnp.int32, (1, n_cmp), 1) * CMP_STRIDE + (CMP_LEN - 1)
    blk = lax.broadcasted_iota(jnp.int32, (1, n_sel), 1)
    blkf = blk.astype(F32)
    gates = _sigmoid(gt_ref[...])
    cur = qpos >> 6
    forced = (blk == 0) | (blk == cur) | (blk == cur - 1)
    in_past = blk * SEL_BLOCK <= qpos
    n_chunks = (t0 + Q_BLOCK + SEL_CHUNK - 1) // SEL_CHUNK
    win0 = pl.multiple_of(jnp.maximum(t0 - WINDOW, 0), LANE)
    wpos = win0 + lax.broadcasted_iota(jnp.int32, (1, WIN_KEYS), 1)
    wdist = qpos4 - wpos
    wmask = (wdist >= 0) & (wdist < WINDOW)
    wdistf = wdist.astype(F32)

    qa, q64, slope4 = [], [], []
    for g in range(N_KV):
        qg = jnp.concatenate([q_ref[:, (g * GQA_R + r) * Q_SLOT:(g * GQA_R + r + 1) * Q_SLOT]
                              for r in range(GQA_R)], axis=0)
        qa.append(qg)
        q64.append(qg[:, 0:HEAD_DIM])
        slope4.append(jnp.concatenate(
            [jnp.full((Q_BLOCK, 1), np.float32(2.0 ** -(g * GQA_R + r + 1)), F32)
             for r in range(GQA_R)], axis=0))

    def cmp_branch(n_used):
        cend_u = cend[:, 0:n_used]
        dist = qposf4 - cend_u.astype(F32)
        for g in range(N_KV):
            gs = slice(g * HEAD_DIM, (g + 1) * HEAD_DIM)
            s = dot(q64[g], kct_ref[gs, 0:n_used])
            p = _masked_softmax(s - slope4[g] * dist, cend_u <= qpos4)
            oc_s[g * rows:(g + 1) * rows, :] = dot(p.astype(BF16), vc_ref[0:n_used, :])
            psum = p[0:Q_BLOCK]
            for r in range(1, GQA_R):
                psum = psum + p[r * Q_BLOCK:(r + 1) * Q_BLOCK]
            p_slc = _split3_dot(psum, msel_ref[0:n_used, :])
            sc_s[g * Q_BLOCK:(g + 1) * Q_BLOCK, :] = jnp.where(
                in_past, jnp.where(forced, FORCE_SCORE, p_slc), -1.0)

    n_prefix = max(1, n_cmp // CMP_PREFIX)
    per = n_cmp // n_prefix
    need = (t0 + Q_BLOCK - CMP_LEN) // CMP_STRIDE + 1
    bucket = jnp.clip((need + per - 1) // per - 1, 0, n_prefix - 1)
    for k in range(n_prefix):
        @pl.when(bucket == k)
        def _(k=k):
            cmp_branch((k + 1) * per)

    o_c = [oc_s[g * rows:(g + 1) * rows, g * HEAD_DIM:(g + 1) * HEAD_DIM] for g in range(N_KV)]

    o_w = []
    for g in range(N_KV):
        gs = slice(g * HEAD_DIM, (g + 1) * HEAD_DIM)
        sw = dot(q64[g], kwt_ref[gs, pl.ds(win0, WIN_KEYS)])
        pw = _masked_softmax(sw - slope4[g] * wdistf, wmask)
        o_w.append(dot(pw.astype(BF16), vw_ref[pl.ds(win0, WIN_KEYS), :])[:, gs])

    idxs, vals = _topk_select(sc_s[...], blkf, min(TOP_N, n_sel))
    sel = jnp.zeros((N_KV * Q_BLOCK, n_sel), F32)
    for idx, val in zip(idxs, vals):
        sel = jnp.where((blkf == idx) & (val >= 0.0), 1.0, sel)
    pen_all = jnp.where(sel > 0.5, 0.0, NEG)
    head_lane = blk < HEAD_BLKS
    pen_head = pen_all.astype(BF16)
    pen = jnp.where(head_lane, NEG, pen_all).astype(BF16)
    penh = [pen_head[g * Q_BLOCK:(g + 1) * Q_BLOCK] for g in range(N_KV)]
    penb = [pen[g * Q_BLOCK:(g + 1) * Q_BLOCK] for g in range(N_KV)]
    sel = jnp.where(head_lane, 0.0, sel)
    n_chunks_all = n_sel // BLK_PER_CHUNK
    for g in range(N_KV):
        col_any = jnp.max(sel[g * Q_BLOCK:(g + 1) * Q_BLOCK], axis=0, keepdims=True)
        for c in range(n_chunks_all):
            hit = jnp.max(col_any[:, c * BLK_PER_CHUNK:(c + 1) * BLK_PER_CHUNK])
            act_ref[g * n_chunks_all + c] = (hit > 0.5).astype(jnp.int32)

    cst = cst_ref[...]
    oh_lane = lax.broadcasted_iota(jnp.int32, (n_sel, LANE), 1)
    oh_base = lax.broadcasted_iota(jnp.int32, (n_sel, LANE), 0) - oh_lane + AUG_PEN
    oh_ok = (oh_lane >= AUG_PEN) & (oh_lane < AUG_PEN + BLK_PER_CHUNK)

    def group_step(c, g, causal):
        k0 = pl.multiple_of(c * SEL_CHUNK, SEL_CHUNK)
        onehot = jnp.where((oh_base == c * BLK_PER_CHUNK) & oh_ok, 1.0, 0.0).astype(BF16)
        placed = dot(penb[g], onehot).astype(BF16)
        qaug = qa[g] + jnp.concatenate([placed] * GQA_R, axis=0)
        kta = jnp.concatenate(
            [kst_ref[g * HEAD_DIM:(g + 1) * HEAD_DIM, pl.ds(k0, SEL_CHUNK)], cst], axis=0)
        s = dot(qaug, kta)
        if causal:
            kpos = k0 + lax.broadcasted_iota(jnp.int32, (1, SEL_CHUNK), 1)
            tri = jnp.where(kpos > qpos, NEG, 0.0)
            s = s + jnp.concatenate([tri] * GQA_R, axis=0)
        crow = slope4[g] * (k0.astype(F32) - qposf4)
        rg = slice(g * rows, (g + 1) * rows)
        m = m_s[rg, :]
        m_new = jnp.maximum(m, jnp.max(s, axis=-1, keepdims=True) + crow)
        p = jnp.exp(s + (crow - m_new))
        alpha = jnp.exp(m - m_new)
        return (m_new, alpha * l_s[rg, :] + jnp.sum(p, axis=-1, keepdims=True),
                alpha * acc_s[rg, :] + dot(p.astype(BF16), vs_ref[pl.ds(k0, SEL_CHUNK), :]))

    def fold(c, groups, causal):
        new = [group_step(c, g, causal) for g in groups]
        rg = slice(groups[0] * rows, (groups[-1] + 1) * rows)
        for ref, k in ((m_s, 0), (l_s, 1), (acc_s, 2)):
            ref[rg, :] = jnp.concatenate([n[k] for n in new], axis=0)

    head_keys = HEAD_BLKS * SEL_BLOCK
    head_oh = jnp.where((oh_base == 0) & (oh_lane >= AUG_PEN) & (oh_lane < AUG_PEN + HEAD_BLKS),
                        1.0, 0.0).astype(BF16)
    head_pos = lax.broadcasted_iota(jnp.int32, (1, head_keys), 1)
    head_tri = jnp.concatenate([jnp.where(head_pos > qpos, NEG, 0.0)] * GQA_R, axis=0)
    head_new = []
    for g in range(N_KV):
        placed = dot(penh[g], head_oh).astype(BF16)
        qaug = qa[g] + jnp.concatenate([placed] * GQA_R, axis=0)
        kta = jnp.concatenate([kst_ref[g * HEAD_DIM:(g + 1) * HEAD_DIM, 0:head_keys],
                               cst[:, 0:head_keys]], axis=0)
        s = dot(qaug, kta) + head_tri
        crow = -slope4[g] * qposf4
        m0 = jnp.max(s, axis=-1, keepdims=True) + crow
        p = jnp.exp(s + (crow - m0))
        head_new.append((m0, jnp.sum(p, axis=-1, keepdims=True),
                         dot(p.astype(BF16), vs_ref[0:head_keys, :])))
    for ref, k in ((m_s, 0), (l_s, 1), (acc_s, 2)):
        ref[...] = jnp.concatenate([n[k] for n in head_new], axis=0)

    def past_chunk(c, carry):
        a0 = act_ref[c] > 0
        a1 = act_ref[n_chunks_all + c] > 0

        @pl.when(a0 & a1)
        def _():
            fold(c, (0, 1), causal=False)

        @pl.when(a0 & jnp.logical_not(a1))
        def _():
            fold(c, (0,), causal=False)

        @pl.when(a1 & jnp.logical_not(a0))
        def _():
            fold(c, (1,), causal=False)

        return carry

    lax.fori_loop(0, n_chunks - 1, past_chunk, 0)
    fold(n_chunks - 1, (0, 1), causal=True)

    head_out = []
    for g in range(N_KV):
        gs = slice(g * HEAD_DIM, (g + 1) * HEAD_DIM)
        rg = slice(g * rows, (g + 1) * rows)
        o_s = (acc_s[rg, :] * (1.0 / jnp.maximum(l_s[rg, :], 1e-30)))[:, gs]
        for r in range(GQA_R):
            h = g * GQA_R + r
            rs = slice(r * Q_BLOCK, (r + 1) * Q_BLOCK)
            head_out.append(o_c[g][rs] * gates[:, 3 * h:3 * h + 1]
                            + o_s[rs] * gates[:, 3 * h + 1:3 * h + 2]
                            + o_w[g][rs] * gates[:, 3 * h + 2:3 * h + 3])
    y = jnp.concatenate(head_out, axis=1)
    y = (y * lax.rsqrt(jnp.mean(y * y, axis=-1, keepdims=True) + 1e-6)) * nrm_ref[...]
    o_ref[...] = y.astype(BF16)


def _attn_prompt(q, gates, kst, vs, kwt, vw, kct, vc, msel, nrm):
    t = q.shape[0]
    cst = _sel_aug_rows()
    full = lambda a: pl.BlockSpec(a.shape, lambda i: (0, 0), pipeline_mode=pl.Buffered(1))
    return pl.pallas_call(
        _attn_prompt_kernel,
        grid=(t // Q_BLOCK,),
        in_specs=[pl.BlockSpec((Q_BLOCK, Q_PAD), lambda i: (i, 0)),
                  pl.BlockSpec((Q_BLOCK, GATE_PAD), lambda i: (i, 0)),
                  full(kst), full(vs), full(kwt), full(vw), full(kct), full(vc), full(msel),
                  full(cst), full(nrm)],
        out_specs=pl.BlockSpec((Q_BLOCK, D_Q), lambda i: (i, 0)),
        out_shape=jax.ShapeDtypeStruct((t, D_Q), BF16),
        scratch_shapes=[pltpu.SMEM((N_KV * (t // SEL_CHUNK),), jnp.int32),
                        pltpu.VMEM((N_KV * GQA_R * Q_BLOCK, 1), F32),
                        pltpu.VMEM((N_KV * GQA_R * Q_BLOCK, 1), F32),
                        pltpu.VMEM((N_KV * GQA_R * Q_BLOCK, D_KV), F32),
                        pltpu.VMEM((N_KV * GQA_R * Q_BLOCK, D_KV), F32),
                        pltpu.VMEM((N_KV * Q_BLOCK, t // SEL_BLOCK), F32)],
        compiler_params=pltpu.CompilerParams(dimension_semantics=("arbitrary",),
                                             vmem_limit_bytes=VMEM_LIMIT),
        name="attn_prompt",
    )(q, gates, kst, vs, kwt, vw, kct, vc, msel, cst, nrm)


FF_CHUNK = D_FF // 2


def _layer_norm(x, g, b):
    mu = jnp.mean(x, axis=-1, keepdims=True)
    xc = x - mu
    var = jnp.mean(xc * xc, axis=-1, keepdims=True)
    return (xc * lax.rsqrt(var + 1e-5)) * g + b


def _ffn_kernel(x_ref, hr_ref, ha_ref, g1_ref, sh2_ref, sc2_ref, g2_ref, wo_ref, l1g_ref, l1b_ref,
                wup_ref, wdn_ref, l2g_ref, l2b_ref, o_ref, *, alpha):
    d = functools.partial(jnp.dot, preferred_element_type=F32)
    mix = d(hr_ref[...], wo_ref[0:D_RNN, :]) + d(ha_ref[...], wo_ref[D_RNN:D_MODEL, :])
    x1 = _layer_norm(alpha * x_ref[...] + g1_ref[...] * mix, l1g_ref[...], l1b_ref[...])
    u = (x1 * (1.0 + sc2_ref[...]) + sh2_ref[...]).astype(BF16)
    f = None
    for c in range(0, D_FF, FF_CHUNK):
        gate = d(u, wup_ref[:, c:c + FF_CHUNK])
        up = d(u, wup_ref[:, D_FF + c:D_FF + c + FF_CHUNK])
        part = d((gate * _sigmoid(gate) * up).astype(BF16), wdn_ref[c:c + FF_CHUNK, :])
        f = part if f is None else f + part
    o_ref[...] = _layer_norm(alpha * x1 + g2_ref[...] * f, l2g_ref[...], l2b_ref[...])


def _merge_ffn(x2d, hr, ha, g1, sh2, sc2, g2, wo, l1g, l1b, wup, wdn, l2g, l2b, tm, alpha):
    r = x2d.shape[0]
    rm = g1.shape[0]
    mod_spec = (pl.BlockSpec((1, D_MODEL), lambda i: (0, 0)) if rm == 1
                else pl.BlockSpec((tm, D_MODEL), lambda i: (i, 0)))
    vec = pl.BlockSpec((1, D_MODEL), lambda i: (0, 0))
    full = lambda a: pl.BlockSpec(a.shape, lambda i: (0, 0), pipeline_mode=pl.Buffered(1))
    return pl.pallas_call(
        functools.partial(_ffn_kernel, alpha=alpha),
        grid=(r // tm,),
        in_specs=[pl.BlockSpec((tm, D_MODEL), lambda i: (i, 0)),
                  pl.BlockSpec((tm, D_RNN), lambda i: (i, 0)),
                  pl.BlockSpec((tm, D_Q), lambda i: (i, 0)),
                  mod_spec, mod_spec, mod_spec, mod_spec,
                  full(wo), vec, vec, full(wup), full(wdn), vec, vec],
        out_specs=pl.BlockSpec((tm, D_MODEL), lambda i: (i, 0)),
        out_shape=jax.ShapeDtypeStruct((r, D_MODEL), F32),
        compiler_params=pltpu.CompilerParams(dimension_semantics=("arbitrary",),
                                             vmem_limit_bytes=VMEM_LIMIT),
        name="merge_ffn",
    )(x2d, hr, ha, g1, sh2, sc2, g2, wo, l1g, l1b, wup, wdn, l2g, l2b)


CACHE_ROWS_PER_PAGE = 4 * N_KV * HEAD_DIM
CMP_ROWS_PER_PAGE = 2 * N_KV * HEAD_DIM


def _sample_cmp_kernel(pt_ref, cache_ref, q_ref, new_ref, pek_ref, pev_ref, wk_ref, wv_ref,
                       w2k_ref, w2v_ref, msel_ref, oc_ref, pslc_ref, buf, x_s, kc_s, vc_s, sem,
                       *, n_pages):
    b = pl.program_id(0)
    nb = pl.num_programs(0)
    past_len = n_pages * PAGE_SIZE
    n_cmp = past_len // CMP_STRIDE

    def page_copy(bb, p, slot):
        pool = pt_ref[bb * n_pages + p]
        return pltpu.make_async_copy(
            cache_ref.at[pl.ds(pool * CACHE_ROWS_PER_PAGE, CMP_ROWS_PER_PAGE), :],
            buf.at[slot, p], sem.at[slot])

    def start_all(bb, slot):
        for p in range(n_pages):
            page_copy(bb, p, slot).start()

    @pl.when(b == 0)
    def _():
        start_all(0, 0)

    slot = b % 2

    @pl.when(b + 1 < nb)
    def _():
        start_all(b + 1, 1 - slot)

    for p in range(n_pages):
        page_copy(b, p, slot).wait()

    groups_per_page = PAGE_SIZE // CMP_STRIDE

    def to_rows(p):
        r0 = p * (groups_per_page * CMP_PITCH)
        for t in range(2):
            tile = buf[slot, p, t * D_KV:(t + 1) * D_KV, :].T
            for j in range(groups_per_page):
                x_s[t, r0 + CMP_PITCH * j:r0 + CMP_PITCH * j + CMP_STRIDE, :] = (
                    tile[CMP_STRIDE * j:CMP_STRIDE * (j + 1)])

    def compress(blk0, nblk):
        kc, vc = _compress_chunk(x_s, blk0, nblk, pek_ref, pev_ref, wk_ref, wv_ref, w2k_ref,
                                 w2v_ref)
        kc_s[blk0:blk0 + nblk, :] = kc
        vc_s[blk0:blk0 + nblk, :] = vc

    part_pages = n_pages // CMP_PARTS
    done = 0
    for k in range(CMP_PARTS):
        for p in range(k * part_pages, (k + 1) * part_pages):
            to_rows(p)
        if k == CMP_PARTS - 1:
            tail_row = lax.broadcasted_iota(jnp.int32, (CMP_STRIDE, D_KV), 0)
            for t in range(2):
                x_s[t, CMP_PITCH * n_cmp:CMP_PITCH * n_cmp + CMP_STRIDE, :] = jnp.where(
                    tail_row == 0, new_ref[0][:, t * D_KV:(t + 1) * D_KV], 0.0)
            upto = n_cmp
        else:
            upto = (k + 1) * part_pages * groups_per_page - 16
        compress(done, upto - done)
        done = upto

    q = q_ref[0]
    kcb = kc_s[...].astype(BF16)
    vcb = vc_s[...].astype(BF16)
    row = lax.broadcasted_iota(jnp.int32, (N_HEADS, 1), 0)
    first = row < GQA_R
    nt = (((1,), (1,)), ((), ()))
    s0 = lax.dot_general(q, kcb[:, 0:HEAD_DIM], nt, preferred_element_type=F32)
    s1 = lax.dot_general(q, kcb[:, HEAD_DIM:D_KV], nt, preferred_element_type=F32)
    cend = lax.broadcasted_iota(jnp.int32, (1, n_cmp), 1) * CMP_STRIDE + (CMP_LEN - 1)
    s = jnp.where(first, s0, s1) - _pow2_neg(row + 1) * (past_len - cend).astype(F32)
    p = _masked_softmax(s, cend <= past_len)
    o = jnp.dot(p.astype(BF16), vcb, preferred_element_type=F32)
    oc_ref[0] = jnp.where(first, o[:, 0:HEAD_DIM], o[:, HEAD_DIM:D_KV])
    psum = jnp.concatenate([jnp.sum(p[0:GQA_R], axis=0, keepdims=True),
                            jnp.sum(p[GQA_R:N_HEADS], axis=0, keepdims=True)], axis=0)
    pslc_ref[0] = _split3_dot(psum, msel_ref[...])


def _sample_cmp(pt_flat, cache2d, q3, new3, pek, pev, wk, wv, w2k, w2v, msel, n_pages):
    b = q3.shape[0]
    past_len = n_pages * PAGE_SIZE
    n_cmp = past_len // CMP_STRIDE
    full = lambda a: pl.BlockSpec(a.shape, lambda i, pt: (0,) * a.ndim)
    return pl.pallas_call(
        functools.partial(_sample_cmp_kernel, n_pages=n_pages),
        grid_spec=pltpu.PrefetchScalarGridSpec(
            num_scalar_prefetch=1, grid=(b,),
            in_specs=[pl.BlockSpec(memory_space=pl.ANY),
                      pl.BlockSpec((1, N_HEADS, HEAD_DIM), lambda i, pt: (i, 0, 0)),
                      pl.BlockSpec((1, 1, 6 * D_KV), lambda i, pt: (i, 0, 0)),
                      full(pek), full(pev), full(wk), full(wv), full(w2k), full(w2v), full(msel)],
            out_specs=[pl.BlockSpec((1, N_HEADS, HEAD_DIM), lambda i, pt: (i, 0, 0)),
                       pl.BlockSpec((1, N_KV, msel.shape[1]), lambda i, pt: (i, 0, 0))],
            scratch_shapes=[pltpu.VMEM((2, n_pages, CMP_ROWS_PER_PAGE, PAGE_SIZE), F32),
                            pltpu.VMEM((2, _pitch_rows(n_cmp + 1), D_KV), F32),
                            pltpu.VMEM((n_cmp, D_KV), F32), pltpu.VMEM((n_cmp, D_KV), F32),
                            pltpu.SemaphoreType.DMA((2,))]),
        out_shape=[jax.ShapeDtypeStruct((b, N_HEADS, HEAD_DIM), F32),
                   jax.ShapeDtypeStruct((b, N_KV, msel.shape[1]), F32)],
        compiler_params=pltpu.CompilerParams(dimension_semantics=("arbitrary",),
                                             vmem_limit_bytes=VMEM_LIMIT),
        name="sample_cmp",
    )(pt_flat, cache2d, q3, new3, pek, pev, wk, wv, w2k, w2v, msel)


META_W = 128


def _sample_topk_kernel(pslc_ref, meta_ref, *, past_len):
    n_lane = pslc_ref.shape[1]
    n_sel = -(-(past_len + 1) // SEL_BLOCK)
    blk = lax.broadcasted_iota(jnp.int32, (1, n_lane), 1)
    cur = past_len // SEL_BLOCK
    forced = (blk == 0) | (blk == cur) | (blk == cur - 1)
    score = jnp.where(blk * SEL_BLOCK <= past_len,
                      jnp.where(forced, FORCE_SCORE, pslc_ref[...]), -1.0)
    score = jnp.where(blk < n_sel, score, -3.0)
    idxs, vals = _topk_select(score, blk.astype(F32), min(TOP_N, n_sel))
    lane = lax.broadcasted_iota(jnp.int32, (pslc_ref.shape[0], META_W), 1)
    meta = jnp.zeros((pslc_ref.shape[0], META_W), F32)
    for it, (idx, val) in enumerate(zip(idxs, vals)):
        meta = jnp.where(lane == it, idx, meta)
        meta = jnp.where(lane == TOP_N + it, jnp.where(val >= 0.0, 1.0, 0.0), meta)
    meta_ref[...] = meta


def _sample_topk(pslc2d, past_len):
    return pl.pallas_call(
        functools.partial(_sample_topk_kernel, past_len=past_len),
        out_shape=jax.ShapeDtypeStruct((pslc2d.shape[0], META_W), F32),
        name="sample_topk",
    )(pslc2d)


SEL_KEYS = TOP_N * PAGE_SIZE


def _sample_attn_kernel(koff_ref, voff_ref, cache_ref, q_ref, meta_ref, new_ref, newt_ref, win_ref,
                        oc_ref, gt_ref, nrm_ref, e16_ref, ha_ref, wout_ref, kbuf, vbuf, sem,
                        *, past_len):
    b = pl.program_id(0)
    nb = pl.num_programs(0)

    def tile_copies(bb, slot):
        cps = []
        for g in range(N_KV):
            for n in range(TOP_N):
                i = (bb * N_KV + g) * TOP_N + n
                dst = pl.ds(n * PAGE_SIZE, PAGE_SIZE)
                cps.append(pltpu.make_async_copy(cache_ref.at[pl.ds(koff_ref[i], HEAD_DIM), :],
                                                 kbuf.at[slot, g, :, dst], sem.at[slot]))
                cps.append(pltpu.make_async_copy(cache_ref.at[pl.ds(voff_ref[i], HEAD_DIM), :],
                                                 vbuf.at[slot, g, :, dst], sem.at[slot]))
        return cps

    @pl.when(b == 0)
    def _():
        for cp in tile_copies(0, 0):
            cp.start()

    slot = b % 2

    @pl.when(b + 1 < nb)
    def _():
        for cp in tile_copies(b + 1, 1 - slot):
            cp.start()

    for cp in tile_copies(b, slot):
        cp.wait()

    q = q_ref[0]
    qf = q.astype(F32)
    row = lax.broadcasted_iota(jnp.int32, (N_HEADS, 1), 0)
    first = row < GQA_R
    slope = _pow2_neg(row + 1)
    new = new_ref[0]
    nt = (((1,), (1,)), ((), ()))

    def new_rows(off):
        a = new[:, off:off + HEAD_DIM]
        c = new[:, off + HEAD_DIM:off + D_KV]
        v = jnp.where(first, jnp.broadcast_to(a, (N_HEADS, HEAD_DIM)),
                      jnp.broadcast_to(c, (N_HEADS, HEAD_DIM)))
        return v.astype(BF16).astype(F32)

    def attend(s_buf, mask_buf, v_of_p, s_new, new_on, v_new):
        s_buf = jnp.where(mask_buf, s_buf, NEG)
        s_new = jnp.where(new_on, s_new, NEG)
        m = jnp.maximum(jnp.max(s_buf, axis=-1, keepdims=True), s_new)
        p = jnp.where(mask_buf, jnp.exp(s_buf - m), 0.0)
        p_new = jnp.where(new_on, jnp.exp(s_new - m), 0.0)
        den = jnp.maximum(jnp.sum(p, axis=-1, keepdims=True) + p_new, 1e-30)
        p = p / den
        p_new = (p_new / den).astype(BF16).astype(F32)
        return v_of_p(p.astype(BF16)) + p_new * v_new

    meta = meta_ref[0]
    e16 = e16_ref[...]
    jexp = jnp.dot(meta[:, 0:TOP_N].astype(BF16), e16, preferred_element_type=F32)
    vexp = jnp.dot(meta[:, TOP_N:2 * TOP_N].astype(BF16), e16, preferred_element_type=F32)
    lane = lax.broadcasted_iota(jnp.int32, (1, SEL_KEYS), 1) & (PAGE_SIZE - 1)
    ji = jexp.astype(jnp.int32)
    kpos = (ji >> 1) * PAGE_SIZE + lane
    key_ok = (vexp > 0.5) & ((kpos >> 6) == ji) & (kpos < past_len)
    new_blk = past_len // SEL_BLOCK
    new_sel = jnp.max(jnp.where((meta[:, 0:TOP_N] == float(new_blk)) & (meta[:, TOP_N:2 * TOP_N] > 0.5),
                                1.0, 0.0), axis=-1, keepdims=True)
    o_sel = []
    for g in range(N_KV):
        sb = jnp.dot(q, kbuf[slot, g].astype(BF16), preferred_element_type=F32)
        sb = sb - slope * (past_len - kpos[g:g + 1]).astype(F32)
        vb = vbuf[slot, g].astype(BF16)
        o_sel.append((sb, key_ok[g:g + 1], vb))
    k_new = new_rows(2 * D_KV)
    v_new = new_rows(3 * D_KV)
    s_new = jnp.sum(qf * k_new, axis=-1, keepdims=True)
    new_on = jnp.where(first, new_sel[0:1], new_sel[1:2]) > 0.5
    outs = [attend(sb, ok, lambda pb, vb=vb: lax.dot_general(pb, vb, nt, preferred_element_type=F32),
                   s_new, new_on, v_new) for sb, ok, vb in o_sel]
    o_s = jnp.where(first, outs[0], outs[1])

    w = win_ref[0]
    wbuf = w.shape[1]
    lane_b = lax.broadcasted_iota(jnp.int32, (4 * HEAD_DIM, newt_ref.shape[1]), 1)
    new_col = jnp.sum(jnp.where(lane_b == b, newt_ref[...], 0.0), axis=-1, keepdims=True)
    lane_w = lax.broadcasted_iota(jnp.int32, (1, wbuf), 1)
    w_new = jnp.where(lane_w == wbuf - 1, new_col, pltpu.roll(w, wbuf - 1, 1))
    wout_ref[0] = w_new
    wdist = (wbuf - 1 - lane_w).astype(F32)
    wb = w_new.astype(BF16)
    ow = []
    for g in range(N_KV):
        sw = jnp.dot(q, wb[g * HEAD_DIM:(g + 1) * HEAD_DIM], preferred_element_type=F32)
        pw = _masked_softmax(sw - slope * wdist, lane_w >= 0)
        ow.append(lax.dot_general(pw.astype(BF16), wb[D_KV + g * HEAD_DIM:D_KV + (g + 1) * HEAD_DIM],
                                  nt, preferred_element_type=F32))
    o_w = jnp.where(first, ow[0], ow[1])

    gates = _sigmoid(gt_ref[0])
    y = oc_ref[0] * gates[:, 0:1] + o_s * gates[:, 1:2] + o_w * gates[:, 2:3]
    ms = jnp.sum(jnp.sum(y * y, axis=-1, keepdims=True), axis=0, keepdims=True) / D_Q
    ha_ref[0] = (y * lax.rsqrt(ms + 1e-6)) * nrm_ref[...]


def _sample_attn(koff, voff, cache2d, q3, meta3, new3, newt, win3, oc3, gt3, nrm8, e16, past_len):
    b = q3.shape[0]
    wbuf = win3.shape[2]
    full = lambda a: pl.BlockSpec(a.shape, lambda i, ko, vo: (0,) * a.ndim)
    per_b = lambda a: pl.BlockSpec((1,) + a.shape[1:], lambda i, ko, vo: (i,) + (0,) * (a.ndim - 1))
    return pl.pallas_call(
        functools.partial(_sample_attn_kernel, past_len=past_len),
        grid_spec=pltpu.PrefetchScalarGridSpec(
            num_scalar_prefetch=2, grid=(b,),
            in_specs=[pl.BlockSpec(memory_space=pl.ANY), per_b(q3), per_b(meta3), per_b(new3),
                      full(newt), per_b(win3), per_b(oc3), per_b(gt3), full(nrm8), full(e16)],
            out_specs=[pl.BlockSpec((1, N_HEADS, HEAD_DIM), lambda i, ko, vo: (i, 0, 0)),
                       pl.BlockSpec((1, 4 * HEAD_DIM, wbuf), lambda i, ko, vo: (i, 0, 0))],
            scratch_shapes=[pltpu.VMEM((2, N_KV, HEAD_DIM, SEL_KEYS), F32),
                            pltpu.VMEM((2, N_KV, HEAD_DIM, SEL_KEYS), F32),
                            pltpu.SemaphoreType.DMA((2,))]),
        out_shape=[jax.ShapeDtypeStruct((b, N_HEADS, HEAD_DIM), F32),
                   jax.ShapeDtypeStruct((b, 4 * HEAD_DIM, wbuf), F32)],
        compiler_params=pltpu.CompilerParams(dimension_semantics=("arbitrary",),
                                             vmem_limit_bytes=VMEM_LIMIT),
        name="sample_attn",
    )(koff, voff, cache2d, q3, meta3, new3, newt, win3, oc3, gt3, nrm8, e16)


def _block_diag(w):
    n, a, b = w.shape
    eye = jnp.eye(n, dtype=w.dtype)
    return (eye[:, None, :, None] * w[:, :, None, :]).reshape(n * a, n * b)


def _cmp_weights(pe, w1, w2):
    pe_cat = jnp.tile(pe, (1, N_KV)).reshape(1, CMP_LEN * D_KV)
    eye = jnp.eye(N_KV, dtype=w1.dtype)
    w1_big = (w1[:, None, :, None, :] * eye[None, :, None, :, None]).reshape(
        CMP_LEN * D_KV, N_KV * CMP_HID)
    w2_big = (w2[None, :, None, :] * eye[:, None, :, None]).reshape(N_KV * CMP_HID, D_KV)
    return pe_cat, w1_big.astype(BF16), w2_big.astype(BF16)


def _sel_matrix(n_cmp, n_lane):
    n = np.arange(n_cmp)[:, None]
    j = np.arange(n_lane)[None, :]
    lo = SEL_RATIO * j - (CMP_LEN // CMP_STRIDE - 1)
    return jnp.asarray(((n >= lo) & (n <= lo + SEL_RATIO)).astype(np.float32), dtype=BF16)


def kernel(x_prompt, x_sample, cache_kv, state_win, state_conv, state_h, page_table, c_prompt, c_sample, w_ada, b_ada, w_in, conv_w, conv_b, rg_wa, rg_ba, rg_wx, rg_bx, rg_lam, cmp_pe_k, cmp_w1_k, cmp_w2_k, cmp_pe_v, cmp_w1_v, cmp_w2_v, norm_rg, norm_attn, w_out, ln1_g, ln1_b, w_up, w_down, ln2_g, ln2_b):
    depth = w_in.shape[0]
    alpha = float((2.0 * depth) ** 0.25)
    bp, t, _ = x_prompt.shape
    assert bp == 1
    bs = x_sample.shape[0]
    assert x_sample.shape[1] == 1
    n_pages = page_table.shape[1]
    past_len = n_pages * PAGE_SIZE
    n_pool = cache_kv.shape[1]
    wbuf = state_win.shape[2]
    assert wbuf == WINDOW and t % CMP_ROWS == 0 and t % SEL_CHUNK == 0 and t >= WIN_KEYS
    assert n_pages % CMP_PARTS == 0 and (n_pages // CMP_PARTS) * PAGE_SIZE >= 2 * 16 * CMP_STRIDE

    xp = x_prompt.reshape(t, D_MODEL)
    xs = x_sample.reshape(bs, D_MODEL)
    pt_flat = page_table.reshape(-1)
    r_mod = -(-(bs + 1) // 8) * 8
    c_all = jnp.zeros((r_mod, D_MODEL), F32).at[0:bs].set(c_sample).at[bs:bs + 1].set(c_prompt)
    vec = lambda a: a.reshape(1, -1)

    outs = [[] for _ in range(8)]
    for l in range(depth):
        mod = _modulation(c_all, w_ada[l], b_ada[l])
        mod_s = [mod[0:bs, k * D_MODEL:(k + 1) * D_MODEL] for k in range(6)]
        mod_p = [mod[bs:bs + 1, k * D_MODEL:(k + 1) * D_MODEL] for k in range(6)]
        w_l = w_in[l]
        w_q = jnp.pad(w_l[:, 2 * D_RNN:2 * D_RNN + D_Q].reshape(D_MODEL, N_HEADS, HEAD_DIM),
                      ((0, 0), (0, 0), (0, Q_SLOT - HEAD_DIM))).reshape(D_MODEL, Q_PAD)
        w_in_b = jnp.concatenate(
            [w_l[:, 0:2 * D_RNN], w_q, w_l[:, 2 * D_RNN + D_Q:],
             jnp.zeros((D_MODEL, GATE_PAD - D_GATE), F32)], axis=1).astype(BF16)
        wrg = jnp.concatenate([_block_diag(rg_wa[l]), _block_diag(rg_wx[l])], axis=1).astype(BF16)
        pek, wk, w2k = _cmp_weights(cmp_pe_k[l], cmp_w1_k[l], cmp_w2_k[l])
        pev, wv, w2v = _cmp_weights(cmp_pe_v[l], cmp_w1_v[l], cmp_w2_v[l])
        wo_b, wup_b, wdn_b = w_out[l].astype(BF16), w_up[l].astype(BF16), w_down[l].astype(BF16)
        rg_args = (conv_w[l], vec(conv_b[l]), wrg, vec(rg_ba[l]), vec(rg_bx[l]), vec(rg_lam[l]),
                   vec(norm_rg[l]))
        ffn_w = (wo_b, vec(ln1_g[l]), vec(ln1_b[l]), wup_b, wdn_b, vec(ln2_g[l]), vec(ln2_b[l]))

        xg, q, rows, gts, vs, vw, kvt, kst, kwt = _in_proj(xp, mod_p[0], mod_p[1], w_in_b, 512,
                                                           2 * D_KV)
        hr, tail, hl = _rg_prompt(xg, *rg_args, tc=256)
        kct, vc = _cmp_prompt(rows, pek, pev, wk, wv, w2k, w2v)
        msel_p = _sel_matrix(t // CMP_STRIDE, t // SEL_BLOCK)
        ha = _attn_prompt(q, gts, kst, vs, kwt, vw, kct, vc, msel_p, vec(norm_attn[l]))
        xp = _merge_ffn(xp, hr, ha, mod_p[2], mod_p[3], mod_p[4], mod_p[5], *ffn_w,
                        tm=512, alpha=alpha)
        outs[0].append(kvt[0:4 * D_KV].reshape(4, N_KV, HEAD_DIM, t).transpose(3, 0, 1, 2)[None])
        outs[2].append(kvt[4 * D_KV:, t - WINDOW:].reshape(2, N_KV, HEAD_DIM, WINDOW)
                       .transpose(3, 0, 1, 2)[None])
        outs[4].append(tail[8 - (CONV_W - 1):][None])
        outs[6].append(hl[0:1])

        xg, q, rows, gts, _, _, kvt, _, _ = _in_proj(xs, mod_s[0], mod_s[1], w_in_b, bs, 6 * D_KV)
        sconv = state_conv[l]
        hr, h_new = _rg_sample(xg, sconv[:, 0], sconv[:, 1], sconv[:, 2], state_h[l], *rg_args)
        cache2d = cache_kv[l].transpose(0, 2, 3, 4, 1).reshape(n_pool * CACHE_ROWS_PER_PAGE, PAGE_SIZE)
        n_lane = -(-(past_len // SEL_BLOCK + 1) // LANE) * LANE
        msel_s = _sel_matrix(past_len // CMP_STRIDE, n_lane)
        q3 = q.reshape(bs, N_HEADS, Q_SLOT)[:, :, 0:HEAD_DIM]
        new3 = rows.reshape(bs, 1, 6 * D_KV)
        oc3, pslc = _sample_cmp(pt_flat, cache2d, q3, new3, pek, pev, wk, wv, w2k, w2v, msel_s, n_pages)
        meta = _sample_topk(pslc.reshape(bs * N_KV, n_lane), past_len)
        picks = meta[:, 0:TOP_N].astype(jnp.int32).reshape(bs, N_KV * TOP_N)
        pool = jnp.take_along_axis(page_table, jnp.minimum(picks // 2, n_pages - 1), axis=1)
        grp = (jnp.arange(N_KV * TOP_N, dtype=jnp.int32) // TOP_N)[None, :]
        koff = (pool * CACHE_ROWS_PER_PAGE + (2 * N_KV + grp) * HEAD_DIM).reshape(-1)
        voff = koff + N_KV * HEAD_DIM
        win3 = state_win[l].transpose(0, 2, 3, 4, 1).reshape(bs, 4 * HEAD_DIM, wbuf)
        e16 = jnp.asarray(np.kron(np.eye(TOP_N, dtype=np.float32), np.ones((1, PAGE_SIZE), np.float32)),
                          dtype=BF16)
        ha3, wnew = _sample_attn(koff, voff, cache2d, q3, meta.reshape(bs, N_KV, META_W), new3,
                                 kvt[4 * D_KV:], win3, oc3, gts[:, 0:D_GATE].reshape(bs, N_HEADS, 3),
                                 norm_attn[l].reshape(N_HEADS, HEAD_DIM), e16, past_len)
        ha = ha3.reshape(bs, D_Q).astype(BF16)
        xs = _merge_ffn(xs, hr, ha, mod_s[2], mod_s[3], mod_s[4], mod_s[5], *ffn_w,
                        tm=bs, alpha=alpha)
        outs[1].append(kvt[0:4 * D_KV].reshape(4, N_KV, HEAD_DIM, bs).transpose(3, 0, 1, 2)[:, None])
        outs[3].append(wnew.reshape(bs, 2, N_KV, HEAD_DIM, wbuf).transpose(0, 4, 1, 2, 3))
        outs[5].append(jnp.stack([sconv[:, 1], sconv[:, 2], xg[:, 0:D_RNN]], axis=1))
        outs[7].append(h_new)

    stk = [jnp.stack(o) for o in outs]
    return (xp.reshape(1, t, D_MODEL), xs.reshape(bs, 1, D_MODEL), stk[0], stk[1], stk[2], stk[3],
            stk[4], stk[5], stk[6], stk[7])
```

```python
import functools

import numpy as np
import jax
import jax.numpy as jnp
from jax import lax
from jax.experimental import pallas as pl
from jax.experimental.pallas import tpu as pltpu

F32 = jnp.float32
BF16 = jnp.bfloat16

D_MODEL = 1024
D_RNN = D_MODEL // 2
RNN_BLOCKS = 8
RNN_BLOCK = D_RNN // RNN_BLOCKS
CONV_W = 4
RG_C = 8.0
HEAD_DIM = 64
N_HEADS = (D_MODEL - D_RNN) // HEAD_DIM
N_KV = 2
GQA_R = N_HEADS // N_KV
D_Q = N_HEADS * HEAD_DIM
D_KV = N_KV * HEAD_DIM
D_GATE = 3 * N_HEADS
CMP_STRIDE = 16
CMP_LEN = 2 * CMP_STRIDE
CMP_HID = 128
SEL_BLOCK = 64
SEL_RATIO = SEL_BLOCK // CMP_STRIDE
TOP_N = 16
WINDOW = 512
Q_BLOCK = 128
FORCE_SCORE = 1.0e4
PAGE_SIZE = 128
D_FF = ((8 * D_MODEL // 3 + 255) // 256) * 256
SCALE = HEAD_DIM ** -0.5

NEG = -1e30
LANE = 128

Q_SLOT = LANE
Q_PAD = N_HEADS * Q_SLOT
OFF_Q = 2 * D_RNN
OFF_KV = OFF_Q + Q_PAD
OFF_GATE = OFF_KV + 6 * D_KV
GATE_PAD = 128
D_IN_PAD = OFF_GATE + GATE_PAD

SEL_CHUNK = 1024
BLK_PER_CHUNK = SEL_CHUNK // SEL_BLOCK
POS_SPLIT = 16
AUG_HI = HEAD_DIM
AUG_LO = HEAD_DIM + 1
AUG_PEN = HEAD_DIM + 2
VMEM_LIMIT = 56 * 1024 * 1024


def _sigmoid(x):
    return 1.0 / (1.0 + jnp.exp(-x))


def _gelu_tanh(x):
    c = np.float32(np.sqrt(2.0 / np.pi))
    return 0.5 * x * (1.0 + jnp.tanh(c * (x + np.float32(0.044715) * (x * x * x))))


def _softplus(x):
    return jnp.maximum(x, 0.0) + jnp.log1p(jnp.exp(-jnp.abs(x)))


def _pow2_neg(e_int):
    return lax.bitcast_convert_type((127 - e_int) << 23, F32)


def _masked_softmax(s, mask):
    s = jnp.where(mask, s, NEG)
    m = jnp.max(s, axis=-1, keepdims=True)
    p = jnp.exp(s - m)
    den = jnp.maximum(jnp.sum(p, axis=-1, keepdims=True), 1e-30)
    return p * jnp.where(m > 0.5 * NEG, 1.0 / den, 0.0)


def _split3_dot(x, m_bf16):
    hi = x.astype(BF16)
    r1 = x - hi.astype(F32)
    mid = r1.astype(BF16)
    lo = (r1 - mid.astype(F32)).astype(BF16)
    d = functools.partial(jnp.dot, preferred_element_type=F32)
    return d(hi, m_bf16) + d(mid, m_bf16) + d(lo, m_bf16)


def _topk_select(score, blkf, k):
    work = score
    idxs, vals = [], []
    for _ in range(k):
        m = jnp.max(work, axis=1, keepdims=True)
        idx = jnp.min(jnp.where(work == m, blkf, 1e9), axis=1, keepdims=True)
        work = jnp.where(blkf == idx, -2.0, work)
        idxs.append(idx)
        vals.append(m)
    return idxs, vals


def _mod_kernel(c_ref, w_ref, b_ref, o_ref):
    c = c_ref[...]
    a = (c * _sigmoid(c)).astype(BF16)
    o_ref[...] = jnp.dot(a, w_ref[...].astype(BF16), preferred_element_type=F32) + b_ref[...]


def _modulation(c_all, w_ada, b_ada):
    r, n = c_all.shape[0], w_ada.shape[1]
    tn = 512
    return pl.pallas_call(
        _mod_kernel,
        grid=(n // tn,),
        in_specs=[pl.BlockSpec((r, D_MODEL), lambda j: (0, 0)),
                  pl.BlockSpec((D_MODEL, tn), lambda j: (0, j)),
                  pl.BlockSpec((1, tn), lambda j: (0, j))],
        out_specs=pl.BlockSpec((r, tn), lambda j: (0, j)),
        out_shape=jax.ShapeDtypeStruct((r, n), F32),
        name="adaln_mod",
    )(c_all, w_ada, b_ada.reshape(1, n))


def _inproj_kernel(x_ref, sh_ref, sc_ref, w_ref, qc_ref, xg_ref, q_ref, rows_ref, gt_ref, vs_ref,
                   vw_ref, kvt_ref, kvwt_ref, kst_ref, kwt_ref, *, n_row_cols):
    u = x_ref[...] * (1.0 + sc_ref[...]) + sh_ref[...]
    p = jnp.dot(u.astype(BF16), w_ref[...], preferred_element_type=F32)
    xg_ref[...] = p[:, 0:OFF_Q]
    q_ref[...] = (p[:, OFF_Q:OFF_KV] * SCALE + qc_ref[...]).astype(BF16)
    kv = p[:, OFF_KV:OFF_GATE]
    rows_ref[...] = kv[:, 0:n_row_cols]
    gt_ref[...] = p[:, OFF_GATE:D_IN_PAD]
    vs_ref[...] = kv[:, 3 * D_KV:4 * D_KV].astype(BF16)
    vw_ref[...] = kv[:, 5 * D_KV:6 * D_KV].astype(BF16)
    kvt = kv.T
    kvt_ref[...] = kvt[0:4 * D_KV]
    kvwt_ref[...] = kvt[4 * D_KV:6 * D_KV]
    kst_ref[...] = kvt[2 * D_KV:3 * D_KV].astype(BF16)
    kwt_ref[...] = kvt[4 * D_KV:5 * D_KV].astype(BF16)


def _q_consts():
    qc = np.zeros((1, Q_PAD), np.float32)
    for h in range(N_HEADS):
        slope = 2.0 ** (-8.0 * (h + 1) / N_HEADS)
        qc[0, h * Q_SLOT + AUG_HI] = POS_SPLIT * slope
        qc[0, h * Q_SLOT + AUG_LO] = slope
    return jnp.asarray(qc)


def _in_proj(x2d, shift, scale, w_bf16, tm, n_row_cols):
    r = x2d.shape[0]
    rm = shift.shape[0]
    mod_spec = (pl.BlockSpec((1, D_MODEL), lambda i: (0, 0)) if rm == 1
                else pl.BlockSpec((tm, D_MODEL), lambda i: (i, 0)))
    row = lambda w: pl.BlockSpec((tm, w), lambda i: (i, 0))
    col = lambda h: pl.BlockSpec((h, tm), lambda i: (0, i))
    return pl.pallas_call(
        functools.partial(_inproj_kernel, n_row_cols=n_row_cols),
        grid=(r // tm,),
        in_specs=[row(D_MODEL), mod_spec, mod_spec,
                  pl.BlockSpec((D_MODEL, D_IN_PAD), lambda i: (0, 0)),
                  pl.BlockSpec((1, Q_PAD), lambda i: (0, 0))],
        out_specs=[row(OFF_Q), row(Q_PAD), row(n_row_cols), row(GATE_PAD), row(D_KV), row(D_KV),
                   col(4 * D_KV), col(2 * D_KV), col(D_KV), col(D_KV)],
        out_shape=[jax.ShapeDtypeStruct((r, OFF_Q), F32),
                   jax.ShapeDtypeStruct((r, Q_PAD), BF16),
                   jax.ShapeDtypeStruct((r, n_row_cols), F32),
                   jax.ShapeDtypeStruct((r, GATE_PAD), F32),
                   jax.ShapeDtypeStruct((r, D_KV), BF16),
                   jax.ShapeDtypeStruct((r, D_KV), BF16),
                   jax.ShapeDtypeStruct((4 * D_KV, r), F32),
                   jax.ShapeDtypeStruct((2 * D_KV, r), F32),
                   jax.ShapeDtypeStruct((D_KV, r), BF16),
                   jax.ShapeDtypeStruct((D_KV, r), BF16)],
        compiler_params=pltpu.CompilerParams(dimension_semantics=("arbitrary",),
                                             vmem_limit_bytes=VMEM_LIMIT),
        name="in_proj",
    )(x2d, shift, scale, w_bf16, _q_consts())


def _rg_gates(xc, wrg_ref, ba_ref, bx_ref, lam_ref):
    g = jnp.dot(xc.astype(BF16), wrg_ref[...], preferred_element_type=F32)
    r = _sigmoid(g[:, 0:D_RNN] + ba_ref[...])
    ig = _sigmoid(g[:, D_RNN:2 * D_RNN] + bx_ref[...])
    log_a = -RG_C * r * _softplus(-lam_ref[...])
    a = jnp.exp(log_a)
    b = jnp.sqrt(-jnp.tanh(log_a) * (a * a + 1.0)) * (ig * xc)
    return a, b


def _rg_out(h, gr, nrm_ref):
    y = h * _gelu_tanh(gr)
    return (y * lax.rsqrt(jnp.mean(y * y, axis=-1, keepdims=True) + 1e-6)) * nrm_ref[...]


def _rg_prompt_kernel(xg_ref, cw_ref, cb_ref, wrg_ref, ba_ref, bx_ref, lam_ref, nrm_ref,
                      y_ref, tail_ref, hl_ref, xp_s, a_s, b_s, h_s, hc_s, *, tc):
    i = pl.program_id(0)

    @pl.when(i == 0)
    def _():
        xp_s[0:8, :] = jnp.zeros((8, D_RNN), F32)
        hc_s[...] = jnp.zeros((8, D_RNN), F32)

    xr = xg_ref[:, 0:D_RNN]
    gr = xg_ref[:, D_RNN:2 * D_RNN]
    xp_s[8:8 + tc, :] = xr
    cw = cw_ref[...]
    xc = (cb_ref[...] + cw[0:1] * xp_s[5:5 + tc, :] + cw[1:2] * xp_s[6:6 + tc, :]
          + cw[2:3] * xp_s[7:7 + tc, :] + cw[3:4] * xr)
    a, b = _rg_gates(xc, wrg_ref, ba_ref, bx_ref, lam_ref)
    a_s[...] = a
    b_s[...] = b
    rowi = lax.broadcasted_iota(jnp.int32, (8, D_RNN), 0)

    def tile(gi, hc):
        r0 = pl.multiple_of(gi * 8, 8)
        at = a_s[pl.ds(r0, 8), :]
        bt = b_s[pl.ds(r0, 8), :]
        for d in (1, 2, 4):
            keep = rowi >= d
            a_sh = jnp.where(keep, pltpu.roll(at, d, 0), 1.0)
            b_sh = jnp.where(keep, pltpu.roll(bt, d, 0), 0.0)
            bt = at * b_sh + bt
            at = at * a_sh
        h = at * hc + bt
        h_s[pl.ds(r0, 8), :] = h
        return h[7:8, :]

    hc = lax.fori_loop(0, tc // 8, tile, hc_s[0:1, :])
    hc_s[0:1, :] = hc
    xp_s[0:8, :] = xr[tc - 8:tc]
    y_ref[...] = _rg_out(h_s[...], gr, nrm_ref).astype(BF16)
    tail_ref[...] = xr[tc - 8:tc]
    hl_ref[...] = jnp.broadcast_to(hc, (8, D_RNN))


def _rg_prompt(xg, cw, cb, wrg, ba, bx, lam, nrm, tc):
    t = xg.shape[0]
    vec = pl.BlockSpec((1, D_RNN), lambda i: (0, 0))
    return pl.pallas_call(
        functools.partial(_rg_prompt_kernel, tc=tc),
        grid=(t // tc,),
        in_specs=[pl.BlockSpec((tc, 2 * D_RNN), lambda i: (i, 0)),
                  pl.BlockSpec((CONV_W, D_RNN), lambda i: (0, 0)), vec,
                  pl.BlockSpec((D_RNN, 2 * D_RNN), lambda i: (0, 0)), vec, vec, vec, vec],
        out_specs=[pl.BlockSpec((tc, D_RNN), lambda i: (i, 0)),
                   pl.BlockSpec((8, D_RNN), lambda i: (0, 0)),
                   pl.BlockSpec((8, D_RNN), lambda i: (0, 0))],
        out_shape=[jax.ShapeDtypeStruct((t, D_RNN), BF16),
                   jax.ShapeDtypeStruct((8, D_RNN), F32),
                   jax.ShapeDtypeStruct((8, D_RNN), F32)],
        scratch_shapes=[pltpu.VMEM((tc + 8, D_RNN), F32), pltpu.VMEM((tc, D_RNN), F32),
                        pltpu.VMEM((tc, D_RNN), F32), pltpu.VMEM((tc, D_RNN), F32),
                        pltpu.VMEM((8, D_RNN), F32)],
        compiler_params=pltpu.CompilerParams(dimension_semantics=("arbitrary",)),
        name="rg_prompt",
    )(xg, cw, cb, wrg, ba, bx, lam, nrm)


def _rg_sample_kernel(xg_ref, c0_ref, c1_ref, c2_ref, h0_ref, cw_ref, cb_ref, wrg_ref, ba_ref,
                      bx_ref, lam_ref, nrm_ref, y_ref, h_ref):
    xr = xg_ref[:, 0:D_RNN]
    gr = xg_ref[:, D_RNN:2 * D_RNN]
    cw = cw_ref[...]
    xc = (cb_ref[...] + cw[0:1] * c0_ref[...] + cw[1:2] * c1_ref[...] + cw[2:3] * c2_ref[...]
          + cw[3:4] * xr)
    a, b = _rg_gates(xc, wrg_ref, ba_ref, bx_ref, lam_ref)
    h = a * h0_ref[...] + b
    h_ref[...] = h
    y_ref[...] = _rg_out(h, gr, nrm_ref).astype(BF16)


def _rg_sample(xg, c0, c1, c2, h0, cw, cb, wrg, ba, bx, lam, nrm):
    b = xg.shape[0]
    return pl.pallas_call(
        _rg_sample_kernel,
        out_shape=[jax.ShapeDtypeStruct((b, D_RNN), BF16), jax.ShapeDtypeStruct((b, D_RNN), F32)],
        name="rg_sample",
    )(xg, c0, c1, c2, h0, cw, cb, wrg, ba, bx, lam, nrm)


CMP_PITCH = 20
CMP_KSPLIT = 8


def _pitch_rows(n_groups):
    return -(-(CMP_PITCH * n_groups) // 8) * 8


def _compress_chunk(x_s, blk0, nblk, pek_ref, pev_ref, wk_ref, wv_ref, w2k_ref, w2v_ref):
    base = blk0 * CMP_PITCH
    outs = []
    for t, (pe_ref, w1_ref, w2_ref) in enumerate(((pek_ref, wk_ref, w2k_ref),
                                                   (pev_ref, wv_ref, w2v_ref))):
        h = None
        for l0 in range(0, CMP_LEN, CMP_KSPLIT):
            pieces = []
            for l in range(l0, l0 + CMP_KSPLIT):
                row = (l // CMP_STRIDE) * CMP_PITCH + l % CMP_STRIDE
                xl = x_s[t, pl.ds(base + row, nblk, stride=CMP_PITCH), :]
                pieces.append((xl + pe_ref[:, l * D_KV:(l + 1) * D_KV]).astype(BF16))
            part = jnp.dot(jnp.concatenate(pieces, axis=1), w1_ref[l0 * D_KV:(l0 + CMP_KSPLIT) * D_KV, :],
                           preferred_element_type=F32)
            h = part if h is None else h + part
        h = h * _sigmoid(h)
        outs.append(jnp.dot(h.astype(BF16), w2_ref[...], preferred_element_type=F32))
    return outs


CMP_CHUNK = 128
CMP_ROWS = CMP_CHUNK * CMP_STRIDE
CMP_PARTS = 2


def _cmp_prompt_kernel(x_ref, nxt_ref, pek_ref, pev_ref, wk_ref, wv_ref, w2k_ref, w2v_ref,
                       kct_ref, vc_ref, x_s):
    for t in range(2):
        cols = slice(t * D_KV, (t + 1) * D_KV)
        for j in range(CMP_CHUNK):
            x_s[t, CMP_PITCH * j:CMP_PITCH * j + CMP_STRIDE, :] = (
                x_ref[CMP_STRIDE * j:CMP_STRIDE * (j + 1), cols])
        x_s[t, CMP_PITCH * CMP_CHUNK:CMP_PITCH * CMP_CHUNK + CMP_STRIDE, :] = nxt_ref[:, cols]
    kc, vc = _compress_chunk(x_s, 0, CMP_CHUNK, pek_ref, pev_ref, wk_ref, wv_ref, w2k_ref, w2v_ref)
    kct_ref[...] = kc.T.astype(BF16)
    vc_ref[...] = vc.astype(BF16)


def _cmp_prompt(rows, pek, pev, wk, wv, w2k, w2v):
    t = rows.shape[0]
    n_steps = t // CMP_ROWS
    last16 = t // CMP_STRIDE - 1
    full = lambda a: pl.BlockSpec(a.shape, lambda i: (0, 0))
    return pl.pallas_call(
        _cmp_prompt_kernel,
        grid=(n_steps,),
        in_specs=[pl.BlockSpec((CMP_ROWS, 2 * D_KV), lambda i: (i, 0)),
                  pl.BlockSpec((CMP_STRIDE, 2 * D_KV),
                               lambda i: (jnp.minimum((i + 1) * CMP_CHUNK, last16), 0)),
                  full(pek), full(pev), full(wk), full(wv), full(w2k), full(w2v)],
        out_specs=[pl.BlockSpec((D_KV, CMP_CHUNK), lambda i: (0, i)),
                   pl.BlockSpec((CMP_CHUNK, D_KV), lambda i: (i, 0))],
        out_shape=[jax.ShapeDtypeStruct((D_KV, t // CMP_STRIDE), BF16),
                   jax.ShapeDtypeStruct((t // CMP_STRIDE, D_KV), BF16)],
        scratch_shapes=[pltpu.VMEM((2, _pitch_rows(CMP_CHUNK + 1), D_KV), F32)],
        compiler_params=pltpu.CompilerParams(dimension_semantics=("arbitrary",),
                                             vmem_limit_bytes=VMEM_LIMIT),
        name="cmp_prompt",
    )(rows, rows, pek, pev, wk, wv, w2k, w2v)


WIN_KEYS = WINDOW + Q_BLOCK
CMP_PREFIX = 256
HEAD_BLKS = LANE // SEL_BLOCK
AUG_HEAD_PEN = AUG_PEN + BLK_PER_CHUNK
AUG_ORIGIN = AUG_HEAD_PEN + HEAD_BLKS
OWN_STEP = 256


def _sel_aug_rows():
    k = np.arange(SEL_CHUNK)
    c = np.zeros((HEAD_DIM, SEL_CHUNK), np.float32)
    c[AUG_HI - HEAD_DIM] = k // POS_SPLIT
    c[AUG_LO - HEAD_DIM] = k % POS_SPLIT
    for b in range(BLK_PER_CHUNK):
        c[AUG_PEN - HEAD_DIM + b] = (k // SEL_BLOCK == b)
    kh = np.arange(HEAD_BLKS * SEL_BLOCK)
    h = np.zeros((HEAD_DIM, kh.size), np.float32)
    h[AUG_HI - HEAD_DIM] = kh // POS_SPLIT
    h[AUG_LO - HEAD_DIM] = kh % POS_SPLIT
    for b in range(HEAD_BLKS):
        h[AUG_HEAD_PEN - HEAD_DIM + b] = (kh // SEL_BLOCK == b)
    h[AUG_ORIGIN - HEAD_DIM] = 1.0
    return jnp.asarray(c, dtype=BF16), jnp.asarray(h, dtype=BF16)


def _attn_prompt_kernel(q_ref, gt_ref, kst_ref, vs_ref, kwt_ref, vw_ref, kct_ref, vc_ref,
                        msel_ref, cst_ref, csth_ref, nrm_ref, o_ref, act_ref, m_s, l_s, acc_s, oc_s,
                        sc_s):
    i = pl.program_id(0)
    t0 = i * Q_BLOCK
    n_cmp = kct_ref.shape[1]
    n_sel = msel_ref.shape[1]
    rows = GQA_R * Q_BLOCK
    dot = functools.partial(jnp.dot, preferred_element_type=F32)
    qpos = t0 + lax.broadcasted_iota(jnp.int32, (Q_BLOCK, 1), 0)
    qpos4 = jnp.concatenate([qpos] * GQA_R, axis=0)
    qposf4 = qpos4.astype(F32)
    cend = lax.broadcasted_iota(jnp.int32, (1, n_cmp), 1) * CMP_STRIDE + (CMP_LEN - 1)
    blk = lax.broadcasted_iota(jnp.int32, (1, n_sel), 1)
    blkf = blk.astype(F32)
    gates = _sigmoid(gt_ref[...])
    cur = qpos >> 6
    forced = (blk == 0) | (blk == cur) | (blk == cur - 1)
    in_past = blk * SEL_BLOCK <= qpos
    n_chunks = (t0 + Q_BLOCK + SEL_CHUNK - 1) // SEL_CHUNK
    win0 = pl.multiple_of(jnp.maximum(t0 - WINDOW, 0), LANE)
    wpos = win0 + lax.broadcasted_iota(jnp.int32, (1, WIN_KEYS), 1)
    wdist = qpos4 - wpos
    wmask = (wdist >= 0) & (wdist < WINDOW)
    wdistf = wdist.astype(F32)

    qa, q64, slope4 = [], [], []
    for g in range(N_KV):
        qg = jnp.concatenate([q_ref[:, (g * GQA_R + r) * Q_SLOT:(g * GQA_R + r + 1) * Q_SLOT]
                              for r in range(GQA_R)], axis=0)
        qa.append(qg)
        q64.append(qg[:, 0:HEAD_DIM])
        slope4.append(jnp.concatenate(
            [jnp.full((Q_BLOCK, 1), np.float32(2.0 ** -(g * GQA_R + r + 1)), F32)
             for r in range(GQA_R)], axis=0))

    def cmp_branch(n_used):
        cend_u = cend[:, 0:n_used]
        dist = qposf4 - cend_u.astype(F32)
        for g in range(N_KV):
            gs = slice(g * HEAD_DIM, (g + 1) * HEAD_DIM)
            s = dot(q64[g], kct_ref[gs, 0:n_used])
            p = _masked_softmax(s - slope4[g] * dist, cend_u <= qpos4)
            oc_s[g * rows:(g + 1) * rows, :] = dot(p.astype(BF16), vc_ref[0:n_used, :])
            psum = p[0:Q_BLOCK]
            for r in range(1, GQA_R):
                psum = psum + p[r * Q_BLOCK:(r + 1) * Q_BLOCK]
            p_slc = _split3_dot(psum, msel_ref[0:n_used, :])
            sc_s[g * Q_BLOCK:(g + 1) * Q_BLOCK, :] = jnp.where(
                in_past, jnp.where(forced, FORCE_SCORE, p_slc), -1.0)

    n_prefix = max(1, n_cmp // CMP_PREFIX)
    per = n_cmp // n_prefix
    need = (t0 + Q_BLOCK - CMP_LEN) // CMP_STRIDE + 1
    bucket = jnp.clip((need + per - 1) // per - 1, 0, n_prefix - 1)
    for k in range(n_prefix):
        @pl.when(bucket == k)
        def _(k=k):
            cmp_branch((k + 1) * per)

    o_c = [oc_s[g * rows:(g + 1) * rows, g * HEAD_DIM:(g + 1) * HEAD_DIM] for g in range(N_KV)]

    o_w = []
    for g in range(N_KV):
        gs = slice(g * HEAD_DIM, (g + 1) * HEAD_DIM)
        sw = dot(q64[g], kwt_ref[gs, pl.ds(win0, WIN_KEYS)])
        pw = _masked_softmax(sw - slope4[g] * wdistf, wmask)
        o_w.append(dot(pw.astype(BF16), vw_ref[pl.ds(win0, WIN_KEYS), :])[:, gs])

    idxs, vals = _topk_select(sc_s[...], blkf, min(TOP_N, n_sel))
    sel = jnp.zeros((N_KV * Q_BLOCK, n_sel), F32)
    for idx, val in zip(idxs, vals):
        sel = jnp.where((blkf == idx) & (val >= 0.0), 1.0, sel)
    pen_all = jnp.where(sel > 0.5, 0.0, NEG)
    head_lane = blk < HEAD_BLKS
    pen_head = pen_all.astype(BF16)
    pen = jnp.where(head_lane, NEG, pen_all).astype(BF16)
    penh = [pen_head[g * Q_BLOCK:(g + 1) * Q_BLOCK] for g in range(N_KV)]
    penb = [pen[g * Q_BLOCK:(g + 1) * Q_BLOCK] for g in range(N_KV)]
    sel = jnp.where(head_lane, 0.0, sel)
    n_chunks_all = n_sel // BLK_PER_CHUNK
    for g in range(N_KV):
        col_any = jnp.max(sel[g * Q_BLOCK:(g + 1) * Q_BLOCK], axis=0, keepdims=True)
        for c in range(n_chunks_all):
            hit = jnp.max(col_any[:, c * BLK_PER_CHUNK:(c + 1) * BLK_PER_CHUNK])
            act_ref[g * n_chunks_all + c] = (hit > 0.5).astype(jnp.int32)

    cst = cst_ref[...]
    oh_lane = lax.broadcasted_iota(jnp.int32, (n_sel, LANE), 1)
    oh_base = lax.broadcasted_iota(jnp.int32, (n_sel, LANE), 0) - oh_lane + AUG_PEN
    oh_ok = (oh_lane >= AUG_PEN) & (oh_lane < AUG_PEN + BLK_PER_CHUNK)

    def chunk_scores(c, g, causal, width):
        k0 = pl.multiple_of(c * SEL_CHUNK, SEL_CHUNK)
        onehot = jnp.where((oh_base == c * BLK_PER_CHUNK) & oh_ok, 1.0, 0.0).astype(BF16)
        placed = dot(penb[g], onehot).astype(BF16)
        qaug = qa[g] + jnp.concatenate([placed] * GQA_R, axis=0)
        kta = jnp.concatenate(
            [kst_ref[g * HEAD_DIM:(g + 1) * HEAD_DIM, pl.ds(k0, width)], cst[:, 0:width]], axis=0)
        s = dot(qaug, kta)
        if causal:
            kpos = k0 + lax.broadcasted_iota(jnp.int32, (1, width), 1)
            tri = jnp.where(kpos > qpos, NEG, 0.0)
            s = s + jnp.concatenate([tri] * GQA_R, axis=0)
        return s

    def chunk_update(c, g, s):
        k0 = pl.multiple_of(c * SEL_CHUNK, SEL_CHUNK)
        crow = slope4[g] * (k0.astype(F32) - qposf4)
        rg = slice(g * rows, (g + 1) * rows)
        m = m_s[rg, :]
        m_new = jnp.maximum(m, jnp.max(s, axis=-1, keepdims=True) + crow)
        p = jnp.exp(s + (crow - m_new))
        alpha = jnp.exp(m - m_new)
        return (m_new, alpha * l_s[rg, :] + jnp.sum(p, axis=-1, keepdims=True),
                alpha * acc_s[rg, :] + dot(p.astype(BF16), vs_ref[pl.ds(k0, s.shape[1]), :]))

    def fold(c, groups, causal, width=SEL_CHUNK):
        scores = [chunk_scores(c, g, causal, width) for g in groups]
        new = [chunk_update(c, g, s) for g, s in zip(groups, scores)]
        rg = slice(groups[0] * rows, (groups[-1] + 1) * rows)
        for ref, k in ((m_s, 0), (l_s, 1), (acc_s, 2)):
            ref[rg, :] = jnp.concatenate([n[k] for n in new], axis=0)

    head_keys = HEAD_BLKS * SEL_BLOCK
    head_oh = jnp.where((oh_base == -BLK_PER_CHUNK) & (oh_lane >= AUG_HEAD_PEN)
                        & (oh_lane < AUG_HEAD_PEN + HEAD_BLKS), 1.0, 0.0).astype(BF16)
    csth = csth_ref[...]
    own_c = n_chunks - 1
    own_k0 = pl.multiple_of(own_c * SEL_CHUNK, SEL_CHUNK)
    own_k0f = own_k0.astype(F32)
    own_oh = jnp.where((oh_base == own_c * BLK_PER_CHUNK) & oh_ok, 1.0, 0.0).astype(BF16)
    lane_q = lax.broadcasted_iota(jnp.int32, (rows, LANE), 1)

    def own_pass(width):
        kpos = jnp.concatenate([lax.broadcasted_iota(jnp.int32, (1, head_keys), 1),
                                own_k0 + lax.broadcasted_iota(jnp.int32, (1, width), 1)], axis=1)
        tri = jnp.concatenate([jnp.where(kpos > qpos, NEG, 0.0)] * GQA_R, axis=0)
        new = []
        for g in range(N_KV):
            gr = slice(g * HEAD_DIM, (g + 1) * HEAD_DIM)
            placed = (dot(penb[g], own_oh) + dot(penh[g], head_oh)).astype(BF16)
            origin = jnp.where(lane_q == AUG_ORIGIN, -slope4[g] * own_k0f, 0.0).astype(BF16)
            qaug = qa[g] + jnp.concatenate([placed] * GQA_R, axis=0) + origin
            kta = jnp.concatenate(
                [jnp.concatenate([kst_ref[gr, 0:head_keys], csth], axis=0),
                 jnp.concatenate([kst_ref[gr, pl.ds(own_k0, width)], cst[:, 0:width]], axis=0)],
                axis=1)
            s = dot(qaug, kta) + tri
            crow = slope4[g] * (own_k0f - qposf4)
            m0 = jnp.max(s, axis=-1, keepdims=True) + crow
            p = jnp.exp(s + (crow - m0))
            vcat = jnp.concatenate([vs_ref[0:head_keys, :], vs_ref[pl.ds(own_k0, width), :]], axis=0)
            new.append((m0, jnp.sum(p, axis=-1, keepdims=True), dot(p.astype(BF16), vcat)))
        for ref, k in ((m_s, 0), (l_s, 1), (acc_s, 2)):
            ref[...] = jnp.concatenate([n[k] for n in new], axis=0)

    own_keys = t0 + Q_BLOCK - own_k0
    for width in range(OWN_STEP, SEL_CHUNK + 1, OWN_STEP):
        @pl.when((own_keys > width - OWN_STEP) & (own_keys <= width))
        def _(width=width):
            own_pass(width)

    def past_chunk(c, carry):
        a0 = act_ref[c] > 0
        a1 = act_ref[n_chunks_all + c] > 0

        @pl.when(a0 & a1)
        def _():
            fold(c, (0, 1), causal=False)

        @pl.when(a0 & jnp.logical_not(a1))
        def _():
            fold(c, (0,), causal=False)

        @pl.when(a1 & jnp.logical_not(a0))
        def _():
            fold(c, (1,), causal=False)

        return carry

    lax.fori_loop(0, n_chunks - 1, past_chunk, 0)

    head_out = []
    for g in range(N_KV):
        gs = slice(g * HEAD_DIM, (g + 1) * HEAD_DIM)
        rg = slice(g * rows, (g + 1) * rows)
        o_s = (acc_s[rg, :] * (1.0 / jnp.maximum(l_s[rg, :], 1e-30)))[:, gs]
        for r in range(GQA_R):
            h = g * GQA_R + r
            rs = slice(r * Q_BLOCK, (r + 1) * Q_BLOCK)
            head_out.append(o_c[g][rs] * gates[:, 3 * h:3 * h + 1]
                            + o_s[rs] * gates[:, 3 * h + 1:3 * h + 2]
                            + o_w[g][rs] * gates[:, 3 * h + 2:3 * h + 3])
    y = jnp.concatenate(head_out, axis=1)
    y = (y * lax.rsqrt(jnp.mean(y * y, axis=-1, keepdims=True) + 1e-6)) * nrm_ref[...]
    o_ref[...] = y.astype(BF16)


def _attn_prompt(q, gates, kst, vs, kwt, vw, kct, vc, msel, nrm):
    t = q.shape[0]
    cst, csth = _sel_aug_rows()
    full = lambda a: pl.BlockSpec(a.shape, lambda i: (0, 0), pipeline_mode=pl.Buffered(1))
    return pl.pallas_call(
        _attn_prompt_kernel,
        grid=(t // Q_BLOCK,),
        in_specs=[pl.BlockSpec((Q_BLOCK, Q_PAD), lambda i: (i, 0)),
                  pl.BlockSpec((Q_BLOCK, GATE_PAD), lambda i: (i, 0)),
                  full(kst), full(vs), full(kwt), full(vw), full(kct), full(vc), full(msel),
                  full(cst), full(csth), full(nrm)],
        out_specs=pl.BlockSpec((Q_BLOCK, D_Q), lambda i: (i, 0)),
        out_shape=jax.ShapeDtypeStruct((t, D_Q), BF16),
        scratch_shapes=[pltpu.SMEM((N_KV * (t // SEL_CHUNK),), jnp.int32),
                        pltpu.VMEM((N_KV * GQA_R * Q_BLOCK, 1), F32),
                        pltpu.VMEM((N_KV * GQA_R * Q_BLOCK, 1), F32),
                        pltpu.VMEM((N_KV * GQA_R * Q_BLOCK, D_KV), F32),
                        pltpu.VMEM((N_KV * GQA_R * Q_BLOCK, D_KV), F32),
                        pltpu.VMEM((N_KV * Q_BLOCK, t // SEL_BLOCK), F32)],
        compiler_params=pltpu.CompilerParams(dimension_semantics=("arbitrary",),
                                             vmem_limit_bytes=VMEM_LIMIT),
        name="attn_prompt",
    )(q, gates, kst, vs, kwt, vw, kct, vc, msel, cst, csth, nrm)


FF_CHUNK = D_FF // 2


def _layer_norm(x, g, b):
    mu = jnp.mean(x, axis=-1, keepdims=True)
    xc = x - mu
    var = jnp.mean(xc * xc, axis=-1, keepdims=True)
    return (xc * lax.rsqrt(var + 1e-5)) * g + b


def _ffn_kernel(x_ref, hr_ref, ha_ref, g1_ref, sh2_ref, sc2_ref, g2_ref, wo_ref, l1g_ref, l1b_ref,
                wup_ref, wdn_ref, l2g_ref, l2b_ref, o_ref, *, alpha):
    d = functools.partial(jnp.dot, preferred_element_type=F32)
    mix = d(hr_ref[...], wo_ref[0:D_RNN, :]) + d(ha_ref[...], wo_ref[D_RNN:D_MODEL, :])
    x1 = _layer_norm(alpha * x_ref[...] + g1_ref[...] * mix, l1g_ref[...], l1b_ref[...])
    u = (x1 * (1.0 + sc2_ref[...]) + sh2_ref[...]).astype(BF16)
    f = None
    for c in range(0, D_FF, FF_CHUNK):
        gate = d(u, wup_ref[:, c:c + FF_CHUNK])
        up = d(u, wup_ref[:, D_FF + c:D_FF + c + FF_CHUNK])
        part = d((gate * _sigmoid(gate) * up).astype(BF16), wdn_ref[c:c + FF_CHUNK, :])
        f = part if f is None else f + part
    o_ref[...] = _layer_norm(alpha * x1 + g2_ref[...] * f, l2g_ref[...], l2b_ref[...])


def _merge_ffn(x2d, hr, ha, g1, sh2, sc2, g2, wo, l1g, l1b, wup, wdn, l2g, l2b, tm, alpha):
    r = x2d.shape[0]
    rm = g1.shape[0]
    mod_spec = (pl.BlockSpec((1, D_MODEL), lambda i: (0, 0)) if rm == 1
                else pl.BlockSpec((tm, D_MODEL), lambda i: (i, 0)))
    vec = pl.BlockSpec((1, D_MODEL), lambda i: (0, 0))
    full = lambda a: pl.BlockSpec(a.shape, lambda i: (0, 0), pipeline_mode=pl.Buffered(1))
    return pl.pallas_call(
        functools.partial(_ffn_kernel, alpha=alpha),
        grid=(r // tm,),
        in_specs=[pl.BlockSpec((tm, D_MODEL), lambda i: (i, 0)),
                  pl.BlockSpec((tm, D_RNN), lambda i: (i, 0)),
                  pl.BlockSpec((tm, D_Q), lambda i: (i, 0)),
                  mod_spec, mod_spec, mod_spec, mod_spec,
                  full(wo), vec, vec, full(wup), full(wdn), vec, vec],
        out_specs=pl.BlockSpec((tm, D_MODEL), lambda i: (i, 0)),
        out_shape=jax.ShapeDtypeStruct((r, D_MODEL), F32),
        compiler_params=pltpu.CompilerParams(dimension_semantics=("arbitrary",),
                                             vmem_limit_bytes=VMEM_LIMIT),
        name="merge_ffn",
    )(x2d, hr, ha, g1, sh2, sc2, g2, wo, l1g, l1b, wup, wdn, l2g, l2b)


CACHE_ROWS_PER_PAGE = 4 * N_KV * HEAD_DIM
CMP_ROWS_PER_PAGE = 2 * N_KV * HEAD_DIM


def _sample_cmp_kernel(pt_ref, cache_ref, q_ref, new_ref, pek_ref, pev_ref, wk_ref, wv_ref,
                       w2k_ref, w2v_ref, msel_ref, oc_ref, pslc_ref, buf, x_s, kc_s, vc_s, sem,
                       *, n_pages):
    b = pl.program_id(0)
    nb = pl.num_programs(0)
    past_len = n_pages * PAGE_SIZE
    n_cmp = past_len // CMP_STRIDE

    def page_copy(bb, p, slot):
        pool = pt_ref[bb * n_pages + p]
        return pltpu.make_async_copy(
            cache_ref.at[pl.ds(pool * CACHE_ROWS_PER_PAGE, CMP_ROWS_PER_PAGE), :],
            buf.at[slot, p], sem.at[slot])

    def start_all(bb, slot):
        for p in range(n_pages):
            page_copy(bb, p, slot).start()

    @pl.when(b == 0)
    def _():
        start_all(0, 0)

    slot = b % 2

    @pl.when(b + 1 < nb)
    def _():
        start_all(b + 1, 1 - slot)

    for p in range(n_pages):
        page_copy(b, p, slot).wait()

    groups_per_page = PAGE_SIZE // CMP_STRIDE

    def to_rows(p):
        r0 = p * (groups_per_page * CMP_PITCH)
        for t in range(2):
            tile = buf[slot, p, t * D_KV:(t + 1) * D_KV, :].T
            for j in range(groups_per_page):
                x_s[t, r0 + CMP_PITCH * j:r0 + CMP_PITCH * j + CMP_STRIDE, :] = (
                    tile[CMP_STRIDE * j:CMP_STRIDE * (j + 1)])

    def compress(blk0, nblk):
        kc, vc = _compress_chunk(x_s, blk0, nblk, pek_ref, pev_ref, wk_ref, wv_ref, w2k_ref,
                                 w2v_ref)
        kc_s[blk0:blk0 + nblk, :] = kc
        vc_s[blk0:blk0 + nblk, :] = vc

    part_pages = n_pages // CMP_PARTS
    done = 0
    for k in range(CMP_PARTS):
        for p in range(k * part_pages, (k + 1) * part_pages):
            to_rows(p)
        if k == CMP_PARTS - 1:
            tail_row = lax.broadcasted_iota(jnp.int32, (CMP_STRIDE, D_KV), 0)
            for t in range(2):
                x_s[t, CMP_PITCH * n_cmp:CMP_PITCH * n_cmp + CMP_STRIDE, :] = jnp.where(
                    tail_row == 0, new_ref[0][:, t * D_KV:(t + 1) * D_KV], 0.0)
            upto = n_cmp
        else:
            upto = (k + 1) * part_pages * groups_per_page - 16
        compress(done, upto - done)
        done = upto

    q = q_ref[0]
    kcb = kc_s[...].astype(BF16)
    vcb = vc_s[...].astype(BF16)
    row = lax.broadcasted_iota(jnp.int32, (N_HEADS, 1), 0)
    first = row < GQA_R
    nt = (((1,), (1,)), ((), ()))
    s0 = lax.dot_general(q, kcb[:, 0:HEAD_DIM], nt, preferred_element_type=F32)
    s1 = lax.dot_general(q, kcb[:, HEAD_DIM:D_KV], nt, preferred_element_type=F32)
    cend = lax.broadcasted_iota(jnp.int32, (1, n_cmp), 1) * CMP_STRIDE + (CMP_LEN - 1)
    s = jnp.where(first, s0, s1) - _pow2_neg(row + 1) * (past_len - cend).astype(F32)
    p = _masked_softmax(s, cend <= past_len)
    o = jnp.dot(p.astype(BF16), vcb, preferred_element_type=F32)
    oc_ref[0] = jnp.where(first, o[:, 0:HEAD_DIM], o[:, HEAD_DIM:D_KV])
    psum = jnp.concatenate([jnp.sum(p[0:GQA_R], axis=0, keepdims=True),
                            jnp.sum(p[GQA_R:N_HEADS], axis=0, keepdims=True)], axis=0)
    pslc_ref[0] = _split3_dot(psum, msel_ref[...])


def _sample_cmp(pt_flat, cache2d, q3, new3, pek, pev, wk, wv, w2k, w2v, msel, n_pages):
    b = q3.shape[0]
    past_len = n_pages * PAGE_SIZE
    n_cmp = past_len // CMP_STRIDE
    full = lambda a: pl.BlockSpec(a.shape, lambda i, pt: (0,) * a.ndim)
    return pl.pallas_call(
        functools.partial(_sample_cmp_kernel, n_pages=n_pages),
        grid_spec=pltpu.PrefetchScalarGridSpec(
            num_scalar_prefetch=1, grid=(b,),
            in_specs=[pl.BlockSpec(memory_space=pl.ANY),
                      pl.BlockSpec((1, N_HEADS, HEAD_DIM), lambda i, pt: (i, 0, 0)),
                      pl.BlockSpec((1, 1, 6 * D_KV), lambda i, pt: (i, 0, 0)),
                      full(pek), full(pev), full(wk), full(wv), full(w2k), full(w2v), full(msel)],
            out_specs=[pl.BlockSpec((1, N_HEADS, HEAD_DIM), lambda i, pt: (i, 0, 0)),
                       pl.BlockSpec((1, N_KV, msel.shape[1]), lambda i, pt: (i, 0, 0))],
            scratch_shapes=[pltpu.VMEM((2, n_pages, CMP_ROWS_PER_PAGE, PAGE_SIZE), F32),
                            pltpu.VMEM((2, _pitch_rows(n_cmp + 1), D_KV), F32),
                            pltpu.VMEM((n_cmp, D_KV), F32), pltpu.VMEM((n_cmp, D_KV), F32),
                            pltpu.SemaphoreType.DMA((2,))]),
        out_shape=[jax.ShapeDtypeStruct((b, N_HEADS, HEAD_DIM), F32),
                   jax.ShapeDtypeStruct((b, N_KV, msel.shape[1]), F32)],
        compiler_params=pltpu.CompilerParams(dimension_semantics=("arbitrary",),
                                             vmem_limit_bytes=VMEM_LIMIT),
        name="sample_cmp",
    )(pt_flat, cache2d, q3, new3, pek, pev, wk, wv, w2k, w2v, msel)


META_W = 128


def _sample_topk_kernel(pslc_ref, meta_ref, *, past_len):
    n_lane = pslc_ref.shape[1]
    n_sel = -(-(past_len + 1) // SEL_BLOCK)
    blk = lax.broadcasted_iota(jnp.int32, (1, n_lane), 1)
    cur = past_len // SEL_BLOCK
    forced = (blk == 0) | (blk == cur) | (blk == cur - 1)
    score = jnp.where(blk * SEL_BLOCK <= past_len,
                      jnp.where(forced, FORCE_SCORE, pslc_ref[...]), -1.0)
    score = jnp.where(blk < n_sel, score, -3.0)
    idxs, vals = _topk_select(score, blk.astype(F32), min(TOP_N, n_sel))
    lane = lax.broadcasted_iota(jnp.int32, (pslc_ref.shape[0], META_W), 1)
    meta = jnp.zeros((pslc_ref.shape[0], META_W), F32)
    for it, (idx, val) in enumerate(zip(idxs, vals)):
        meta = jnp.where(lane == it, idx, meta)
        meta = jnp.where(lane == TOP_N + it, jnp.where(val >= 0.0, 1.0, 0.0), meta)
    meta_ref[...] = meta


def _sample_topk(pslc2d, past_len):
    return pl.pallas_call(
        functools.partial(_sample_topk_kernel, past_len=past_len),
        out_shape=jax.ShapeDtypeStruct((pslc2d.shape[0], META_W), F32),
        name="sample_topk",
    )(pslc2d)


SEL_KEYS = TOP_N * PAGE_SIZE


def _sample_attn_kernel(koff_ref, voff_ref, cache_ref, q_ref, meta_ref, new_ref, newt_ref, win_ref,
                        oc_ref, gt_ref, nrm_ref, e16_ref, ha_ref, wout_ref, kbuf, vbuf, sem,
                        *, past_len):
    b = pl.program_id(0)
    nb = pl.num_programs(0)

    def tile_copies(bb, slot):
        cps = []
        for g in range(N_KV):
            for n in range(TOP_N):
                i = (bb * N_KV + g) * TOP_N + n
                dst = pl.ds(n * PAGE_SIZE, PAGE_SIZE)
                cps.append(pltpu.make_async_copy(cache_ref.at[pl.ds(koff_ref[i], HEAD_DIM), :],
                                                 kbuf.at[slot, g, :, dst], sem.at[slot]))
                cps.append(pltpu.make_async_copy(cache_ref.at[pl.ds(voff_ref[i], HEAD_DIM), :],
                                                 vbuf.at[slot, g, :, dst], sem.at[slot]))
        return cps

    @pl.when(b == 0)
    def _():
        for cp in tile_copies(0, 0):
            cp.start()

    slot = b % 2

    @pl.when(b + 1 < nb)
    def _():
        for cp in tile_copies(b + 1, 1 - slot):
            cp.start()

    for cp in tile_copies(b, slot):
        cp.wait()

    q = q_ref[0]
    qf = q.astype(F32)
    row = lax.broadcasted_iota(jnp.int32, (N_HEADS, 1), 0)
    first = row < GQA_R
    slope = _pow2_neg(row + 1)
    new = new_ref[0]
    nt = (((1,), (1,)), ((), ()))

    def new_rows(off):
        a = new[:, off:off + HEAD_DIM]
        c = new[:, off + HEAD_DIM:off + D_KV]
        v = jnp.where(first, jnp.broadcast_to(a, (N_HEADS, HEAD_DIM)),
                      jnp.broadcast_to(c, (N_HEADS, HEAD_DIM)))
        return v.astype(BF16).astype(F32)

    def attend(s_buf, mask_buf, v_of_p, s_new, new_on, v_new):
        s_buf = jnp.where(mask_buf, s_buf, NEG)
        s_new = jnp.where(new_on, s_new, NEG)
        m = jnp.maximum(jnp.max(s_buf, axis=-1, keepdims=True), s_new)
        p = jnp.where(mask_buf, jnp.exp(s_buf - m), 0.0)
        p_new = jnp.where(new_on, jnp.exp(s_new - m), 0.0)
        den = jnp.maximum(jnp.sum(p, axis=-1, keepdims=True) + p_new, 1e-30)
        p = p / den
        p_new = (p_new / den).astype(BF16).astype(F32)
        return v_of_p(p.astype(BF16)) + p_new * v_new

    meta = meta_ref[0]
    e16 = e16_ref[...]
    jexp = jnp.dot(meta[:, 0:TOP_N].astype(BF16), e16, preferred_element_type=F32)
    vexp = jnp.dot(meta[:, TOP_N:2 * TOP_N].astype(BF16), e16, preferred_element_type=F32)
    lane = lax.broadcasted_iota(jnp.int32, (1, SEL_KEYS), 1) & (PAGE_SIZE - 1)
    ji = jexp.astype(jnp.int32)
    kpos = (ji >> 1) * PAGE_SIZE + lane
    key_ok = (vexp > 0.5) & ((kpos >> 6) == ji) & (kpos < past_len)
    new_blk = past_len // SEL_BLOCK
    new_sel = jnp.max(jnp.where((meta[:, 0:TOP_N] == float(new_blk)) & (meta[:, TOP_N:2 * TOP_N] > 0.5),
                                1.0, 0.0), axis=-1, keepdims=True)
    o_sel = []
    for g in range(N_KV):
        sb = jnp.dot(q, kbuf[slot, g].astype(BF16), preferred_element_type=F32)
        sb = sb - slope * (past_len - kpos[g:g + 1]).astype(F32)
        vb = vbuf[slot, g].astype(BF16)
        o_sel.append((sb, key_ok[g:g + 1], vb))
    k_new = new_rows(2 * D_KV)
    v_new = new_rows(3 * D_KV)
    s_new = jnp.sum(qf * k_new, axis=-1, keepdims=True)
    new_on = jnp.where(first, new_sel[0:1], new_sel[1:2]) > 0.5
    outs = [attend(sb, ok, lambda pb, vb=vb: lax.dot_general(pb, vb, nt, preferred_element_type=F32),
                   s_new, new_on, v_new) for sb, ok, vb in o_sel]
    o_s = jnp.where(first, outs[0], outs[1])

    w = win_ref[0]
    wbuf = w.shape[1]
    lane_b = lax.broadcasted_iota(jnp.int32, (4 * HEAD_DIM, newt_ref.shape[1]), 1)
    new_col = jnp.sum(jnp.where(lane_b == b, newt_ref[...], 0.0), axis=-1, keepdims=True)
    lane_w = lax.broadcasted_iota(jnp.int32, (1, wbuf), 1)
    w_new = jnp.where(lane_w == wbuf - 1, new_col, pltpu.roll(w, wbuf - 1, 1))
    wout_ref[0] = w_new
    wdist = (wbuf - 1 - lane_w).astype(F32)
    wb = w_new.astype(BF16)
    ow = []
    for g in range(N_KV):
        sw = jnp.dot(q, wb[g * HEAD_DIM:(g + 1) * HEAD_DIM], preferred_element_type=F32)
        pw = _masked_softmax(sw - slope * wdist, lane_w >= 0)
        ow.append(lax.dot_general(pw.astype(BF16), wb[D_KV + g * HEAD_DIM:D_KV + (g + 1) * HEAD_DIM],
                                  nt, preferred_element_type=F32))
    o_w = jnp.where(first, ow[0], ow[1])

    gates = _sigmoid(gt_ref[0])
    y = oc_ref[0] * gates[:, 0:1] + o_s * gates[:, 1:2] + o_w * gates[:, 2:3]
    ms = jnp.sum(jnp.sum(y * y, axis=-1, keepdims=True), axis=0, keepdims=True) / D_Q
    ha_ref[0] = (y * lax.rsqrt(ms + 1e-6)) * nrm_ref[...]


def _sample_attn(koff, voff, cache2d, q3, meta3, new3, newt, win3, oc3, gt3, nrm8, e16, past_len):
    b = q3.shape[0]
    wbuf = win3.shape[2]
    full = lambda a: pl.BlockSpec(a.shape, lambda i, ko, vo: (0,) * a.ndim)
    per_b = lambda a: pl.BlockSpec((1,) + a.shape[1:], lambda i, ko, vo: (i,) + (0,) * (a.ndim - 1))
    return pl.pallas_call(
        functools.partial(_sample_attn_kernel, past_len=past_len),
        grid_spec=pltpu.PrefetchScalarGridSpec(
            num_scalar_prefetch=2, grid=(b,),
            in_specs=[pl.BlockSpec(memory_space=pl.ANY), per_b(q3), per_b(meta3), per_b(new3),
                      full(newt), per_b(win3), per_b(oc3), per_b(gt3), full(nrm8), full(e16)],
            out_specs=[pl.BlockSpec((1, N_HEADS, HEAD_DIM), lambda i, ko, vo: (i, 0, 0)),
                       pl.BlockSpec((1, 4 * HEAD_DIM, wbuf), lambda i, ko, vo: (i, 0, 0))],
            scratch_shapes=[pltpu.VMEM((2, N_KV, HEAD_DIM, SEL_KEYS), F32),
                            pltpu.VMEM((2, N_KV, HEAD_DIM, SEL_KEYS), F32),
                            pltpu.SemaphoreType.DMA((2,))]),
        out_shape=[jax.ShapeDtypeStruct((b, N_HEADS, HEAD_DIM), F32),
                   jax.ShapeDtypeStruct((b, 4 * HEAD_DIM, wbuf), F32)],
        compiler_params=pltpu.CompilerParams(dimension_semantics=("arbitrary",),
                                             vmem_limit_bytes=VMEM_LIMIT),
        name="sample_attn",
    )(koff, voff, cache2d, q3, meta3, new3, newt, win3, oc3, gt3, nrm8, e16)


def _block_diag(w):
    n, a, b = w.shape
    eye = jnp.eye(n, dtype=w.dtype)
    return (eye[:, None, :, None] * w[:, :, None, :]).reshape(n * a, n * b)


def _cmp_weights(pe, w1, w2):
    pe_cat = jnp.tile(pe, (1, N_KV)).reshape(1, CMP_LEN * D_KV)
    eye = jnp.eye(N_KV, dtype=w1.dtype)
    w1_big = (w1[:, None, :, None, :] * eye[None, :, None, :, None]).reshape(
        CMP_LEN * D_KV, N_KV * CMP_HID)
    w2_big = (w2[None, :, None, :] * eye[:, None, :, None]).reshape(N_KV * CMP_HID, D_KV)
    return pe_cat, w1_big.astype(BF16), w2_big.astype(BF16)


def _sel_matrix(n_cmp, n_lane):
    n = np.arange(n_cmp)[:, None]
    j = np.arange(n_lane)[None, :]
    lo = SEL_RATIO * j - (CMP_LEN // CMP_STRIDE - 1)
    return jnp.asarray(((n >= lo) & (n <= lo + SEL_RATIO)).astype(np.float32), dtype=BF16)


def kernel(x_prompt, x_sample, cache_kv, state_win, state_conv, state_h, page_table, c_prompt, c_sample, w_ada, b_ada, w_in, conv_w, conv_b, rg_wa, rg_ba, rg_wx, rg_bx, rg_lam, cmp_pe_k, cmp_w1_k, cmp_w2_k, cmp_pe_v, cmp_w1_v, cmp_w2_v, norm_rg, norm_attn, w_out, ln1_g, ln1_b, w_up, w_down, ln2_g, ln2_b):
    depth = w_in.shape[0]
    alpha = float((2.0 * depth) ** 0.25)
    bp, t, _ = x_prompt.shape
    assert bp == 1
    bs = x_sample.shape[0]
    assert x_sample.shape[1] == 1
    n_pages = page_table.shape[1]
    past_len = n_pages * PAGE_SIZE
    n_pool = cache_kv.shape[1]
    wbuf = state_win.shape[2]
    assert wbuf == WINDOW and t % CMP_ROWS == 0 and t % SEL_CHUNK == 0 and t >= WIN_KEYS
    assert n_pages % CMP_PARTS == 0 and (n_pages // CMP_PARTS) * PAGE_SIZE >= 2 * 16 * CMP_STRIDE

    xp = x_prompt.reshape(t, D_MODEL)
    xs = x_sample.reshape(bs, D_MODEL)
    pt_flat = page_table.reshape(-1)
    r_mod = -(-(bs + 1) // 8) * 8
    c_all = jnp.zeros((r_mod, D_MODEL), F32).at[0:bs].set(c_sample).at[bs:bs + 1].set(c_prompt)
    vec = lambda a: a.reshape(1, -1)

    outs = [[] for _ in range(8)]
    for l in range(depth):
        mod = _modulation(c_all, w_ada[l], b_ada[l])
        mod_s = [mod[0:bs, k * D_MODEL:(k + 1) * D_MODEL] for k in range(6)]
        mod_p = [mod[bs:bs + 1, k * D_MODEL:(k + 1) * D_MODEL] for k in range(6)]
        w_l = w_in[l]
        w_q = jnp.pad(w_l[:, 2 * D_RNN:2 * D_RNN + D_Q].reshape(D_MODEL, N_HEADS, HEAD_DIM),
                      ((0, 0), (0, 0), (0, Q_SLOT - HEAD_DIM))).reshape(D_MODEL, Q_PAD)
        w_in_b = jnp.concatenate(
            [w_l[:, 0:2 * D_RNN], w_q, w_l[:, 2 * D_RNN + D_Q:],
             jnp.zeros((D_MODEL, GATE_PAD - D_GATE), F32)], axis=1).astype(BF16)
        wrg = jnp.concatenate([_block_diag(rg_wa[l]), _block_diag(rg_wx[l])], axis=1).astype(BF16)
        pek, wk, w2k = _cmp_weights(cmp_pe_k[l], cmp_w1_k[l], cmp_w2_k[l])
        pev, wv, w2v = _cmp_weights(cmp_pe_v[l], cmp_w1_v[l], cmp_w2_v[l])
        wo_b, wup_b, wdn_b = w_out[l].astype(BF16), w_up[l].astype(BF16), w_down[l].astype(BF16)
        rg_args = (conv_w[l], vec(conv_b[l]), wrg, vec(rg_ba[l]), vec(rg_bx[l]), vec(rg_lam[l]),
                   vec(norm_rg[l]))
        ffn_w = (wo_b, vec(ln1_g[l]), vec(ln1_b[l]), wup_b, wdn_b, vec(ln2_g[l]), vec(ln2_b[l]))

        xg, q, rows, gts, vs, vw, kvt, kvwt, kst, kwt = _in_proj(xp, mod_p[0], mod_p[1], w_in_b, 512,
                                                                 2 * D_KV)
        hr, tail, hl = _rg_prompt(xg, *rg_args, tc=256)
        kct, vc = _cmp_prompt(rows, pek, pev, wk, wv, w2k, w2v)
        msel_p = _sel_matrix(t // CMP_STRIDE, t // SEL_BLOCK)
        ha = _attn_prompt(q, gts, kst, vs, kwt, vw, kct, vc, msel_p, vec(norm_attn[l]))
        xp = _merge_ffn(xp, hr, ha, mod_p[2], mod_p[3], mod_p[4], mod_p[5], *ffn_w,
                        tm=512, alpha=alpha)
        outs[0].append(kvt.reshape(4, N_KV, HEAD_DIM, t).transpose(3, 0, 1, 2)[None])
        outs[2].append(kvwt[:, t - WINDOW:].reshape(2, N_KV, HEAD_DIM, WINDOW)
                       .transpose(3, 0, 1, 2)[None])
        outs[4].append(tail[8 - (CONV_W - 1):][None])
        outs[6].append(hl[0:1])

        xg, q, rows, gts, _, _, kvt, kvwt, _, _ = _in_proj(xs, mod_s[0], mod_s[1], w_in_b, bs,
                                                           6 * D_KV)
        sconv = state_conv[l]
        hr, h_new = _rg_sample(xg, sconv[:, 0], sconv[:, 1], sconv[:, 2], state_h[l], *rg_args)
        cache2d = cache_kv[l].transpose(0, 2, 3, 4, 1).reshape(n_pool * CACHE_ROWS_PER_PAGE, PAGE_SIZE)
        n_lane = -(-(past_len // SEL_BLOCK + 1) // LANE) * LANE
        msel_s = _sel_matrix(past_len // CMP_STRIDE, n_lane)
        q3 = q.reshape(bs, N_HEADS, Q_SLOT)[:, :, 0:HEAD_DIM]
        new3 = rows.reshape(bs, 1, 6 * D_KV)
        oc3, pslc = _sample_cmp(pt_flat, cache2d, q3, new3, pek, pev, wk, wv, w2k, w2v, msel_s, n_pages)
        meta = _sample_topk(pslc.reshape(bs * N_KV, n_lane), past_len)
        picks = meta[:, 0:TOP_N].astype(jnp.int32).reshape(bs, N_KV * TOP_N)
        pool = jnp.take_along_axis(page_table, jnp.minimum(picks // 2, n_pages - 1), axis=1)
        grp = (jnp.arange(N_KV * TOP_N, dtype=jnp.int32) // TOP_N)[None, :]
        koff = (pool * CACHE_ROWS_PER_PAGE + (2 * N_KV + grp) * HEAD_DIM).reshape(-1)
        voff = koff + N_KV * HEAD_DIM
        win3 = state_win[l].transpose(0, 2, 3, 4, 1).reshape(bs, 4 * HEAD_DIM, wbuf)
        e16 = jnp.asarray(np.kron(np.eye(TOP_N, dtype=np.float32), np.ones((1, PAGE_SIZE), np.float32)),
                          dtype=BF16)
        ha3, wnew = _sample_attn(koff, voff, cache2d, q3, meta.reshape(bs, N_KV, META_W), new3,
                                 kvwt, win3, oc3, gts[:, 0:D_GATE].reshape(bs, N_HEADS, 3),
                                 norm_attn[l].reshape(N_HEADS, HEAD_DIM), e16, past_len)
        ha = ha3.reshape(bs, D_Q).astype(BF16)
        xs = _merge_ffn(xs, hr, ha, mod_s[2], mod_s[3], mod_s[4], mod_s[5], *ffn_w,
                        tm=bs, alpha=alpha)
        outs[1].append(kvt.reshape(4, N_KV, HEAD_DIM, bs).transpose(3, 0, 1, 2)[:, None])
        outs[3].append(wnew.reshape(bs, 2, N_KV, HEAD_DIM, wbuf).transpose(0, 4, 1, 2, 3))
        outs[5].append(jnp.stack([sconv[:, 1], sconv[:, 2], xg[:, 0:D_RNN]], axis=1))
        outs[7].append(h_new)

    stk = [jnp.stack(o) for o in outs]
    return (xp.reshape(1, t, D_MODEL), xs.reshape(bs, 1, D_MODEL), stk[0], stk[1], stk[2], stk[3],
            stk[4], stk[5], stk[6], stk[7])
```

```python
import functools

import numpy as np
import jax
import jax.numpy as jnp
from jax import lax
from jax.experimental import pallas as pl
from jax.experimental.pallas import tpu as pltpu

F32 = jnp.float32
BF16 = jnp.bfloat16

D_MODEL = 1024
D_RNN = D_MODEL // 2
RNN_BLOCKS = 8
RNN_BLOCK = D_RNN // RNN_BLOCKS
CONV_W = 4
RG_C = 8.0
HEAD_DIM = 64
N_HEADS = (D_MODEL - D_RNN) // HEAD_DIM
N_KV = 2
GQA_R = N_HEADS // N_KV
D_Q = N_HEADS * HEAD_DIM
D_KV = N_KV * HEAD_DIM
D_GATE = 3 * N_HEADS
CMP_STRIDE = 16
CMP_LEN = 2 * CMP_STRIDE
CMP_HID = 128
SEL_BLOCK = 64
SEL_RATIO = SEL_BLOCK // CMP_STRIDE
TOP_N = 16
WINDOW = 512
Q_BLOCK = 128
FORCE_SCORE = 1.0e4
PAGE_SIZE = 128
D_FF = ((8 * D_MODEL // 3 + 255) // 256) * 256
SCALE = HEAD_DIM ** -0.5

NEG = -1e30
LANE = 128

Q_SLOT = LANE
Q_PAD = N_HEADS * Q_SLOT
OFF_Q = 2 * D_RNN
OFF_KV = OFF_Q + Q_PAD
OFF_GATE = OFF_KV + 6 * D_KV
GATE_PAD = 128
D_IN_PAD = OFF_GATE + GATE_PAD

SEL_CHUNK = 1024
BLK_PER_CHUNK = SEL_CHUNK // SEL_BLOCK
POS_SPLIT = 16
AUG_HI = HEAD_DIM
AUG_LO = HEAD_DIM + 1
AUG_PEN = HEAD_DIM + 2
VMEM_LIMIT = 56 * 1024 * 1024


def _sigmoid(x):
    return 1.0 / (1.0 + jnp.exp(-x))


def _gelu_tanh(x):
    c = np.float32(np.sqrt(2.0 / np.pi))
    return 0.5 * x * (1.0 + jnp.tanh(c * (x + np.float32(0.044715) * (x * x * x))))


def _softplus(x):
    return jnp.maximum(x, 0.0) + jnp.log1p(jnp.exp(-jnp.abs(x)))


def _pow2_neg(e_int):
    return lax.bitcast_convert_type((127 - e_int) << 23, F32)


def _masked_softmax(s, mask):
    s = jnp.where(mask, s, NEG)
    m = jnp.max(s, axis=-1, keepdims=True)
    p = jnp.exp(s - m)
    den = jnp.maximum(jnp.sum(p, axis=-1, keepdims=True), 1e-30)
    return p * jnp.where(m > 0.5 * NEG, 1.0 / den, 0.0)


def _split3_dot(x, m_bf16):
    hi = x.astype(BF16)
    r1 = x - hi.astype(F32)
    mid = r1.astype(BF16)
    lo = (r1 - mid.astype(F32)).astype(BF16)
    d = functools.partial(jnp.dot, preferred_element_type=F32)
    return d(hi, m_bf16) + d(mid, m_bf16) + d(lo, m_bf16)


def _topk_select(score, blkf, k):
    work = score
    idxs, vals = [], []
    for _ in range(k):
        m = jnp.max(work, axis=1, keepdims=True)
        idx = jnp.min(jnp.where(work == m, blkf, 1e9), axis=1, keepdims=True)
        work = jnp.where(blkf == idx, -2.0, work)
        idxs.append(idx)
        vals.append(m)
    return idxs, vals


def _topk_member(score, blkf, k):
    work = score
    for _ in range(k):
        m = jnp.max(work, axis=1, keepdims=True)
        idx = jnp.min(jnp.where(work == m, blkf, 1e9), axis=1, keepdims=True)
        work = jnp.where(blkf == idx, -2.0, work)
    return jnp.where((work < -1.5) & (score >= 0.0), 1.0, 0.0)


def _mod_kernel(c_ref, w_ref, b_ref, o_ref):
    c = c_ref[...]
    a = (c * _sigmoid(c)).astype(BF16)
    o_ref[...] = jnp.dot(a, w_ref[...].astype(BF16), preferred_element_type=F32) + b_ref[...]


def _modulation(c_all, w_ada, b_ada):
    r, n = c_all.shape[0], w_ada.shape[1]
    tn = 512
    return pl.pallas_call(
        _mod_kernel,
        grid=(n // tn,),
        in_specs=[pl.BlockSpec((r, D_MODEL), lambda j: (0, 0)),
                  pl.BlockSpec((D_MODEL, tn), lambda j: (0, j)),
                  pl.BlockSpec((1, tn), lambda j: (0, j))],
        out_specs=pl.BlockSpec((r, tn), lambda j: (0, j)),
        out_shape=jax.ShapeDtypeStruct((r, n), F32),
        name="adaln_mod",
    )(c_all, w_ada, b_ada.reshape(1, n))


def _inproj_kernel(x_ref, sh_ref, sc_ref, w_ref, qc_ref, xg_ref, q_ref, rows_ref, gt_ref, vs_ref,
                   vw_ref, kvt_ref, kvwt_ref, kst_ref, kwt_ref, *, n_row_cols):
    u = x_ref[...] * (1.0 + sc_ref[...]) + sh_ref[...]
    p = jnp.dot(u.astype(BF16), w_ref[...], preferred_element_type=F32)
    xg_ref[...] = p[:, 0:OFF_Q]
    q_ref[...] = (p[:, OFF_Q:OFF_KV] * SCALE + qc_ref[...]).astype(BF16)
    kv = p[:, OFF_KV:OFF_GATE]
    rows_ref[...] = kv[:, 0:n_row_cols]
    gt_ref[...] = p[:, OFF_GATE:D_IN_PAD]
    vs_ref[...] = kv[:, 3 * D_KV:4 * D_KV].astype(BF16)
    vw_ref[...] = kv[:, 5 * D_KV:6 * D_KV].astype(BF16)
    kvt = kv.T
    kvt_ref[...] = kvt[0:4 * D_KV]
    kvwt_ref[...] = kvt[4 * D_KV:6 * D_KV]
    kst_ref[...] = kvt[2 * D_KV:3 * D_KV].astype(BF16)
    kwt_ref[...] = kvt[4 * D_KV:5 * D_KV].astype(BF16)


def _q_consts():
    qc = np.zeros((1, Q_PAD), np.float32)
    for h in range(N_HEADS):
        slope = 2.0 ** (-8.0 * (h + 1) / N_HEADS)
        qc[0, h * Q_SLOT + AUG_HI] = POS_SPLIT * slope
        qc[0, h * Q_SLOT + AUG_LO] = slope
    return jnp.asarray(qc)


def _in_proj(x2d, shift, scale, w_bf16, tm, n_row_cols):
    r = x2d.shape[0]
    rm = shift.shape[0]
    mod_spec = (pl.BlockSpec((1, D_MODEL), lambda i: (0, 0)) if rm == 1
                else pl.BlockSpec((tm, D_MODEL), lambda i: (i, 0)))
    row = lambda w: pl.BlockSpec((tm, w), lambda i: (i, 0))
    col = lambda h: pl.BlockSpec((h, tm), lambda i: (0, i))
    return pl.pallas_call(
        functools.partial(_inproj_kernel, n_row_cols=n_row_cols),
        grid=(r // tm,),
        in_specs=[row(D_MODEL), mod_spec, mod_spec,
                  pl.BlockSpec((D_MODEL, D_IN_PAD), lambda i: (0, 0)),
                  pl.BlockSpec((1, Q_PAD), lambda i: (0, 0))],
        out_specs=[row(OFF_Q), row(Q_PAD), row(n_row_cols), row(GATE_PAD), row(D_KV), row(D_KV),
                   col(4 * D_KV), col(2 * D_KV), col(D_KV), col(D_KV)],
        out_shape=[jax.ShapeDtypeStruct((r, OFF_Q), F32),
                   jax.ShapeDtypeStruct((r, Q_PAD), BF16),
                   jax.ShapeDtypeStruct((r, n_row_cols), F32),
                   jax.ShapeDtypeStruct((r, GATE_PAD), F32),
                   jax.ShapeDtypeStruct((r, D_KV), BF16),
                   jax.ShapeDtypeStruct((r, D_KV), BF16),
                   jax.ShapeDtypeStruct((4 * D_KV, r), F32),
                   jax.ShapeDtypeStruct((2 * D_KV, r), F32),
                   jax.ShapeDtypeStruct((D_KV, r), BF16),
                   jax.ShapeDtypeStruct((D_KV, r), BF16)],
        compiler_params=pltpu.CompilerParams(dimension_semantics=("arbitrary",),
                                             vmem_limit_bytes=VMEM_LIMIT),
        name="in_proj",
    )(x2d, shift, scale, w_bf16, _q_consts())


def _rg_gates(xc, wrg_ref, ba_ref, bx_ref, lam_ref):
    g = jnp.dot(xc.astype(BF16), wrg_ref[...], preferred_element_type=F32)
    r = _sigmoid(g[:, 0:D_RNN] + ba_ref[...])
    ig = _sigmoid(g[:, D_RNN:2 * D_RNN] + bx_ref[...])
    log_a = -RG_C * r * _softplus(-lam_ref[...])
    a = jnp.exp(log_a)
    b = jnp.sqrt(-jnp.tanh(log_a) * (a * a + 1.0)) * (ig * xc)
    return a, b


def _rg_out(h, gr, nrm_ref):
    y = h * _gelu_tanh(gr)
    return (y * lax.rsqrt(jnp.mean(y * y, axis=-1, keepdims=True) + 1e-6)) * nrm_ref[...]


def _rg_prompt_kernel(xg_ref, cw_ref, cb_ref, wrg_ref, ba_ref, bx_ref, lam_ref, nrm_ref,
                      y_ref, tail_ref, hl_ref, xp_s, a_s, b_s, h_s, hc_s, *, tc):
    i = pl.program_id(0)

    @pl.when(i == 0)
    def _():
        xp_s[0:8, :] = jnp.zeros((8, D_RNN), F32)
        hc_s[...] = jnp.zeros((8, D_RNN), F32)

    xr = xg_ref[:, 0:D_RNN]
    gr = xg_ref[:, D_RNN:2 * D_RNN]
    xp_s[8:8 + tc, :] = xr
    cw = cw_ref[...]
    xc = (cb_ref[...] + cw[0:1] * xp_s[5:5 + tc, :] + cw[1:2] * xp_s[6:6 + tc, :]
          + cw[2:3] * xp_s[7:7 + tc, :] + cw[3:4] * xr)
    a, b = _rg_gates(xc, wrg_ref, ba_ref, bx_ref, lam_ref)
    a_s[...] = a
    b_s[...] = b
    rowi = lax.broadcasted_iota(jnp.int32, (8, D_RNN), 0)

    def tile(gi, hc):
        r0 = pl.multiple_of(gi * 8, 8)
        at = a_s[pl.ds(r0, 8), :]
        bt = b_s[pl.ds(r0, 8), :]
        for d in (1, 2, 4):
            keep = rowi >= d
            a_sh = jnp.where(keep, pltpu.roll(at, d, 0), 1.0)
            b_sh = jnp.where(keep, pltpu.roll(bt, d, 0), 0.0)
            bt = at * b_sh + bt
            at = at * a_sh
        h = at * hc + bt
        h_s[pl.ds(r0, 8), :] = h
        return h[7:8, :]

    hc = lax.fori_loop(0, tc // 8, tile, hc_s[0:1, :], unroll=4)
    hc_s[0:1, :] = hc
    xp_s[0:8, :] = xr[tc - 8:tc]
    y_ref[...] = _rg_out(h_s[...], gr, nrm_ref).astype(BF16)
    tail_ref[...] = xr[tc - 8:tc]
    hl_ref[...] = jnp.broadcast_to(hc, (8, D_RNN))


def _rg_prompt(xg, cw, cb, wrg, ba, bx, lam, nrm, tc):
    t = xg.shape[0]
    vec = pl.BlockSpec((1, D_RNN), lambda i: (0, 0))
    return pl.pallas_call(
        functools.partial(_rg_prompt_kernel, tc=tc),
        grid=(t // tc,),
        in_specs=[pl.BlockSpec((tc, 2 * D_RNN), lambda i: (i, 0)),
                  pl.BlockSpec((CONV_W, D_RNN), lambda i: (0, 0)), vec,
                  pl.BlockSpec((D_RNN, 2 * D_RNN), lambda i: (0, 0)), vec, vec, vec, vec],
        out_specs=[pl.BlockSpec((tc, D_RNN), lambda i: (i, 0)),
                   pl.BlockSpec((8, D_RNN), lambda i: (0, 0)),
                   pl.BlockSpec((8, D_RNN), lambda i: (0, 0))],
        out_shape=[jax.ShapeDtypeStruct((t, D_RNN), BF16),
                   jax.ShapeDtypeStruct((8, D_RNN), F32),
                   jax.ShapeDtypeStruct((8, D_RNN), F32)],
        scratch_shapes=[pltpu.VMEM((tc + 8, D_RNN), F32), pltpu.VMEM((tc, D_RNN), F32),
                        pltpu.VMEM((tc, D_RNN), F32), pltpu.VMEM((tc, D_RNN), F32),
                        pltpu.VMEM((8, D_RNN), F32)],
        compiler_params=pltpu.CompilerParams(dimension_semantics=("arbitrary",)),
        name="rg_prompt",
    )(xg, cw, cb, wrg, ba, bx, lam, nrm)


def _rg_sample_kernel(xg_ref, c0_ref, c1_ref, c2_ref, h0_ref, cw_ref, cb_ref, wrg_ref, ba_ref,
                      bx_ref, lam_ref, nrm_ref, y_ref, h_ref):
    xr = xg_ref[:, 0:D_RNN]
    gr = xg_ref[:, D_RNN:2 * D_RNN]
    cw = cw_ref[...]
    xc = (cb_ref[...] + cw[0:1] * c0_ref[...] + cw[1:2] * c1_ref[...] + cw[2:3] * c2_ref[...]
          + cw[3:4] * xr)
    a, b = _rg_gates(xc, wrg_ref, ba_ref, bx_ref, lam_ref)
    h = a * h0_ref[...] + b
    h_ref[...] = h
    y_ref[...] = _rg_out(h, gr, nrm_ref).astype(BF16)


def _rg_sample(xg, c0, c1, c2, h0, cw, cb, wrg, ba, bx, lam, nrm):
    b = xg.shape[0]
    return pl.pallas_call(
        _rg_sample_kernel,
        out_shape=[jax.ShapeDtypeStruct((b, D_RNN), BF16), jax.ShapeDtypeStruct((b, D_RNN), F32)],
        name="rg_sample",
    )(xg, c0, c1, c2, h0, cw, cb, wrg, ba, bx, lam, nrm)


CMP_PITCH = 20
CMP_KSPLIT = 8


def _pitch_rows(n_groups):
    return -(-(CMP_PITCH * n_groups) // 8) * 8


def _compress_chunk(x_s, blk0, nblk, pek_ref, pev_ref, wk_ref, wv_ref, w2k_ref, w2v_ref):
    base = blk0 * CMP_PITCH
    outs = []
    for t, (pe_ref, w1_ref, w2_ref) in enumerate(((pek_ref, wk_ref, w2k_ref),
                                                   (pev_ref, wv_ref, w2v_ref))):
        h = None
        for l0 in range(0, CMP_LEN, CMP_KSPLIT):
            pieces = []
            for l in range(l0, l0 + CMP_KSPLIT):
                row = (l // CMP_STRIDE) * CMP_PITCH + l % CMP_STRIDE
                xl = x_s[t, pl.ds(base + row, nblk, stride=CMP_PITCH), :]
                pieces.append((xl + pe_ref[:, l * D_KV:(l + 1) * D_KV]).astype(BF16))
            part = jnp.dot(jnp.concatenate(pieces, axis=1), w1_ref[l0 * D_KV:(l0 + CMP_KSPLIT) * D_KV, :],
                           preferred_element_type=F32)
            h = part if h is None else h + part
        h = h * _sigmoid(h)
        outs.append(jnp.dot(h.astype(BF16), w2_ref[...], preferred_element_type=F32))
    return outs


CMP_CHUNK = 128
CMP_ROWS = CMP_CHUNK * CMP_STRIDE
CMP_PARTS = 2


def _cmp_prompt_kernel(x_ref, nxt_ref, pek_ref, pev_ref, wk_ref, wv_ref, w2k_ref, w2v_ref,
                       kct_ref, vc_ref, x_s):
    for t in range(2):
        cols = slice(t * D_KV, (t + 1) * D_KV)
        for j in range(CMP_CHUNK):
            x_s[t, CMP_PITCH * j:CMP_PITCH * j + CMP_STRIDE, :] = (
                x_ref[CMP_STRIDE * j:CMP_STRIDE * (j + 1), cols])
        x_s[t, CMP_PITCH * CMP_CHUNK:CMP_PITCH * CMP_CHUNK + CMP_STRIDE, :] = nxt_ref[:, cols]
    kc, vc = _compress_chunk(x_s, 0, CMP_CHUNK, pek_ref, pev_ref, wk_ref, wv_ref, w2k_ref, w2v_ref)
    kct_ref[...] = kc.T.astype(BF16)
    vc_ref[...] = vc.astype(BF16)


def _cmp_prompt(rows, pek, pev, wk, wv, w2k, w2v):
    t = rows.shape[0]
    n_steps = t // CMP_ROWS
    last16 = t // CMP_STRIDE - 1
    full = lambda a: pl.BlockSpec(a.shape, lambda i: (0, 0))
    return pl.pallas_call(
        _cmp_prompt_kernel,
        grid=(n_steps,),
        in_specs=[pl.BlockSpec((CMP_ROWS, 2 * D_KV), lambda i: (i, 0)),
                  pl.BlockSpec((CMP_STRIDE, 2 * D_KV),
                               lambda i: (jnp.minimum((i + 1) * CMP_CHUNK, last16), 0)),
                  full(pek), full(pev), full(wk), full(wv), full(w2k), full(w2v)],
        out_specs=[pl.BlockSpec((D_KV, CMP_CHUNK), lambda i: (0, i)),
                   pl.BlockSpec((CMP_CHUNK, D_KV), lambda i: (i, 0))],
        out_shape=[jax.ShapeDtypeStruct((D_KV, t // CMP_STRIDE), BF16),
                   jax.ShapeDtypeStruct((t // CMP_STRIDE, D_KV), BF16)],
        scratch_shapes=[pltpu.VMEM((2, _pitch_rows(CMP_CHUNK + 1), D_KV), F32)],
        compiler_params=pltpu.CompilerParams(dimension_semantics=("arbitrary",),
                                             vmem_limit_bytes=VMEM_LIMIT),
        name="cmp_prompt",
    )(rows, rows, pek, pev, wk, wv, w2k, w2v)


WIN_KEYS = WINDOW + Q_BLOCK
CMP_PREFIX = 256
HEAD_BLKS = LANE // SEL_BLOCK
AUG_HEAD_PEN = AUG_PEN + BLK_PER_CHUNK
AUG_ORIGIN = AUG_HEAD_PEN + HEAD_BLKS
OWN_STEP = 256


def _sel_aug_rows():
    k = np.arange(SEL_CHUNK)
    c = np.zeros((HEAD_DIM, SEL_CHUNK), np.float32)
    c[AUG_HI - HEAD_DIM] = k // POS_SPLIT
    c[AUG_LO - HEAD_DIM] = k % POS_SPLIT
    for b in range(BLK_PER_CHUNK):
        c[AUG_PEN - HEAD_DIM + b] = (k // SEL_BLOCK == b)
    kh = np.arange(HEAD_BLKS * SEL_BLOCK)
    h = np.zeros((HEAD_DIM, kh.size), np.float32)
    h[AUG_HI - HEAD_DIM] = kh // POS_SPLIT
    h[AUG_LO - HEAD_DIM] = kh % POS_SPLIT
    for b in range(HEAD_BLKS):
        h[AUG_HEAD_PEN - HEAD_DIM + b] = (kh // SEL_BLOCK == b)
    h[AUG_ORIGIN - HEAD_DIM] = 1.0
    return jnp.asarray(c, dtype=BF16), jnp.asarray(h, dtype=BF16)


def _attn_prompt_kernel(q_ref, gt_ref, kst_ref, vs_ref, kwt_ref, vw_ref, kct_ref, vc_ref,
                        msel_ref, cst_ref, csth_ref, nrm_ref, o_ref, act_ref, m_s, l_s, acc_s, oc_s,
                        sc_s):
    i = pl.program_id(0)
    t0 = i * Q_BLOCK
    n_cmp = kct_ref.shape[1]
    n_sel = msel_ref.shape[1]
    rows = GQA_R * Q_BLOCK
    dot = functools.partial(jnp.dot, preferred_element_type=F32)
    qpos = t0 + lax.broadcasted_iota(jnp.int32, (Q_BLOCK, 1), 0)
    qpos4 = jnp.concatenate([qpos] * GQA_R, axis=0)
    qposf4 = qpos4.astype(F32)
    cend = lax.broadcasted_iota(jnp.int32, (1, n_cmp), 1) * CMP_STRIDE + (CMP_LEN - 1)
    blk = lax.broadcasted_iota(jnp.int32, (1, n_sel), 1)
    blkf = blk.astype(F32)
    gates = _sigmoid(gt_ref[...])
    cur = qpos >> 6
    forced = (blk == 0) | (blk == cur) | (blk == cur - 1)
    in_past = blk * SEL_BLOCK <= qpos
    n_chunks = (t0 + Q_BLOCK + SEL_CHUNK - 1) // SEL_CHUNK
    win0 = pl.multiple_of(jnp.maximum(t0 - WINDOW, 0), LANE)
    wpos = win0 + lax.broadcasted_iota(jnp.int32, (1, WIN_KEYS), 1)
    wdist = qpos4 - wpos
    wmask = (wdist >= 0) & (wdist < WINDOW)
    wdistf = wdist.astype(F32)

    qa, q64, slope4 = [], [], []
    for g in range(N_KV):
        qg = jnp.concatenate([q_ref[:, (g * GQA_R + r) * Q_SLOT:(g * GQA_R + r + 1) * Q_SLOT]
                              for r in range(GQA_R)], axis=0)
        qa.append(qg)
        q64.append(qg[:, 0:HEAD_DIM])
        slope4.append(jnp.concatenate(
            [jnp.full((Q_BLOCK, 1), np.float32(2.0 ** -(g * GQA_R + r + 1)), F32)
             for r in range(GQA_R)], axis=0))

    def cmp_branch(n_used):
        cend_u = cend[:, 0:n_used]
        dist = qposf4 - cend_u.astype(F32)
        for g in range(N_KV):
            gs = slice(g * HEAD_DIM, (g + 1) * HEAD_DIM)
            s = dot(q64[g], kct_ref[gs, 0:n_used])
            p = _masked_softmax(s - slope4[g] * dist, cend_u <= qpos4)
            oc_s[g * rows:(g + 1) * rows, :] = dot(p.astype(BF16), vc_ref[0:n_used, :])
            psum = p[0:Q_BLOCK]
            for r in range(1, GQA_R):
                psum = psum + p[r * Q_BLOCK:(r + 1) * Q_BLOCK]
            p_slc = _split3_dot(psum, msel_ref[0:n_used, :])
            sc_s[g * Q_BLOCK:(g + 1) * Q_BLOCK, :] = jnp.where(
                in_past, jnp.where(forced, FORCE_SCORE, p_slc), -1.0)

    n_prefix = max(1, n_cmp // CMP_PREFIX)
    per = n_cmp // n_prefix
    need = (t0 + Q_BLOCK - CMP_LEN) // CMP_STRIDE + 1
    bucket = jnp.clip((need + per - 1) // per - 1, 0, n_prefix - 1)
    for k in range(n_prefix):
        @pl.when(bucket == k)
        def _(k=k):
            cmp_branch((k + 1) * per)

    o_c = [oc_s[g * rows:(g + 1) * rows, g * HEAD_DIM:(g + 1) * HEAD_DIM] for g in range(N_KV)]

    o_w = []
    for g in range(N_KV):
        gs = slice(g * HEAD_DIM, (g + 1) * HEAD_DIM)
        sw = dot(q64[g], kwt_ref[gs, pl.ds(win0, WIN_KEYS)])
        pw = _masked_softmax(sw - slope4[g] * wdistf, wmask)
        o_w.append(dot(pw.astype(BF16), vw_ref[pl.ds(win0, WIN_KEYS), :])[:, gs])

    sel = _topk_member(sc_s[...], blkf, min(TOP_N, n_sel))
    pen_all = jnp.where(sel > 0.5, 0.0, NEG)
    head_lane = blk < HEAD_BLKS
    pen_head = pen_all.astype(BF16)
    pen = jnp.where(head_lane, NEG, pen_all).astype(BF16)
    penh = [pen_head[g * Q_BLOCK:(g + 1) * Q_BLOCK] for g in range(N_KV)]
    penb = [pen[g * Q_BLOCK:(g + 1) * Q_BLOCK] for g in range(N_KV)]
    sel = jnp.where(head_lane, 0.0, sel)
    n_chunks_all = n_sel // BLK_PER_CHUNK
    for g in range(N_KV):
        col_any = jnp.max(sel[g * Q_BLOCK:(g + 1) * Q_BLOCK], axis=0, keepdims=True)
        for c in range(n_chunks_all):
            hit = jnp.max(col_any[:, c * BLK_PER_CHUNK:(c + 1) * BLK_PER_CHUNK])
            act_ref[g * n_chunks_all + c] = (hit > 0.5).astype(jnp.int32)

    cst = cst_ref[...]
    oh_lane = lax.broadcasted_iota(jnp.int32, (n_sel, LANE), 1)
    oh_base = lax.broadcasted_iota(jnp.int32, (n_sel, LANE), 0) - oh_lane + AUG_PEN
    oh_ok = (oh_lane >= AUG_PEN) & (oh_lane < AUG_PEN + BLK_PER_CHUNK)

    def placed_penalty(c, g):
        onehot = jnp.where((oh_base == c * BLK_PER_CHUNK) & oh_ok, 1.0, 0.0).astype(BF16)
        return dot(penb[g], onehot).astype(BF16)

    def chunk_scores(c, g, causal, width):
        k0 = pl.multiple_of(c * SEL_CHUNK, SEL_CHUNK)
        placed = placed_penalty(c, g)
        qaug = qa[g] + jnp.concatenate([placed] * GQA_R, axis=0)
        kta = jnp.concatenate(
            [kst_ref[g * HEAD_DIM:(g + 1) * HEAD_DIM, pl.ds(k0, width)], cst[:, 0:width]], axis=0)
        s = dot(qaug, kta)
        if causal:
            kpos = k0 + lax.broadcasted_iota(jnp.int32, (1, width), 1)
            tri = jnp.where(kpos > qpos, NEG, 0.0)
            s = s + jnp.concatenate([tri] * GQA_R, axis=0)
        return s

    def chunk_update(c, g, s):
        k0 = pl.multiple_of(c * SEL_CHUNK, SEL_CHUNK)
        crow = slope4[g] * (k0.astype(F32) - qposf4)
        rg = slice(g * rows, (g + 1) * rows)
        m = m_s[rg, :]
        m_new = jnp.maximum(m, jnp.max(s, axis=-1, keepdims=True) + crow)
        p = jnp.exp(s + (crow - m_new))
        alpha = jnp.exp(m - m_new)
        return (m_new, alpha * l_s[rg, :] + jnp.sum(p, axis=-1, keepdims=True),
                alpha * acc_s[rg, :] + dot(p.astype(BF16), vs_ref[pl.ds(k0, s.shape[1]), :]))

    def fold(c, groups, causal, width=SEL_CHUNK):
        scores = [chunk_scores(c, g, causal, width) for g in groups]
        new = [chunk_update(c, g, s) for g, s in zip(groups, scores)]
        rg = slice(groups[0] * rows, (groups[-1] + 1) * rows)
        for ref, k in ((m_s, 0), (l_s, 1), (acc_s, 2)):
            ref[rg, :] = jnp.concatenate([n[k] for n in new], axis=0)

    head_keys = HEAD_BLKS * SEL_BLOCK
    head_oh = jnp.where((oh_base == -BLK_PER_CHUNK) & (oh_lane >= AUG_HEAD_PEN)
                        & (oh_lane < AUG_HEAD_PEN + HEAD_BLKS), 1.0, 0.0).astype(BF16)
    csth = csth_ref[...]
    own_c = n_chunks - 1
    own_k0 = pl.multiple_of(own_c * SEL_CHUNK, SEL_CHUNK)
    own_k0f = own_k0.astype(F32)
    lane_q = lax.broadcasted_iota(jnp.int32, (rows, LANE), 1)

    def own_pass(width):
        kpos = jnp.concatenate([lax.broadcasted_iota(jnp.int32, (1, head_keys), 1),
                                own_k0 + lax.broadcasted_iota(jnp.int32, (1, width), 1)], axis=1)
        tri = jnp.concatenate([jnp.where(kpos > qpos, NEG, 0.0)] * GQA_R, axis=0)
        new = []
        for g in range(N_KV):
            gr = slice(g * HEAD_DIM, (g + 1) * HEAD_DIM)
            placed = placed_penalty(own_c, g) + dot(penh[g], head_oh).astype(BF16)
            origin = jnp.where(lane_q == AUG_ORIGIN, -slope4[g] * own_k0f, 0.0).astype(BF16)
            qaug = qa[g] + jnp.concatenate([placed] * GQA_R, axis=0) + origin
            kta = jnp.concatenate(
                [jnp.concatenate([kst_ref[gr, 0:head_keys], csth], axis=0),
                 jnp.concatenate([kst_ref[gr, pl.ds(own_k0, width)], cst[:, 0:width]], axis=0)],
                axis=1)
            s = dot(qaug, kta) + tri
            crow = slope4[g] * (own_k0f - qposf4)
            m0 = jnp.max(s, axis=-1, keepdims=True) + crow
            p = jnp.exp(s + (crow - m0))
            vcat = jnp.concatenate([vs_ref[0:head_keys, :], vs_ref[pl.ds(own_k0, width), :]], axis=0)
            new.append((m0, jnp.sum(p, axis=-1, keepdims=True), dot(p.astype(BF16), vcat)))
        for ref, k in ((m_s, 0), (l_s, 1), (acc_s, 2)):
            ref[...] = jnp.concatenate([n[k] for n in new], axis=0)

    own_keys = t0 + Q_BLOCK - own_k0
    for width in range(OWN_STEP, SEL_CHUNK + 1, OWN_STEP):
        @pl.when((own_keys > width - OWN_STEP) & (own_keys <= width))
        def _(width=width):
            own_pass(width)

    def past_chunk(c, carry):
        a0 = act_ref[c] > 0
        a1 = act_ref[n_chunks_all + c] > 0

        @pl.when(a0 & a1)
        def _():
            fold(c, (0, 1), causal=False)

        @pl.when(a0 & jnp.logical_not(a1))
        def _():
            fold(c, (0,), causal=False)

        @pl.when(a1 & jnp.logical_not(a0))
        def _():
            fold(c, (1,), causal=False)

        return carry

    lax.fori_loop(0, n_chunks - 1, past_chunk, 0)

    head_out = []
    for g in range(N_KV):
        gs = slice(g * HEAD_DIM, (g + 1) * HEAD_DIM)
        rg = slice(g * rows, (g + 1) * rows)
        o_s = (acc_s[rg, :] * (1.0 / jnp.maximum(l_s[rg, :], 1e-30)))[:, gs]
        for r in range(GQA_R):
            h = g * GQA_R + r
            rs = slice(r * Q_BLOCK, (r + 1) * Q_BLOCK)
            head_out.append(o_c[g][rs] * gates[:, 3 * h:3 * h + 1]
                            + o_s[rs] * gates[:, 3 * h + 1:3 * h + 2]
                            + o_w[g][rs] * gates[:, 3 * h + 2:3 * h + 3])
    y = jnp.concatenate(head_out, axis=1)
    y = (y * lax.rsqrt(jnp.mean(y * y, axis=-1, keepdims=True) + 1e-6)) * nrm_ref[...]
    o_ref[...] = y.astype(BF16)


def _attn_prompt(q, gates, kst, vs, kwt, vw, kct, vc, msel, nrm):
    t = q.shape[0]
    cst, csth = _sel_aug_rows()
    full = lambda a: pl.BlockSpec(a.shape, lambda i: (0, 0), pipeline_mode=pl.Buffered(1))
    return pl.pallas_call(
        _attn_prompt_kernel,
        grid=(t // Q_BLOCK,),
        in_specs=[pl.BlockSpec((Q_BLOCK, Q_PAD), lambda i: (i, 0)),
                  pl.BlockSpec((Q_BLOCK, GATE_PAD), lambda i: (i, 0)),
                  full(kst), full(vs), full(kwt), full(vw), full(kct), full(vc), full(msel),
                  full(cst), full(csth), full(nrm)],
        out_specs=pl.BlockSpec((Q_BLOCK, D_Q), lambda i: (i, 0)),
        out_shape=jax.ShapeDtypeStruct((t, D_Q), BF16),
        scratch_shapes=[pltpu.SMEM((N_KV * (t // SEL_CHUNK),), jnp.int32),
                        pltpu.VMEM((N_KV * GQA_R * Q_BLOCK, 1), F32),
                        pltpu.VMEM((N_KV * GQA_R * Q_BLOCK, 1), F32),
                        pltpu.VMEM((N_KV * GQA_R * Q_BLOCK, D_KV), F32),
                        pltpu.VMEM((N_KV * GQA_R * Q_BLOCK, D_KV), F32),
                        pltpu.VMEM((N_KV * Q_BLOCK, t // SEL_BLOCK), F32)],
        compiler_params=pltpu.CompilerParams(dimension_semantics=("arbitrary",),
                                             vmem_limit_bytes=VMEM_LIMIT),
        name="attn_prompt",
    )(q, gates, kst, vs, kwt, vw, kct, vc, msel, cst, csth, nrm)


FF_CHUNK = D_FF // 2


def _layer_norm(x, g, b):
    mu = jnp.mean(x, axis=-1, keepdims=True)
    xc = x - mu
    var = jnp.mean(xc * xc, axis=-1, keepdims=True)
    return (xc * lax.rsqrt(var + 1e-5)) * g + b


def _ffn_kernel(x_ref, hr_ref, ha_ref, g1_ref, sh2_ref, sc2_ref, g2_ref, wo_ref, l1g_ref, l1b_ref,
                wup_ref, wdn_ref, l2g_ref, l2b_ref, o_ref, *, alpha):
    d = functools.partial(jnp.dot, preferred_element_type=F32)
    mix = d(hr_ref[...], wo_ref[0:D_RNN, :]) + d(ha_ref[...], wo_ref[D_RNN:D_MODEL, :])
    x1 = _layer_norm(alpha * x_ref[...] + g1_ref[...] * mix, l1g_ref[...], l1b_ref[...])
    u = (x1 * (1.0 + sc2_ref[...]) + sh2_ref[...]).astype(BF16)
    f = None
    for c in range(0, D_FF, FF_CHUNK):
        gate = d(u, wup_ref[:, c:c + FF_CHUNK])
        up = d(u, wup_ref[:, D_FF + c:D_FF + c + FF_CHUNK])
        part = d((gate * _sigmoid(gate) * up).astype(BF16), wdn_ref[c:c + FF_CHUNK, :])
        f = part if f is None else f + part
    o_ref[...] = _layer_norm(alpha * x1 + g2_ref[...] * f, l2g_ref[...], l2b_ref[...])


def _merge_ffn(x2d, hr, ha, g1, sh2, sc2, g2, wo, l1g, l1b, wup, wdn, l2g, l2b, tm, alpha):
    r = x2d.shape[0]
    rm = g1.shape[0]
    mod_spec = (pl.BlockSpec((1, D_MODEL), lambda i: (0, 0)) if rm == 1
                else pl.BlockSpec((tm, D_MODEL), lambda i: (i, 0)))
    vec = pl.BlockSpec((1, D_MODEL), lambda i: (0, 0))
    full = lambda a: pl.BlockSpec(a.shape, lambda i: (0, 0), pipeline_mode=pl.Buffered(1))
    return pl.pallas_call(
        functools.partial(_ffn_kernel, alpha=alpha),
        grid=(r // tm,),
        in_specs=[pl.BlockSpec((tm, D_MODEL), lambda i: (i, 0)),
                  pl.BlockSpec((tm, D_RNN), lambda i: (i, 0)),
                  pl.BlockSpec((tm, D_Q), lambda i: (i, 0)),
                  mod_spec, mod_spec, mod_spec, mod_spec,
                  full(wo), vec, vec, full(wup), full(wdn), vec, vec],
        out_specs=pl.BlockSpec((tm, D_MODEL), lambda i: (i, 0)),
        out_shape=jax.ShapeDtypeStruct((r, D_MODEL), F32),
        compiler_params=pltpu.CompilerParams(dimension_semantics=("arbitrary",),
                                             vmem_limit_bytes=VMEM_LIMIT),
        name="merge_ffn",
    )(x2d, hr, ha, g1, sh2, sc2, g2, wo, l1g, l1b, wup, wdn, l2g, l2b)


CACHE_ROWS_PER_PAGE = 4 * N_KV * HEAD_DIM
CMP_ROWS_PER_PAGE = 2 * N_KV * HEAD_DIM


def _sample_cmp_kernel(pt_ref, cache_ref, q_ref, new_ref, pek_ref, pev_ref, wk_ref, wv_ref,
                       w2k_ref, w2v_ref, msel_ref, oc_ref, pslc_ref, buf, x_s, kc_s, vc_s, sem,
                       *, n_pages):
    b = pl.program_id(0)
    nb = pl.num_programs(0)
    past_len = n_pages * PAGE_SIZE
    n_cmp = past_len // CMP_STRIDE

    def page_copy(bb, p, slot):
        pool = pt_ref[bb * n_pages + p]
        return pltpu.make_async_copy(
            cache_ref.at[pl.ds(pool * CACHE_ROWS_PER_PAGE, CMP_ROWS_PER_PAGE), :],
            buf.at[slot, p], sem.at[slot])

    def start_all(bb, slot):
        for p in range(n_pages):
            page_copy(bb, p, slot).start()

    @pl.when(b == 0)
    def _():
        start_all(0, 0)

    slot = b % 2

    @pl.when(b + 1 < nb)
    def _():
        start_all(b + 1, 1 - slot)

    for p in range(n_pages):
        page_copy(b, p, slot).wait()

    groups_per_page = PAGE_SIZE // CMP_STRIDE

    def to_rows(p):
        r0 = p * (groups_per_page * CMP_PITCH)
        for t in range(2):
            tile = buf[slot, p, t * D_KV:(t + 1) * D_KV, :].T
            for j in range(groups_per_page):
                x_s[t, r0 + CMP_PITCH * j:r0 + CMP_PITCH * j + CMP_STRIDE, :] = (
                    tile[CMP_STRIDE * j:CMP_STRIDE * (j + 1)])

    def compress(blk0, nblk):
        kc, vc = _compress_chunk(x_s, blk0, nblk, pek_ref, pev_ref, wk_ref, wv_ref, w2k_ref,
                                 w2v_ref)
        kc_s[blk0:blk0 + nblk, :] = kc
        vc_s[blk0:blk0 + nblk, :] = vc

    part_pages = n_pages // CMP_PARTS
    done = 0
    for k in range(CMP_PARTS):
        for p in range(k * part_pages, (k + 1) * part_pages):
            to_rows(p)
        if k == CMP_PARTS - 1:
            tail_row = lax.broadcasted_iota(jnp.int32, (CMP_STRIDE, D_KV), 0)
            for t in range(2):
                x_s[t, CMP_PITCH * n_cmp:CMP_PITCH * n_cmp + CMP_STRIDE, :] = jnp.where(
                    tail_row == 0, new_ref[0][:, t * D_KV:(t + 1) * D_KV], 0.0)
            upto = n_cmp
        else:
            upto = (k + 1) * part_pages * groups_per_page - 16
        compress(done, upto - done)
        done = upto

    q = q_ref[0]
    kcb = kc_s[...].astype(BF16)
    vcb = vc_s[...].astype(BF16)
    row = lax.broadcasted_iota(jnp.int32, (N_HEADS, 1), 0)
    first = row < GQA_R
    nt = (((1,), (1,)), ((), ()))
    s0 = lax.dot_general(q, kcb[:, 0:HEAD_DIM], nt, preferred_element_type=F32)
    s1 = lax.dot_general(q, kcb[:, HEAD_DIM:D_KV], nt, preferred_element_type=F32)
    cend = lax.broadcasted_iota(jnp.int32, (1, n_cmp), 1) * CMP_STRIDE + (CMP_LEN - 1)
    s = jnp.where(first, s0, s1) - _pow2_neg(row + 1) * (past_len - cend).astype(F32)
    p = _masked_softmax(s, cend <= past_len)
    o = jnp.dot(p.astype(BF16), vcb, preferred_element_type=F32)
    oc_ref[0] = jnp.where(first, o[:, 0:HEAD_DIM], o[:, HEAD_DIM:D_KV])
    psum = jnp.concatenate([jnp.sum(p[0:GQA_R], axis=0, keepdims=True),
                            jnp.sum(p[GQA_R:N_HEADS], axis=0, keepdims=True)], axis=0)
    pslc_ref[0] = _split3_dot(psum, msel_ref[...])


def _sample_cmp(pt_flat, cache2d, q3, new3, pek, pev, wk, wv, w2k, w2v, msel, n_pages):
    b = q3.shape[0]
    past_len = n_pages * PAGE_SIZE
    n_cmp = past_len // CMP_STRIDE
    full = lambda a: pl.BlockSpec(a.shape, lambda i, pt: (0,) * a.ndim)
    return pl.pallas_call(
        functools.partial(_sample_cmp_kernel, n_pages=n_pages),
        grid_spec=pltpu.PrefetchScalarGridSpec(
            num_scalar_prefetch=1, grid=(b,),
            in_specs=[pl.BlockSpec(memory_space=pl.ANY),
                      pl.BlockSpec((1, N_HEADS, HEAD_DIM), lambda i, pt: (i, 0, 0)),
                      pl.BlockSpec((1, 1, 6 * D_KV), lambda i, pt: (i, 0, 0)),
                      full(pek), full(pev), full(wk), full(wv), full(w2k), full(w2v), full(msel)],
            out_specs=[pl.BlockSpec((1, N_HEADS, HEAD_DIM), lambda i, pt: (i, 0, 0)),
                       pl.BlockSpec((1, N_KV, msel.shape[1]), lambda i, pt: (i, 0, 0))],
            scratch_shapes=[pltpu.VMEM((2, n_pages, CMP_ROWS_PER_PAGE, PAGE_SIZE), F32),
                            pltpu.VMEM((2, _pitch_rows(n_cmp + 1), D_KV), F32),
                            pltpu.VMEM((n_cmp, D_KV), F32), pltpu.VMEM((n_cmp, D_KV), F32),
                            pltpu.SemaphoreType.DMA((2,))]),
        out_shape=[jax.ShapeDtypeStruct((b, N_HEADS, HEAD_DIM), F32),
                   jax.ShapeDtypeStruct((b, N_KV, msel.shape[1]), F32)],
        compiler_params=pltpu.CompilerParams(dimension_semantics=("arbitrary",),
                                             vmem_limit_bytes=VMEM_LIMIT),
        name="sample_cmp",
    )(pt_flat, cache2d, q3, new3, pek, pev, wk, wv, w2k, w2v, msel)


META_W = 128


def _sample_topk_kernel(pslc_ref, meta_ref, *, past_len):
    n_lane = pslc_ref.shape[1]
    n_sel = -(-(past_len + 1) // SEL_BLOCK)
    blk = lax.broadcasted_iota(jnp.int32, (1, n_lane), 1)
    cur = past_len // SEL_BLOCK
    forced = (blk == 0) | (blk == cur) | (blk == cur - 1)
    score = jnp.where(blk * SEL_BLOCK <= past_len,
                      jnp.where(forced, FORCE_SCORE, pslc_ref[...]), -1.0)
    score = jnp.where(blk < n_sel, score, -3.0)
    idxs, vals = _topk_select(score, blk.astype(F32), min(TOP_N, n_sel))
    lane = lax.broadcasted_iota(jnp.int32, (pslc_ref.shape[0], META_W), 1)
    meta = jnp.zeros((pslc_ref.shape[0], META_W), F32)
    for it, (idx, val) in enumerate(zip(idxs, vals)):
        meta = jnp.where(lane == it, idx, meta)
        meta = jnp.where(lane == TOP_N + it, jnp.where(val >= 0.0, 1.0, 0.0), meta)
    meta_ref[...] = meta


def _sample_topk(pslc2d, past_len):
    return pl.pallas_call(
        functools.partial(_sample_topk_kernel, past_len=past_len),
        out_shape=jax.ShapeDtypeStruct((pslc2d.shape[0], META_W), F32),
        name="sample_topk",
    )(pslc2d)


SEL_KEYS = TOP_N * PAGE_SIZE


def _sample_attn_kernel(koff_ref, voff_ref, cache_ref, q_ref, meta_ref, new_ref, newt_ref, win_ref,
                        oc_ref, gt_ref, nrm_ref, e16_ref, ha_ref, wout_ref, kbuf, vbuf, sem,
                        *, past_len):
    b = pl.program_id(0)
    nb = pl.num_programs(0)

    def tile_copies(bb, slot):
        cps = []
        for g in range(N_KV):
            for n in range(TOP_N):
                i = (bb * N_KV + g) * TOP_N + n
                dst = pl.ds(n * PAGE_SIZE, PAGE_SIZE)
                cps.append(pltpu.make_async_copy(cache_ref.at[pl.ds(koff_ref[i], HEAD_DIM), :],
                                                 kbuf.at[slot, g, :, dst], sem.at[slot]))
                cps.append(pltpu.make_async_copy(cache_ref.at[pl.ds(voff_ref[i], HEAD_DIM), :],
                                                 vbuf.at[slot, g, :, dst], sem.at[slot]))
        return cps

    @pl.when(b == 0)
    def _():
        for cp in tile_copies(0, 0):
            cp.start()

    slot = b % 2

    @pl.when(b + 1 < nb)
    def _():
        for cp in tile_copies(b + 1, 1 - slot):
            cp.start()

    for cp in tile_copies(b, slot):
        cp.wait()

    q = q_ref[0]
    qf = q.astype(F32)
    row = lax.broadcasted_iota(jnp.int32, (N_HEADS, 1), 0)
    first = row < GQA_R
    slope = _pow2_neg(row + 1)
    new = new_ref[0]
    nt = (((1,), (1,)), ((), ()))

    def new_rows(off):
        a = new[:, off:off + HEAD_DIM]
        c = new[:, off + HEAD_DIM:off + D_KV]
        v = jnp.where(first, jnp.broadcast_to(a, (N_HEADS, HEAD_DIM)),
                      jnp.broadcast_to(c, (N_HEADS, HEAD_DIM)))
        return v.astype(BF16).astype(F32)

    def attend(s_buf, mask_buf, v_of_p, s_new, new_on, v_new):
        s_buf = jnp.where(mask_buf, s_buf, NEG)
        s_new = jnp.where(new_on, s_new, NEG)
        m = jnp.maximum(jnp.max(s_buf, axis=-1, keepdims=True), s_new)
        p = jnp.where(mask_buf, jnp.exp(s_buf - m), 0.0)
        p_new = jnp.where(new_on, jnp.exp(s_new - m), 0.0)
        den = jnp.maximum(jnp.sum(p, axis=-1, keepdims=True) + p_new, 1e-30)
        p = p / den
        p_new = (p_new / den).astype(BF16).astype(F32)
        return v_of_p(p.astype(BF16)) + p_new * v_new

    meta = meta_ref[0]
    e16 = e16_ref[...]
    jexp = jnp.dot(meta[:, 0:TOP_N].astype(BF16), e16, preferred_element_type=F32)
    vexp = jnp.dot(meta[:, TOP_N:2 * TOP_N].astype(BF16), e16, preferred_element_type=F32)
    lane = lax.broadcasted_iota(jnp.int32, (1, SEL_KEYS), 1) & (PAGE_SIZE - 1)
    ji = jexp.astype(jnp.int32)
    kpos = (ji >> 1) * PAGE_SIZE + lane
    key_ok = (vexp > 0.5) & ((kpos >> 6) == ji) & (kpos < past_len)
    new_blk = past_len // SEL_BLOCK
    new_sel = jnp.max(jnp.where((meta[:, 0:TOP_N] == float(new_blk)) & (meta[:, TOP_N:2 * TOP_N] > 0.5),
                                1.0, 0.0), axis=-1, keepdims=True)
    o_sel = []
    for g in range(N_KV):
        sb = jnp.dot(q, kbuf[slot, g].astype(BF16), preferred_element_type=F32)
        sb = sb - slope * (past_len - kpos[g:g + 1]).astype(F32)
        vb = vbuf[slot, g].astype(BF16)
        o_sel.append((sb, key_ok[g:g + 1], vb))
    k_new = new_rows(2 * D_KV)
    v_new = new_rows(3 * D_KV)
    s_new = jnp.sum(qf * k_new, axis=-1, keepdims=True)
    new_on = jnp.where(first, new_sel[0:1], new_sel[1:2]) > 0.5
    outs = [attend(sb, ok, lambda pb, vb=vb: lax.dot_general(pb, vb, nt, preferred_element_type=F32),
                   s_new, new_on, v_new) for sb, ok, vb in o_sel]
    o_s = jnp.where(first, outs[0], outs[1])

    w = win_ref[0]
    wbuf = w.shape[1]
    lane_b = lax.broadcasted_iota(jnp.int32, (4 * HEAD_DIM, newt_ref.shape[1]), 1)
    new_col = jnp.sum(jnp.where(lane_b == b, newt_ref[...], 0.0), axis=-1, keepdims=True)
    lane_w = lax.broadcasted_iota(jnp.int32, (1, wbuf), 1)
    w_new = jnp.where(lane_w == wbuf - 1, new_col, pltpu.roll(w, wbuf - 1, 1))
    wout_ref[0] = w_new
    wdist = (wbuf - 1 - lane_w).astype(F32)
    wb = w_new.astype(BF16)
    ow = []
    for g in range(N_KV):
        sw = jnp.dot(q, wb[g * HEAD_DIM:(g + 1) * HEAD_DIM], preferred_element_type=F32)
        pw = _masked_softmax(sw - slope * wdist, lane_w >= 0)
        ow.append(lax.dot_general(pw.astype(BF16), wb[D_KV + g * HEAD_DIM:D_KV + (g + 1) * HEAD_DIM],
                                  nt, preferred_element_type=F32))
    o_w = jnp.where(first, ow[0], ow[1])

    gates = _sigmoid(gt_ref[0])
    y = oc_ref[0] * gates[:, 0:1] + o_s * gates[:, 1:2] + o_w * gates[:, 2:3]
    ms = jnp.sum(jnp.sum(y * y, axis=-1, keepdims=True), axis=0, keepdims=True) / D_Q
    ha_ref[0] = (y * lax.rsqrt(ms + 1e-6)) * nrm_ref[...]


def _sample_attn(koff, voff, cache2d, q3, meta3, new3, newt, win3, oc3, gt3, nrm8, e16, past_len):
    b = q3.shape[0]
    wbuf = win3.shape[2]
    full = lambda a: pl.BlockSpec(a.shape, lambda i, ko, vo: (0,) * a.ndim)
    per_b = lambda a: pl.BlockSpec((1,) + a.shape[1:], lambda i, ko, vo: (i,) + (0,) * (a.ndim - 1))
    return pl.pallas_call(
        functools.partial(_sample_attn_kernel, past_len=past_len),
        grid_spec=pltpu.PrefetchScalarGridSpec(
            num_scalar_prefetch=2, grid=(b,),
            in_specs=[pl.BlockSpec(memory_space=pl.ANY), per_b(q3), per_b(meta3), per_b(new3),
                      full(newt), per_b(win3), per_b(oc3), per_b(gt3), full(nrm8), full(e16)],
            out_specs=[pl.BlockSpec((1, N_HEADS, HEAD_DIM), lambda i, ko, vo: (i, 0, 0)),
                       pl.BlockSpec((1, 4 * HEAD_DIM, wbuf), lambda i, ko, vo: (i, 0, 0))],
            scratch_shapes=[pltpu.VMEM((2, N_KV, HEAD_DIM, SEL_KEYS), F32),
                            pltpu.VMEM((2, N_KV, HEAD_DIM, SEL_KEYS), F32),
                            pltpu.SemaphoreType.DMA((2,))]),
        out_shape=[jax.ShapeDtypeStruct((b, N_HEADS, HEAD_DIM), F32),
                   jax.ShapeDtypeStruct((b, 4 * HEAD_DIM, wbuf), F32)],
        compiler_params=pltpu.CompilerParams(dimension_semantics=("arbitrary",),
                                             vmem_limit_bytes=VMEM_LIMIT),
        name="sample_attn",
    )(koff, voff, cache2d, q3, meta3, new3, newt, win3, oc3, gt3, nrm8, e16)


def _block_diag(w):
    n, a, b = w.shape
    eye = jnp.eye(n, dtype=w.dtype)
    return (eye[:, None, :, None] * w[:, :, None, :]).reshape(n * a, n * b)


def _cmp_weights(pe, w1, w2):
    pe_cat = jnp.tile(pe, (1, N_KV)).reshape(1, CMP_LEN * D_KV)
    eye = jnp.eye(N_KV, dtype=w1.dtype)
    w1_big = (w1[:, None, :, None, :] * eye[None, :, None, :, None]).reshape(
        CMP_LEN * D_KV, N_KV * CMP_HID)
    w2_big = (w2[None, :, None, :] * eye[:, None, :, None]).reshape(N_KV * CMP_HID, D_KV)
    return pe_cat, w1_big.astype(BF16), w2_big.astype(BF16)


def _sel_matrix(n_cmp, n_lane):
    n = np.arange(n_cmp)[:, None]
    j = np.arange(n_lane)[None, :]
    lo = SEL_RATIO * j - (CMP_LEN // CMP_STRIDE - 1)
    return jnp.asarray(((n >= lo) & (n <= lo + SEL_RATIO)).astype(np.float32), dtype=BF16)


def kernel(x_prompt, x_sample, cache_kv, state_win, state_conv, state_h, page_table, c_prompt, c_sample, w_ada, b_ada, w_in, conv_w, conv_b, rg_wa, rg_ba, rg_wx, rg_bx, rg_lam, cmp_pe_k, cmp_w1_k, cmp_w2_k, cmp_pe_v, cmp_w1_v, cmp_w2_v, norm_rg, norm_attn, w_out, ln1_g, ln1_b, w_up, w_down, ln2_g, ln2_b):
    depth = w_in.shape[0]
    alpha = float((2.0 * depth) ** 0.25)
    bp, t, _ = x_prompt.shape
    assert bp == 1
    bs = x_sample.shape[0]
    assert x_sample.shape[1] == 1
    n_pages = page_table.shape[1]
    past_len = n_pages * PAGE_SIZE
    n_pool = cache_kv.shape[1]
    wbuf = state_win.shape[2]
    assert wbuf == WINDOW and t % CMP_ROWS == 0 and t % SEL_CHUNK == 0 and t >= WIN_KEYS
    assert n_pages % CMP_PARTS == 0 and (n_pages // CMP_PARTS) * PAGE_SIZE >= 2 * 16 * CMP_STRIDE

    xp = x_prompt.reshape(t, D_MODEL)
    xs = x_sample.reshape(bs, D_MODEL)
    pt_flat = page_table.reshape(-1)
    r_mod = -(-(bs + 1) // 8) * 8
    c_all = jnp.zeros((r_mod, D_MODEL), F32).at[0:bs].set(c_sample).at[bs:bs + 1].set(c_prompt)
    vec = lambda a: a.reshape(1, -1)

    outs = [[] for _ in range(8)]
    for l in range(depth):
        mod = _modulation(c_all, w_ada[l], b_ada[l])
        mod_s = [mod[0:bs, k * D_MODEL:(k + 1) * D_MODEL] for k in range(6)]
        mod_p = [mod[bs:bs + 1, k * D_MODEL:(k + 1) * D_MODEL] for k in range(6)]
        w_l = w_in[l]
        w_q = jnp.pad(w_l[:, 2 * D_RNN:2 * D_RNN + D_Q].reshape(D_MODEL, N_HEADS, HEAD_DIM),
                      ((0, 0), (0, 0), (0, Q_SLOT - HEAD_DIM))).reshape(D_MODEL, Q_PAD)
        w_in_b = jnp.concatenate(
            [w_l[:, 0:2 * D_RNN], w_q, w_l[:, 2 * D_RNN + D_Q:],
             jnp.zeros((D_MODEL, GATE_PAD - D_GATE), F32)], axis=1).astype(BF16)
        wrg = jnp.concatenate([_block_diag(rg_wa[l]), _block_diag(rg_wx[l])], axis=1).astype(BF16)
        pek, wk, w2k = _cmp_weights(cmp_pe_k[l], cmp_w1_k[l], cmp_w2_k[l])
        pev, wv, w2v = _cmp_weights(cmp_pe_v[l], cmp_w1_v[l], cmp_w2_v[l])
        wo_b, wup_b, wdn_b = w_out[l].astype(BF16), w_up[l].astype(BF16), w_down[l].astype(BF16)
        rg_args = (conv_w[l], vec(conv_b[l]), wrg, vec(rg_ba[l]), vec(rg_bx[l]), vec(rg_lam[l]),
                   vec(norm_rg[l]))
        ffn_w = (wo_b, vec(ln1_g[l]), vec(ln1_b[l]), wup_b, wdn_b, vec(ln2_g[l]), vec(ln2_b[l]))

        xg, q, rows, gts, vs, vw, kvt, kvwt, kst, kwt = _in_proj(xp, mod_p[0], mod_p[1], w_in_b, 512,
                                                                 2 * D_KV)
        hr, tail, hl = _rg_prompt(xg, *rg_args, tc=256)
        kct, vc = _cmp_prompt(rows, pek, pev, wk, wv, w2k, w2v)
        msel_p = _sel_matrix(t // CMP_STRIDE, t // SEL_BLOCK)
        ha = _attn_prompt(q, gts, kst, vs, kwt, vw, kct, vc, msel_p, vec(norm_attn[l]))
        xp = _merge_ffn(xp, hr, ha, mod_p[2], mod_p[3], mod_p[4], mod_p[5], *ffn_w,
                        tm=512, alpha=alpha)
        outs[0].append(kvt.reshape(4, N_KV, HEAD_DIM, t).transpose(3, 0, 1, 2)[None])
        outs[2].append(kvwt[:, t - WINDOW:].reshape(2, N_KV, HEAD_DIM, WINDOW)
                       .transpose(3, 0, 1, 2)[None])
        outs[4].append(tail[8 - (CONV_W - 1):][None])
        outs[6].append(hl[0:1])

        xg, q, rows, gts, _, _, kvt, kvwt, _, _ = _in_proj(xs, mod_s[0], mod_s[1], w_in_b, bs,
                                                           6 * D_KV)
        sconv = state_conv[l]
        hr, h_new = _rg_sample(xg, sconv[:, 0], sconv[:, 1], sconv[:, 2], state_h[l], *rg_args)
        cache2d = cache_kv[l].transpose(0, 2, 3, 4, 1).reshape(n_pool * CACHE_ROWS_PER_PAGE, PAGE_SIZE)
        n_lane = -(-(past_len // SEL_BLOCK + 1) // LANE) * LANE
        msel_s = _sel_matrix(past_len // CMP_STRIDE, n_lane)
        q3 = q.reshape(bs, N_HEADS, Q_SLOT)[:, :, 0:HEAD_DIM]
        new3 = rows.reshape(bs, 1, 6 * D_KV)
        oc3, pslc = _sample_cmp(pt_flat, cache2d, q3, new3, pek, pev, wk, wv, w2k, w2v, msel_s, n_pages)
        meta = _sample_topk(pslc.reshape(bs * N_KV, n_lane), past_len)
        picks = meta[:, 0:TOP_N].astype(jnp.int32).reshape(bs, N_KV * TOP_N)
        pool = jnp.take_along_axis(page_table, jnp.minimum(picks // 2, n_pages - 1), axis=1)
        grp = (jnp.arange(N_KV * TOP_N, dtype=jnp.int32) // TOP_N)[None, :]
        koff = (pool * CACHE_ROWS_PER_PAGE + (2 * N_KV + grp) * HEAD_DIM).reshape(-1)
        voff = koff + N_KV * HEAD_DIM
        win3 = state_win[l].transpose(0, 2, 3, 4, 1).reshape(bs, 4 * HEAD_DIM, wbuf)
        e16 = jnp.asarray(np.kron(np.eye(TOP_N, dtype=np.float32), np.ones((1, PAGE_SIZE), np.float32)),
                          dtype=BF16)
        ha3, wnew = _sample_attn(koff, voff, cache2d, q3, meta.reshape(bs, N_KV, META_W), new3,
                                 kvwt, win3, oc3, gts[:, 0:D_GATE].reshape(bs, N_HEADS, 3),
                                 norm_attn[l].reshape(N_HEADS, HEAD_DIM), e16, past_len)
        ha = ha3.reshape(bs, D_Q).astype(BF16)
        xs = _merge_ffn(xs, hr, ha, mod_s[2], mod_s[3], mod_s[4], mod_s[5], *ffn_w,
                        tm=bs, alpha=alpha)
        outs[1].append(kvt.reshape(4, N_KV, HEAD_DIM, bs).transpose(3, 0, 1, 2)[:, None])
        outs[3].append(wnew.reshape(bs, 2, N_KV, HEAD_DIM, wbuf).transpose(0, 4, 1, 2, 3))
        outs[5].append(jnp.stack([sconv[:, 1], sconv[:, 2], xg[:, 0:D_RNN]], axis=1))
        outs[7].append(h_new)

    stk = [jnp.stack(o) for o in outs]
    return (xp.reshape(1, t, D_MODEL), xs.reshape(bs, 1, D_MODEL), stk[0], stk[1], stk[2], stk[3],
            stk[4], stk[5], stk[6], stk[7])
```

```python
import functools

import numpy as np
import jax
import jax.numpy as jnp
from jax import lax
from jax.experimental import pallas as pl
from jax.experimental.pallas import tpu as pltpu

F32 = jnp.float32
BF16 = jnp.bfloat16

D_MODEL = 1024
D_RNN = D_MODEL // 2
RNN_BLOCKS = 8
RNN_BLOCK = D_RNN // RNN_BLOCKS
CONV_W = 4
RG_C = 8.0
HEAD_DIM = 64
N_HEADS = (D_MODEL - D_RNN) // HEAD_DIM
N_KV = 2
GQA_R = N_HEADS // N_KV
D_Q = N_HEADS * HEAD_DIM
D_KV = N_KV * HEAD_DIM
D_GATE = 3 * N_HEADS
CMP_STRIDE = 16
CMP_LEN = 2 * CMP_STRIDE
CMP_HID = 128
SEL_BLOCK = 64
SEL_RATIO = SEL_BLOCK // CMP_STRIDE
TOP_N = 16
WINDOW = 512
Q_BLOCK = 128
FORCE_SCORE = 1.0e4
PAGE_SIZE = 128
D_FF = ((8 * D_MODEL // 3 + 255) // 256) * 256
SCALE = HEAD_DIM ** -0.5

NEG = -1e30
LANE = 128

Q_SLOT = LANE
Q_PAD = N_HEADS * Q_SLOT
OFF_Q = 2 * D_RNN
OFF_KV = OFF_Q + Q_PAD
OFF_GATE = OFF_KV + 6 * D_KV
GATE_PAD = 128
D_IN_PAD = OFF_GATE + GATE_PAD

SEL_CHUNK = 1024
BLK_PER_CHUNK = SEL_CHUNK // SEL_BLOCK
POS_SPLIT = 16
AUG_HI = HEAD_DIM
AUG_LO = HEAD_DIM + 1
AUG_PEN = HEAD_DIM + 2
VMEM_LIMIT = 56 * 1024 * 1024


def _sigmoid(x):
    return 1.0 / (1.0 + jnp.exp(-x))


def _gelu_tanh(x):
    c = np.float32(np.sqrt(2.0 / np.pi))
    return 0.5 * x * (1.0 + jnp.tanh(c * (x + np.float32(0.044715) * (x * x * x))))


def _softplus(x):
    return jnp.maximum(x, 0.0) + jnp.log1p(jnp.exp(-jnp.abs(x)))


def _pow2_neg(e_int):
    return lax.bitcast_convert_type((127 - e_int) << 23, F32)


def _masked_softmax(s, mask):
    s = jnp.where(mask, s, NEG)
    m = jnp.max(s, axis=-1, keepdims=True)
    p = jnp.exp(s - m)
    den = jnp.maximum(jnp.sum(p, axis=-1, keepdims=True), 1e-30)
    return p * jnp.where(m > 0.5 * NEG, 1.0 / den, 0.0)


def _split3_dot(x, m_bf16):
    hi = x.astype(BF16)
    r1 = x - hi.astype(F32)
    mid = r1.astype(BF16)
    lo = (r1 - mid.astype(F32)).astype(BF16)
    d = functools.partial(jnp.dot, preferred_element_type=F32)
    return d(hi, m_bf16) + d(mid, m_bf16) + d(lo, m_bf16)


def _topk_select(score, blkf, k):
    work = score
    idxs, vals = [], []
    for _ in range(k):
        m = jnp.max(work, axis=1, keepdims=True)
        idx = jnp.min(jnp.where(work == m, blkf, 1e9), axis=1, keepdims=True)
        work = jnp.where(blkf == idx, -2.0, work)
        idxs.append(idx)
        vals.append(m)
    return idxs, vals


def _topk_member(score, blkf, k):
    work = score
    for _ in range(k):
        m = jnp.max(work, axis=1, keepdims=True)
        idx = jnp.min(jnp.where(work == m, blkf, 1e9), axis=1, keepdims=True)
        work = jnp.where(blkf == idx, -2.0, work)
    return jnp.where((work < -1.5) & (score >= 0.0), 1.0, 0.0)


def _mod_kernel(c_ref, w_ref, b_ref, o_ref):
    c = c_ref[...]
    a = (c * _sigmoid(c)).astype(BF16)
    o_ref[...] = jnp.dot(a, w_ref[...].astype(BF16), preferred_element_type=F32) + b_ref[...]


def _modulation(c_all, w_ada, b_ada):
    r, n = c_all.shape[0], w_ada.shape[1]
    tn = 512
    return pl.pallas_call(
        _mod_kernel,
        grid=(n // tn,),
        in_specs=[pl.BlockSpec((r, D_MODEL), lambda j: (0, 0)),
                  pl.BlockSpec((D_MODEL, tn), lambda j: (0, j)),
                  pl.BlockSpec((1, tn), lambda j: (0, j))],
        out_specs=pl.BlockSpec((r, tn), lambda j: (0, j)),
        out_shape=jax.ShapeDtypeStruct((r, n), F32),
        name="adaln_mod",
    )(c_all, w_ada, b_ada.reshape(1, n))


def _inproj_kernel(x_ref, sh_ref, sc_ref, w_ref, qc_ref, xg_ref, q_ref, rows_ref, gt_ref, vs_ref,
                   vw_ref, kvt_ref, kvwt_ref, kst_ref, kwt_ref, *, n_row_cols):
    u = x_ref[...] * (1.0 + sc_ref[...]) + sh_ref[...]
    p = jnp.dot(u.astype(BF16), w_ref[...], preferred_element_type=F32)
    xg_ref[...] = p[:, 0:OFF_Q]
    q_ref[...] = (p[:, OFF_Q:OFF_KV] * SCALE + qc_ref[...]).astype(BF16)
    kv = p[:, OFF_KV:OFF_GATE]
    rows_ref[...] = kv[:, 0:n_row_cols]
    gt_ref[...] = p[:, OFF_GATE:D_IN_PAD]
    vs_ref[...] = kv[:, 3 * D_KV:4 * D_KV].astype(BF16)
    vw_ref[...] = kv[:, 5 * D_KV:6 * D_KV].astype(BF16)
    kvt = kv.T
    kvt_ref[...] = kvt[0:4 * D_KV]
    kvwt_ref[...] = kvt[4 * D_KV:6 * D_KV]
    kst_ref[...] = kvt[2 * D_KV:3 * D_KV].astype(BF16)
    kwt_ref[...] = kvt[4 * D_KV:5 * D_KV].astype(BF16)


def _q_consts():
    qc = np.zeros((1, Q_PAD), np.float32)
    for h in range(N_HEADS):
        slope = 2.0 ** (-8.0 * (h + 1) / N_HEADS)
        qc[0, h * Q_SLOT + AUG_HI] = POS_SPLIT * slope
        qc[0, h * Q_SLOT + AUG_LO] = slope
    return jnp.asarray(qc)


def _in_proj(x2d, shift, scale, w_bf16, tm, n_row_cols):
    r = x2d.shape[0]
    rm = shift.shape[0]
    mod_spec = (pl.BlockSpec((1, D_MODEL), lambda i: (0, 0)) if rm == 1
                else pl.BlockSpec((tm, D_MODEL), lambda i: (i, 0)))
    row = lambda w: pl.BlockSpec((tm, w), lambda i: (i, 0))
    col = lambda h: pl.BlockSpec((h, tm), lambda i: (0, i))
    return pl.pallas_call(
        functools.partial(_inproj_kernel, n_row_cols=n_row_cols),
        grid=(r // tm,),
        in_specs=[row(D_MODEL), mod_spec, mod_spec,
                  pl.BlockSpec((D_MODEL, D_IN_PAD), lambda i: (0, 0)),
                  pl.BlockSpec((1, Q_PAD), lambda i: (0, 0))],
        out_specs=[row(OFF_Q), row(Q_PAD), row(n_row_cols), row(GATE_PAD), row(D_KV), row(D_KV),
                   col(4 * D_KV), col(2 * D_KV), col(D_KV), col(D_KV)],
        out_shape=[jax.ShapeDtypeStruct((r, OFF_Q), F32),
                   jax.ShapeDtypeStruct((r, Q_PAD), BF16),
                   jax.ShapeDtypeStruct((r, n_row_cols), F32),
                   jax.ShapeDtypeStruct((r, GATE_PAD), F32),
                   jax.ShapeDtypeStruct((r, D_KV), BF16),
                   jax.ShapeDtypeStruct((r, D_KV), BF16),
                   jax.ShapeDtypeStruct((4 * D_KV, r), F32),
                   jax.ShapeDtypeStruct((2 * D_KV, r), F32),
                   jax.ShapeDtypeStruct((D_KV, r), BF16),
                   jax.ShapeDtypeStruct((D_KV, r), BF16)],
        compiler_params=pltpu.CompilerParams(dimension_semantics=("arbitrary",),
                                             vmem_limit_bytes=VMEM_LIMIT),
        name="in_proj",
    )(x2d, shift, scale, w_bf16, _q_consts())


def _rg_gates(xc, wrg_ref, ba_ref, bx_ref, lam_ref):
    g = jnp.dot(xc.astype(BF16), wrg_ref[...], preferred_element_type=F32)
    r = _sigmoid(g[:, 0:D_RNN] + ba_ref[...])
    ig = _sigmoid(g[:, D_RNN:2 * D_RNN] + bx_ref[...])
    log_a = -RG_C * r * _softplus(-lam_ref[...])
    a = jnp.exp(log_a)
    b = jnp.sqrt(-jnp.tanh(log_a) * (a * a + 1.0)) * (ig * xc)
    return a, b


def _rg_out(h, gr, nrm_ref):
    y = h * _gelu_tanh(gr)
    return (y * lax.rsqrt(jnp.mean(y * y, axis=-1, keepdims=True) + 1e-6)) * nrm_ref[...]


def _rg_prompt_kernel(xg_ref, cw_ref, cb_ref, wrg_ref, ba_ref, bx_ref, lam_ref, nrm_ref,
                      y_ref, tail_ref, hl_ref, xp_s, a_s, b_s, h_s, hc_s, *, tc):
    i = pl.program_id(0)

    @pl.when(i == 0)
    def _():
        xp_s[0:8, :] = jnp.zeros((8, D_RNN), F32)
        hc_s[...] = jnp.zeros((8, D_RNN), F32)

    xr = xg_ref[:, 0:D_RNN]
    gr = xg_ref[:, D_RNN:2 * D_RNN]
    xp_s[8:8 + tc, :] = xr
    cw = cw_ref[...]
    xc = (cb_ref[...] + cw[0:1] * xp_s[5:5 + tc, :] + cw[1:2] * xp_s[6:6 + tc, :]
          + cw[2:3] * xp_s[7:7 + tc, :] + cw[3:4] * xr)
    a, b = _rg_gates(xc, wrg_ref, ba_ref, bx_ref, lam_ref)
    a_s[...] = a
    b_s[...] = b
    rowi = lax.broadcasted_iota(jnp.int32, (8, D_RNN), 0)

    def tile(gi, hc):
        r0 = pl.multiple_of(gi * 8, 8)
        at = a_s[pl.ds(r0, 8), :]
        bt = b_s[pl.ds(r0, 8), :]
        for d in (1, 2, 4):
            keep = rowi >= d
            a_sh = jnp.where(keep, pltpu.roll(at, d, 0), 1.0)
            b_sh = jnp.where(keep, pltpu.roll(bt, d, 0), 0.0)
            bt = at * b_sh + bt
            at = at * a_sh
        h = at * hc + bt
        h_s[pl.ds(r0, 8), :] = h
        return h[7:8, :]

    hc = lax.fori_loop(0, tc // 8, tile, hc_s[0:1, :], unroll=4)
    hc_s[0:1, :] = hc
    xp_s[0:8, :] = xr[tc - 8:tc]
    y_ref[...] = _rg_out(h_s[...], gr, nrm_ref).astype(BF16)
    tail_ref[...] = xr[tc - 8:tc]
    hl_ref[...] = jnp.broadcast_to(hc, (8, D_RNN))


def _rg_prompt(xg, cw, cb, wrg, ba, bx, lam, nrm, tc):
    t = xg.shape[0]
    vec = pl.BlockSpec((1, D_RNN), lambda i: (0, 0))
    return pl.pallas_call(
        functools.partial(_rg_prompt_kernel, tc=tc),
        grid=(t // tc,),
        in_specs=[pl.BlockSpec((tc, 2 * D_RNN), lambda i: (i, 0)),
                  pl.BlockSpec((CONV_W, D_RNN), lambda i: (0, 0)), vec,
                  pl.BlockSpec((D_RNN, 2 * D_RNN), lambda i: (0, 0)), vec, vec, vec, vec],
        out_specs=[pl.BlockSpec((tc, D_RNN), lambda i: (i, 0)),
                   pl.BlockSpec((8, D_RNN), lambda i: (0, 0)),
                   pl.BlockSpec((8, D_RNN), lambda i: (0, 0))],
        out_shape=[jax.ShapeDtypeStruct((t, D_RNN), BF16),
                   jax.ShapeDtypeStruct((8, D_RNN), F32),
                   jax.ShapeDtypeStruct((8, D_RNN), F32)],
        scratch_shapes=[pltpu.VMEM((tc + 8, D_RNN), F32), pltpu.VMEM((tc, D_RNN), F32),
                        pltpu.VMEM((tc, D_RNN), F32), pltpu.VMEM((tc, D_RNN), F32),
                        pltpu.VMEM((8, D_RNN), F32)],
        compiler_params=pltpu.CompilerParams(dimension_semantics=("arbitrary",)),
        name="rg_prompt",
    )(xg, cw, cb, wrg, ba, bx, lam, nrm)


def _rg_sample_kernel(xg_ref, c0_ref, c1_ref, c2_ref, h0_ref, cw_ref, cb_ref, wrg_ref, ba_ref,
                      bx_ref, lam_ref, nrm_ref, y_ref, h_ref):
    xr = xg_ref[:, 0:D_RNN]
    gr = xg_ref[:, D_RNN:2 * D_RNN]
    cw = cw_ref[...]
    xc = (cb_ref[...] + cw[0:1] * c0_ref[...] + cw[1:2] * c1_ref[...] + cw[2:3] * c2_ref[...]
          + cw[3:4] * xr)
    a, b = _rg_gates(xc, wrg_ref, ba_ref, bx_ref, lam_ref)
    h = a * h0_ref[...] + b
    h_ref[...] = h
    y_ref[...] = _rg_out(h, gr, nrm_ref).astype(BF16)


def _rg_sample(xg, c0, c1, c2, h0, cw, cb, wrg, ba, bx, lam, nrm):
    b = xg.shape[0]
    return pl.pallas_call(
        _rg_sample_kernel,
        out_shape=[jax.ShapeDtypeStruct((b, D_RNN), BF16), jax.ShapeDtypeStruct((b, D_RNN), F32)],
        name="rg_sample",
    )(xg, c0, c1, c2, h0, cw, cb, wrg, ba, bx, lam, nrm)


CMP_PITCH = 20
CMP_KSPLIT = 8


def _pitch_rows(n_groups):
    return -(-(CMP_PITCH * n_groups) // 8) * 8


def _compress_chunk(x_s, blk0, nblk, pek_ref, pev_ref, wk_ref, wv_ref, w2k_ref, w2v_ref):
    base = blk0 * CMP_PITCH
    outs = []
    for t, (pe_ref, w1_ref, w2_ref) in enumerate(((pek_ref, wk_ref, w2k_ref),
                                                   (pev_ref, wv_ref, w2v_ref))):
        h = None
        for l0 in range(0, CMP_LEN, CMP_KSPLIT):
            pieces = []
            for l in range(l0, l0 + CMP_KSPLIT):
                row = (l // CMP_STRIDE) * CMP_PITCH + l % CMP_STRIDE
                xl = x_s[t, pl.ds(base + row, nblk, stride=CMP_PITCH), :]
                pieces.append((xl + pe_ref[:, l * D_KV:(l + 1) * D_KV]).astype(BF16))
            part = jnp.dot(jnp.concatenate(pieces, axis=1), w1_ref[l0 * D_KV:(l0 + CMP_KSPLIT) * D_KV, :],
                           preferred_element_type=F32)
            h = part if h is None else h + part
        h = h * _sigmoid(h)
        outs.append(jnp.dot(h.astype(BF16), w2_ref[...], preferred_element_type=F32))
    return outs


CMP_CHUNK = 128
CMP_ROWS = CMP_CHUNK * CMP_STRIDE
CMP_PARTS = 2


def _cmp_prompt_kernel(x_ref, nxt_ref, pek_ref, pev_ref, wk_ref, wv_ref, w2k_ref, w2v_ref,
                       kct_ref, vc_ref, x_s):
    for t in range(2):
        cols = slice(t * D_KV, (t + 1) * D_KV)
        for j in range(CMP_CHUNK):
            x_s[t, CMP_PITCH * j:CMP_PITCH * j + CMP_STRIDE, :] = (
                x_ref[CMP_STRIDE * j:CMP_STRIDE * (j + 1), cols])
        x_s[t, CMP_PITCH * CMP_CHUNK:CMP_PITCH * CMP_CHUNK + CMP_STRIDE, :] = nxt_ref[:, cols]
    kc, vc = _compress_chunk(x_s, 0, CMP_CHUNK, pek_ref, pev_ref, wk_ref, wv_ref, w2k_ref, w2v_ref)
    kct_ref[...] = kc.T.astype(BF16)
    vc_ref[...] = vc.astype(BF16)


def _cmp_prompt(rows, pek, pev, wk, wv, w2k, w2v):
    t = rows.shape[0]
    n_steps = t // CMP_ROWS
    last16 = t // CMP_STRIDE - 1
    full = lambda a: pl.BlockSpec(a.shape, lambda i: (0, 0))
    return pl.pallas_call(
        _cmp_prompt_kernel,
        grid=(n_steps,),
        in_specs=[pl.BlockSpec((CMP_ROWS, 2 * D_KV), lambda i: (i, 0)),
                  pl.BlockSpec((CMP_STRIDE, 2 * D_KV),
                               lambda i: (jnp.minimum((i + 1) * CMP_CHUNK, last16), 0)),
                  full(pek), full(pev), full(wk), full(wv), full(w2k), full(w2v)],
        out_specs=[pl.BlockSpec((D_KV, CMP_CHUNK), lambda i: (0, i)),
                   pl.BlockSpec((CMP_CHUNK, D_KV), lambda i: (i, 0))],
        out_shape=[jax.ShapeDtypeStruct((D_KV, t // CMP_STRIDE), BF16),
                   jax.ShapeDtypeStruct((t // CMP_STRIDE, D_KV), BF16)],
        scratch_shapes=[pltpu.VMEM((2, _pitch_rows(CMP_CHUNK + 1), D_KV), F32)],
        compiler_params=pltpu.CompilerParams(dimension_semantics=("arbitrary",),
                                             vmem_limit_bytes=VMEM_LIMIT),
        name="cmp_prompt",
    )(rows, rows, pek, pev, wk, wv, w2k, w2v)


WIN_KEYS = WINDOW + Q_BLOCK
CMP_PREFIX = 256
HEAD_BLKS = LANE // SEL_BLOCK
AUG_HEAD_PEN = AUG_PEN + BLK_PER_CHUNK
AUG_ORIGIN = AUG_HEAD_PEN + HEAD_BLKS
OWN_STEP = 256


def _sel_aug_rows():
    k = np.arange(SEL_CHUNK)
    c = np.zeros((HEAD_DIM, SEL_CHUNK), np.float32)
    c[AUG_HI - HEAD_DIM] = k // POS_SPLIT
    c[AUG_LO - HEAD_DIM] = k % POS_SPLIT
    for b in range(BLK_PER_CHUNK):
        c[AUG_PEN - HEAD_DIM + b] = (k // SEL_BLOCK == b)
    kh = np.arange(HEAD_BLKS * SEL_BLOCK)
    h = np.zeros((HEAD_DIM, kh.size), np.float32)
    h[AUG_HI - HEAD_DIM] = kh // POS_SPLIT
    h[AUG_LO - HEAD_DIM] = kh % POS_SPLIT
    for b in range(HEAD_BLKS):
        h[AUG_HEAD_PEN - HEAD_DIM + b] = (kh // SEL_BLOCK == b)
    h[AUG_ORIGIN - HEAD_DIM] = 1.0
    return jnp.asarray(c, dtype=BF16), jnp.asarray(h, dtype=BF16)


def _attn_prompt_kernel(q_ref, gt_ref, kst_ref, vs_ref, kwt_ref, vw_ref, kct_ref, vc_ref,
                        msel_ref, cst_ref, csth_ref, nrm_ref, o_ref, act_ref, m_s, l_s, acc_s, oc_s,
                        sc_s):
    i = pl.program_id(0)
    t0 = i * Q_BLOCK
    n_cmp = kct_ref.shape[1]
    n_sel = msel_ref.shape[1]
    rows = GQA_R * Q_BLOCK
    dot = functools.partial(jnp.dot, preferred_element_type=F32)
    qpos = t0 + lax.broadcasted_iota(jnp.int32, (Q_BLOCK, 1), 0)
    qpos4 = jnp.concatenate([qpos] * GQA_R, axis=0)
    qposf4 = qpos4.astype(F32)
    cend = lax.broadcasted_iota(jnp.int32, (1, n_cmp), 1) * CMP_STRIDE + (CMP_LEN - 1)
    blk = lax.broadcasted_iota(jnp.int32, (1, n_sel), 1)
    blkf = blk.astype(F32)
    gates = _sigmoid(gt_ref[...])
    cur = qpos >> 6
    forced = (blk == 0) | (blk == cur) | (blk == cur - 1)
    in_past = blk * SEL_BLOCK <= qpos
    n_chunks = (t0 + Q_BLOCK + SEL_CHUNK - 1) // SEL_CHUNK
    win0 = pl.multiple_of(jnp.maximum(t0 - WINDOW, 0), LANE)
    wpos = win0 + lax.broadcasted_iota(jnp.int32, (1, WIN_KEYS), 1)
    wdist = qpos4 - wpos
    wmask = (wdist >= 0) & (wdist < WINDOW)
    wdistf = wdist.astype(F32)

    qa, q64, slope4 = [], [], []
    for g in range(N_KV):
        qg = jnp.concatenate([q_ref[:, (g * GQA_R + r) * Q_SLOT:(g * GQA_R + r + 1) * Q_SLOT]
                              for r in range(GQA_R)], axis=0)
        qa.append(qg)
        q64.append(qg[:, 0:HEAD_DIM])
        slope4.append(jnp.concatenate(
            [jnp.full((Q_BLOCK, 1), np.float32(2.0 ** -(g * GQA_R + r + 1)), F32)
             for r in range(GQA_R)], axis=0))

    def cmp_branch(n_used):
        cend_u = cend[:, 0:n_used]
        dist = qposf4 - cend_u.astype(F32)
        for g in range(N_KV):
            gs = slice(g * HEAD_DIM, (g + 1) * HEAD_DIM)
            s = dot(q64[g], kct_ref[gs, 0:n_used])
            p = _masked_softmax(s - slope4[g] * dist, cend_u <= qpos4)
            oc_s[g * rows:(g + 1) * rows, :] = dot(p.astype(BF16), vc_ref[0:n_used, :])
            psum = p[0:Q_BLOCK]
            for r in range(1, GQA_R):
                psum = psum + p[r * Q_BLOCK:(r + 1) * Q_BLOCK]
            p_slc = _split3_dot(psum, msel_ref[0:n_used, :])
            sc_s[g * Q_BLOCK:(g + 1) * Q_BLOCK, :] = jnp.where(
                in_past, jnp.where(forced, FORCE_SCORE, p_slc), -1.0)

    n_prefix = max(1, n_cmp // CMP_PREFIX)
    per = n_cmp // n_prefix
    need = (t0 + Q_BLOCK - CMP_LEN) // CMP_STRIDE + 1
    bucket = jnp.clip((need + per - 1) // per - 1, 0, n_prefix - 1)
    for k in range(n_prefix):
        @pl.when(bucket == k)
        def _(k=k):
            cmp_branch((k + 1) * per)

    o_c = [oc_s[g * rows:(g + 1) * rows, g * HEAD_DIM:(g + 1) * HEAD_DIM] for g in range(N_KV)]

    o_w = []
    for g in range(N_KV):
        gs = slice(g * HEAD_DIM, (g + 1) * HEAD_DIM)
        sw = dot(q64[g], kwt_ref[gs, pl.ds(win0, WIN_KEYS)])
        pw = _masked_softmax(sw - slope4[g] * wdistf, wmask)
        o_w.append(dot(pw.astype(BF16), vw_ref[pl.ds(win0, WIN_KEYS), :])[:, gs])

    sel = _topk_member(sc_s[...], blkf, min(TOP_N, n_sel))
    pen_all = jnp.where(sel > 0.5, 0.0, NEG)
    head_lane = blk < HEAD_BLKS
    pen_head = pen_all.astype(BF16)
    pen = jnp.where(head_lane, NEG, pen_all).astype(BF16)
    penh = [pen_head[g * Q_BLOCK:(g + 1) * Q_BLOCK] for g in range(N_KV)]
    penb = [pen[g * Q_BLOCK:(g + 1) * Q_BLOCK] for g in range(N_KV)]
    sel = jnp.where(head_lane, 0.0, sel)
    n_chunks_all = n_sel // BLK_PER_CHUNK
    for g in range(N_KV):
        col_any = jnp.max(sel[g * Q_BLOCK:(g + 1) * Q_BLOCK], axis=0, keepdims=True)
        for c in range(n_chunks_all):
            hit = jnp.max(col_any[:, c * BLK_PER_CHUNK:(c + 1) * BLK_PER_CHUNK])
            act_ref[g * n_chunks_all + c] = (hit > 0.5).astype(jnp.int32)

    cst = cst_ref[...]
    oh_lane = lax.broadcasted_iota(jnp.int32, (n_sel, LANE), 1)
    oh_base = lax.broadcasted_iota(jnp.int32, (n_sel, LANE), 0) - oh_lane + AUG_PEN
    oh_ok = (oh_lane >= AUG_PEN) & (oh_lane < AUG_PEN + BLK_PER_CHUNK)

    def placed_penalty(c, g):
        onehot = jnp.where((oh_base == c * BLK_PER_CHUNK) & oh_ok, 1.0, 0.0).astype(BF16)
        return dot(penb[g], onehot).astype(BF16)

    def chunk_scores(c, g, causal, width):
        k0 = pl.multiple_of(c * SEL_CHUNK, SEL_CHUNK)
        placed = placed_penalty(c, g)
        qaug = qa[g] + jnp.concatenate([placed] * GQA_R, axis=0)
        kta = jnp.concatenate(
            [kst_ref[g * HEAD_DIM:(g + 1) * HEAD_DIM, pl.ds(k0, width)], cst[:, 0:width]], axis=0)
        s = dot(qaug, kta)
        if causal:
            kpos = k0 + lax.broadcasted_iota(jnp.int32, (1, width), 1)
            tri = jnp.where(kpos > qpos, NEG, 0.0)
            s = s + jnp.concatenate([tri] * GQA_R, axis=0)
        return s

    def chunk_update(c, g, s):
        k0 = pl.multiple_of(c * SEL_CHUNK, SEL_CHUNK)
        crow = slope4[g] * (k0.astype(F32) - qposf4)
        rg = slice(g * rows, (g + 1) * rows)
        m = m_s[rg, :]
        m_new = jnp.maximum(m, jnp.max(s, axis=-1, keepdims=True) + crow)
        p = jnp.exp(s + (crow - m_new))
        alpha = jnp.exp(m - m_new)
        return (m_new, alpha * l_s[rg, :] + jnp.sum(p, axis=-1, keepdims=True),
                alpha * acc_s[rg, :] + dot(p.astype(BF16), vs_ref[pl.ds(k0, s.shape[1]), :]))

    def fold(c, groups, causal, width=SEL_CHUNK):
        scores = [chunk_scores(c, g, causal, width) for g in groups]
        new = [chunk_update(c, g, s) for g, s in zip(groups, scores)]
        rg = slice(groups[0] * rows, (groups[-1] + 1) * rows)
        for ref, k in ((m_s, 0), (l_s, 1), (acc_s, 2)):
            ref[rg, :] = jnp.concatenate([n[k] for n in new], axis=0)

    head_keys = HEAD_BLKS * SEL_BLOCK
    head_oh = jnp.where((oh_base == -BLK_PER_CHUNK) & (oh_lane >= AUG_HEAD_PEN)
                        & (oh_lane < AUG_HEAD_PEN + HEAD_BLKS), 1.0, 0.0).astype(BF16)
    csth = csth_ref[...]
    own_c = n_chunks - 1
    own_k0 = pl.multiple_of(own_c * SEL_CHUNK, SEL_CHUNK)
    own_k0f = own_k0.astype(F32)
    lane_q = lax.broadcasted_iota(jnp.int32, (rows, LANE), 1)

    def own_pass(width):
        kpos = jnp.concatenate([lax.broadcasted_iota(jnp.int32, (1, head_keys), 1),
                                own_k0 + lax.broadcasted_iota(jnp.int32, (1, width), 1)], axis=1)
        tri = jnp.concatenate([jnp.where(kpos > qpos, NEG, 0.0)] * GQA_R, axis=0)
        new = []
        for g in range(N_KV):
            gr = slice(g * HEAD_DIM, (g + 1) * HEAD_DIM)
            placed = placed_penalty(own_c, g) + dot(penh[g], head_oh).astype(BF16)
            origin = jnp.where(lane_q == AUG_ORIGIN, -slope4[g] * own_k0f, 0.0).astype(BF16)
            qaug = qa[g] + jnp.concatenate([placed] * GQA_R, axis=0) + origin
            kta = jnp.concatenate(
                [jnp.concatenate([kst_ref[gr, 0:head_keys], csth], axis=0),
                 jnp.concatenate([kst_ref[gr, pl.ds(own_k0, width)], cst[:, 0:width]], axis=0)],
                axis=1)
            s = dot(qaug, kta) + tri
            crow = slope4[g] * (own_k0f - qposf4)
            m0 = jnp.max(s, axis=-1, keepdims=True) + crow
            p = jnp.exp(s + (crow - m0))
            vcat = jnp.concatenate([vs_ref[0:head_keys, :], vs_ref[pl.ds(own_k0, width), :]], axis=0)
            new.append((m0, jnp.sum(p, axis=-1, keepdims=True), dot(p.astype(BF16), vcat)))
        for ref, k in ((m_s, 0), (l_s, 1), (acc_s, 2)):
            ref[...] = jnp.concatenate([n[k] for n in new], axis=0)

    own_keys = t0 + Q_BLOCK - own_k0
    for width in range(OWN_STEP, SEL_CHUNK + 1, OWN_STEP):
        @pl.when((own_keys > width - OWN_STEP) & (own_keys <= width))
        def _(width=width):
            own_pass(width)

    def past_chunk(c, carry):
        a0 = act_ref[c] > 0
        a1 = act_ref[n_chunks_all + c] > 0

        @pl.when(a0 & a1)
        def _():
            fold(c, (0, 1), causal=False)

        @pl.when(a0 & jnp.logical_not(a1))
        def _():
            fold(c, (0,), causal=False)

        @pl.when(a1 & jnp.logical_not(a0))
        def _():
            fold(c, (1,), causal=False)

        return carry

    lax.fori_loop(0, n_chunks - 1, past_chunk, 0)

    head_out = []
    for g in range(N_KV):
        gs = slice(g * HEAD_DIM, (g + 1) * HEAD_DIM)
        rg = slice(g * rows, (g + 1) * rows)
        o_s = (acc_s[rg, :] * (1.0 / jnp.maximum(l_s[rg, :], 1e-30)))[:, gs]
        for r in range(GQA_R):
            h = g * GQA_R + r
            rs = slice(r * Q_BLOCK, (r + 1) * Q_BLOCK)
            head_out.append(o_c[g][rs] * gates[:, 3 * h:3 * h + 1]
                            + o_s[rs] * gates[:, 3 * h + 1:3 * h + 2]
                            + o_w[g][rs] * gates[:, 3 * h + 2:3 * h + 3])
    y = jnp.concatenate(head_out, axis=1)
    y = (y * lax.rsqrt(jnp.mean(y * y, axis=-1, keepdims=True) + 1e-6)) * nrm_ref[...]
    o_ref[...] = y.astype(BF16)


def _attn_prompt(q, gates, kst, vs, kwt, vw, kct, vc, msel, nrm):
    t = q.shape[0]
    cst, csth = _sel_aug_rows()
    full = lambda a: pl.BlockSpec(a.shape, lambda i: (0, 0), pipeline_mode=pl.Buffered(1))
    return pl.pallas_call(
        _attn_prompt_kernel,
        grid=(t // Q_BLOCK,),
        in_specs=[pl.BlockSpec((Q_BLOCK, Q_PAD), lambda i: (i, 0)),
                  pl.BlockSpec((Q_BLOCK, GATE_PAD), lambda i: (i, 0)),
                  full(kst), full(vs), full(kwt), full(vw), full(kct), full(vc), full(msel),
                  full(cst), full(csth), full(nrm)],
        out_specs=pl.BlockSpec((Q_BLOCK, D_Q), lambda i: (i, 0)),
        out_shape=jax.ShapeDtypeStruct((t, D_Q), BF16),
        scratch_shapes=[pltpu.SMEM((N_KV * (t // SEL_CHUNK),), jnp.int32),
                        pltpu.VMEM((N_KV * GQA_R * Q_BLOCK, 1), F32),
                        pltpu.VMEM((N_KV * GQA_R * Q_BLOCK, 1), F32),
                        pltpu.VMEM((N_KV * GQA_R * Q_BLOCK, D_KV), F32),
                        pltpu.VMEM((N_KV * GQA_R * Q_BLOCK, D_KV), F32),
                        pltpu.VMEM((N_KV * Q_BLOCK, t // SEL_BLOCK), F32)],
        compiler_params=pltpu.CompilerParams(dimension_semantics=("arbitrary",),
                                             vmem_limit_bytes=VMEM_LIMIT),
        name="attn_prompt",
    )(q, gates, kst, vs, kwt, vw, kct, vc, msel, cst, csth, nrm)


FF_CHUNK = D_FF // 2


def _layer_norm(x, g, b):
    mu = jnp.mean(x, axis=-1, keepdims=True)
    xc = x - mu
    var = jnp.mean(xc * xc, axis=-1, keepdims=True)
    return (xc * lax.rsqrt(var + 1e-5)) * g + b


def _ffn_kernel(x_ref, hr_ref, ha_ref, g1_ref, sh2_ref, sc2_ref, g2_ref, wo_ref, l1g_ref, l1b_ref,
                wup_ref, wdn_ref, l2g_ref, l2b_ref, o_ref, *, alpha):
    d = functools.partial(jnp.dot, preferred_element_type=F32)
    mix = d(hr_ref[...], wo_ref[0:D_RNN, :]) + d(ha_ref[...], wo_ref[D_RNN:D_MODEL, :])
    x1 = _layer_norm(alpha * x_ref[...] + g1_ref[...] * mix, l1g_ref[...], l1b_ref[...])
    u = (x1 * (1.0 + sc2_ref[...]) + sh2_ref[...]).astype(BF16)
    f = None
    for c in range(0, D_FF, FF_CHUNK):
        gate = d(u, wup_ref[:, c:c + FF_CHUNK])
        up = d(u, wup_ref[:, D_FF + c:D_FF + c + FF_CHUNK])
        part = d((gate * _sigmoid(gate) * up).astype(BF16), wdn_ref[c:c + FF_CHUNK, :])
        f = part if f is None else f + part
    o_ref[...] = _layer_norm(alpha * x1 + g2_ref[...] * f, l2g_ref[...], l2b_ref[...])


def _merge_ffn(x2d, hr, ha, g1, sh2, sc2, g2, wo, l1g, l1b, wup, wdn, l2g, l2b, tm, alpha):
    r = x2d.shape[0]
    rm = g1.shape[0]
    mod_spec = (pl.BlockSpec((1, D_MODEL), lambda i: (0, 0)) if rm == 1
                else pl.BlockSpec((tm, D_MODEL), lambda i: (i, 0)))
    vec = pl.BlockSpec((1, D_MODEL), lambda i: (0, 0))
    full = lambda a: pl.BlockSpec(a.shape, lambda i: (0, 0), pipeline_mode=pl.Buffered(1))
    return pl.pallas_call(
        functools.partial(_ffn_kernel, alpha=alpha),
        grid=(r // tm,),
        in_specs=[pl.BlockSpec((tm, D_MODEL), lambda i: (i, 0)),
                  pl.BlockSpec((tm, D_RNN), lambda i: (i, 0)),
                  pl.BlockSpec((tm, D_Q), lambda i: (i, 0)),
                  mod_spec, mod_spec, mod_spec, mod_spec,
                  full(wo), vec, vec, full(wup), full(wdn), vec, vec],
        out_specs=pl.BlockSpec((tm, D_MODEL), lambda i: (i, 0)),
        out_shape=jax.ShapeDtypeStruct((r, D_MODEL), F32),
        compiler_params=pltpu.CompilerParams(dimension_semantics=("arbitrary",),
                                             vmem_limit_bytes=VMEM_LIMIT),
        name="merge_ffn",
    )(x2d, hr, ha, g1, sh2, sc2, g2, wo, l1g, l1b, wup, wdn, l2g, l2b)


CACHE_ROWS_PER_PAGE = 4 * N_KV * HEAD_DIM
CMP_ROWS_PER_PAGE = 2 * N_KV * HEAD_DIM


def _sample_cmp_kernel(pt_ref, cache_ref, q_ref, new_ref, pek_ref, pev_ref, wk_ref, wv_ref,
                       w2k_ref, w2v_ref, msel_ref, oc_ref, pslc_ref, buf, x_s, kc_s, vc_s, sem,
                       *, n_pages):
    b = pl.program_id(0)
    nb = pl.num_programs(0)
    past_len = n_pages * PAGE_SIZE
    n_cmp = past_len // CMP_STRIDE

    def page_copy(bb, p, slot):
        pool = pt_ref[bb * n_pages + p]
        return pltpu.make_async_copy(
            cache_ref.at[pl.ds(pool * CACHE_ROWS_PER_PAGE, CMP_ROWS_PER_PAGE), :],
            buf.at[slot, p], sem.at[slot])

    def start_all(bb, slot):
        for p in range(n_pages):
            page_copy(bb, p, slot).start(priority=p % 2)

    @pl.when(b == 0)
    def _():
        start_all(0, 0)

    slot = b % 2

    @pl.when(b + 1 < nb)
    def _():
        start_all(b + 1, 1 - slot)

    for p in range(n_pages):
        page_copy(b, p, slot).wait()

    groups_per_page = PAGE_SIZE // CMP_STRIDE

    def to_rows(p):
        r0 = p * (groups_per_page * CMP_PITCH)
        for t in range(2):
            tile = buf[slot, p, t * D_KV:(t + 1) * D_KV, :].T
            for j in range(groups_per_page):
                x_s[t, r0 + CMP_PITCH * j:r0 + CMP_PITCH * j + CMP_STRIDE, :] = (
                    tile[CMP_STRIDE * j:CMP_STRIDE * (j + 1)])

    def compress(blk0, nblk):
        kc, vc = _compress_chunk(x_s, blk0, nblk, pek_ref, pev_ref, wk_ref, wv_ref, w2k_ref,
                                 w2v_ref)
        kc_s[blk0:blk0 + nblk, :] = kc
        vc_s[blk0:blk0 + nblk, :] = vc

    part_pages = n_pages // CMP_PARTS
    done = 0
    for k in range(CMP_PARTS):
        for p in range(k * part_pages, (k + 1) * part_pages):
            to_rows(p)
        if k == CMP_PARTS - 1:
            tail_row = lax.broadcasted_iota(jnp.int32, (CMP_STRIDE, D_KV), 0)
            for t in range(2):
                x_s[t, CMP_PITCH * n_cmp:CMP_PITCH * n_cmp + CMP_STRIDE, :] = jnp.where(
                    tail_row == 0, new_ref[0][:, t * D_KV:(t + 1) * D_KV], 0.0)
            upto = n_cmp
        else:
            upto = (k + 1) * part_pages * groups_per_page - 16
        compress(done, upto - done)
        done = upto

    q = q_ref[0]
    kcb = kc_s[...].astype(BF16)
    vcb = vc_s[...].astype(BF16)
    row = lax.broadcasted_iota(jnp.int32, (N_HEADS, 1), 0)
    first = row < GQA_R
    nt = (((1,), (1,)), ((), ()))
    s0 = lax.dot_general(q, kcb[:, 0:HEAD_DIM], nt, preferred_element_type=F32)
    s1 = lax.dot_general(q, kcb[:, HEAD_DIM:D_KV], nt, preferred_element_type=F32)
    cend = lax.broadcasted_iota(jnp.int32, (1, n_cmp), 1) * CMP_STRIDE + (CMP_LEN - 1)
    s = jnp.where(first, s0, s1) - _pow2_neg(row + 1) * (past_len - cend).astype(F32)
    p = _masked_softmax(s, cend <= past_len)
    o = jnp.dot(p.astype(BF16), vcb, preferred_element_type=F32)
    oc_ref[0] = jnp.where(first, o[:, 0:HEAD_DIM], o[:, HEAD_DIM:D_KV])
    psum = jnp.concatenate([jnp.sum(p[0:GQA_R], axis=0, keepdims=True),
                            jnp.sum(p[GQA_R:N_HEADS], axis=0, keepdims=True)], axis=0)
    pslc_ref[0] = _split3_dot(psum, msel_ref[...])


def _sample_cmp(pt_flat, cache2d, q3, new3, pek, pev, wk, wv, w2k, w2v, msel, n_pages):
    b = q3.shape[0]
    past_len = n_pages * PAGE_SIZE
    n_cmp = past_len // CMP_STRIDE
    full = lambda a: pl.BlockSpec(a.shape, lambda i, pt: (0,) * a.ndim)
    return pl.pallas_call(
        functools.partial(_sample_cmp_kernel, n_pages=n_pages),
        grid_spec=pltpu.PrefetchScalarGridSpec(
            num_scalar_prefetch=1, grid=(b,),
            in_specs=[pl.BlockSpec(memory_space=pl.ANY),
                      pl.BlockSpec((1, N_HEADS, HEAD_DIM), lambda i, pt: (i, 0, 0)),
                      pl.BlockSpec((1, 1, 6 * D_KV), lambda i, pt: (i, 0, 0)),
                      full(pek), full(pev), full(wk), full(wv), full(w2k), full(w2v), full(msel)],
            out_specs=[pl.BlockSpec((1, N_HEADS, HEAD_DIM), lambda i, pt: (i, 0, 0)),
                       pl.BlockSpec((1, N_KV, msel.shape[1]), lambda i, pt: (i, 0, 0))],
            scratch_shapes=[pltpu.VMEM((2, n_pages, CMP_ROWS_PER_PAGE, PAGE_SIZE), F32),
                            pltpu.VMEM((2, _pitch_rows(n_cmp + 1), D_KV), F32),
                            pltpu.VMEM((n_cmp, D_KV), F32), pltpu.VMEM((n_cmp, D_KV), F32),
                            pltpu.SemaphoreType.DMA((2,))]),
        out_shape=[jax.ShapeDtypeStruct((b, N_HEADS, HEAD_DIM), F32),
                   jax.ShapeDtypeStruct((b, N_KV, msel.shape[1]), F32)],
        compiler_params=pltpu.CompilerParams(dimension_semantics=("arbitrary",),
                                             vmem_limit_bytes=VMEM_LIMIT),
        name="sample_cmp",
    )(pt_flat, cache2d, q3, new3, pek, pev, wk, wv, w2k, w2v, msel)


META_W = 128


def _sample_topk_kernel(pslc_ref, meta_ref, *, past_len):
    n_lane = pslc_ref.shape[1]
    n_sel = -(-(past_len + 1) // SEL_BLOCK)
    blk = lax.broadcasted_iota(jnp.int32, (1, n_lane), 1)
    cur = past_len // SEL_BLOCK
    forced = (blk == 0) | (blk == cur) | (blk == cur - 1)
    score = jnp.where(blk * SEL_BLOCK <= past_len,
                      jnp.where(forced, FORCE_SCORE, pslc_ref[...]), -1.0)
    score = jnp.where(blk < n_sel, score, -3.0)
    idxs, vals = _topk_select(score, blk.astype(F32), min(TOP_N, n_sel))
    lane = lax.broadcasted_iota(jnp.int32, (pslc_ref.shape[0], META_W), 1)
    meta = jnp.zeros((pslc_ref.shape[0], META_W), F32)
    for it, (idx, val) in enumerate(zip(idxs, vals)):
        meta = jnp.where(lane == it, idx, meta)
        meta = jnp.where(lane == TOP_N + it, jnp.where(val >= 0.0, 1.0, 0.0), meta)
    meta_ref[...] = meta


def _sample_topk(pslc2d, past_len):
    return pl.pallas_call(
        functools.partial(_sample_topk_kernel, past_len=past_len),
        out_shape=jax.ShapeDtypeStruct((pslc2d.shape[0], META_W), F32),
        name="sample_topk",
    )(pslc2d)


SEL_KEYS = TOP_N * PAGE_SIZE


def _sample_attn_kernel(koff_ref, voff_ref, cache_ref, q_ref, meta_ref, new_ref, newt_ref, win_ref,
                        oc_ref, gt_ref, nrm_ref, e16_ref, ha_ref, wout_ref, kbuf, vbuf, sem,
                        *, past_len):
    b = pl.program_id(0)
    nb = pl.num_programs(0)

    def tile_copies(bb, slot):
        cps = []
        for g in range(N_KV):
            for n in range(TOP_N):
                i = (bb * N_KV + g) * TOP_N + n
                dst = pl.ds(n * PAGE_SIZE, PAGE_SIZE)
                cps.append(pltpu.make_async_copy(cache_ref.at[pl.ds(koff_ref[i], HEAD_DIM), :],
                                                 kbuf.at[slot, g, :, dst], sem.at[slot]))
                cps.append(pltpu.make_async_copy(cache_ref.at[pl.ds(voff_ref[i], HEAD_DIM), :],
                                                 vbuf.at[slot, g, :, dst], sem.at[slot]))
        return cps

    @pl.when(b == 0)
    def _():
        for n, cp in enumerate(tile_copies(0, 0)):
            cp.start(priority=n % 2)

    slot = b % 2

    @pl.when(b + 1 < nb)
    def _():
        for n, cp in enumerate(tile_copies(b + 1, 1 - slot)):
            cp.start(priority=n % 2)

    for cp in tile_copies(b, slot):
        cp.wait()

    q = q_ref[0]
    qf = q.astype(F32)
    row = lax.broadcasted_iota(jnp.int32, (N_HEADS, 1), 0)
    first = row < GQA_R
    slope = _pow2_neg(row + 1)
    new = new_ref[0]
    nt = (((1,), (1,)), ((), ()))

    def new_rows(off):
        a = new[:, off:off + HEAD_DIM]
        c = new[:, off + HEAD_DIM:off + D_KV]
        v = jnp.where(first, jnp.broadcast_to(a, (N_HEADS, HEAD_DIM)),
                      jnp.broadcast_to(c, (N_HEADS, HEAD_DIM)))
        return v.astype(BF16).astype(F32)

    def attend(s_buf, mask_buf, v_of_p, s_new, new_on, v_new):
        s_buf = jnp.where(mask_buf, s_buf, NEG)
        s_new = jnp.where(new_on, s_new, NEG)
        m = jnp.maximum(jnp.max(s_buf, axis=-1, keepdims=True), s_new)
        p = jnp.where(mask_buf, jnp.exp(s_buf - m), 0.0)
        p_new = jnp.where(new_on, jnp.exp(s_new - m), 0.0)
        den = jnp.maximum(jnp.sum(p, axis=-1, keepdims=True) + p_new, 1e-30)
        p = p / den
        p_new = (p_new / den).astype(BF16).astype(F32)
        return v_of_p(p.astype(BF16)) + p_new * v_new

    meta = meta_ref[0]
    e16 = e16_ref[...]
    jexp = jnp.dot(meta[:, 0:TOP_N].astype(BF16), e16, preferred_element_type=F32)
    vexp = jnp.dot(meta[:, TOP_N:2 * TOP_N].astype(BF16), e16, preferred_element_type=F32)
    lane = lax.broadcasted_iota(jnp.int32, (1, SEL_KEYS), 1) & (PAGE_SIZE - 1)
    ji = jexp.astype(jnp.int32)
    kpos = (ji >> 1) * PAGE_SIZE + lane
    key_ok = (vexp > 0.5) & ((kpos >> 6) == ji) & (kpos < past_len)
    new_blk = past_len // SEL_BLOCK
    new_sel = jnp.max(jnp.where((meta[:, 0:TOP_N] == float(new_blk)) & (meta[:, TOP_N:2 * TOP_N] > 0.5),
                                1.0, 0.0), axis=-1, keepdims=True)
    o_sel = []
    for g in range(N_KV):
        sb = jnp.dot(q, kbuf[slot, g].astype(BF16), preferred_element_type=F32)
        sb = sb - slope * (past_len - kpos[g:g + 1]).astype(F32)
        vb = vbuf[slot, g].astype(BF16)
        o_sel.append((sb, key_ok[g:g + 1], vb))
    k_new = new_rows(2 * D_KV)
    v_new = new_rows(3 * D_KV)
    s_new = jnp.sum(qf * k_new, axis=-1, keepdims=True)
    new_on = jnp.where(first, new_sel[0:1], new_sel[1:2]) > 0.5
    outs = [attend(sb, ok, lambda pb, vb=vb: lax.dot_general(pb, vb, nt, preferred_element_type=F32),
                   s_new, new_on, v_new) for sb, ok, vb in o_sel]
    o_s = jnp.where(first, outs[0], outs[1])

    w = win_ref[0]
    wbuf = w.shape[1]
    lane_b = lax.broadcasted_iota(jnp.int32, (4 * HEAD_DIM, newt_ref.shape[1]), 1)
    new_col = jnp.sum(jnp.where(lane_b == b, newt_ref[...], 0.0), axis=-1, keepdims=True)
    lane_w = lax.broadcasted_iota(jnp.int32, (1, wbuf), 1)
    w_new = jnp.where(lane_w == wbuf - 1, new_col, pltpu.roll(w, wbuf - 1, 1))
    wout_ref[0] = w_new
    wdist = (wbuf - 1 - lane_w).astype(F32)
    wb = w_new.astype(BF16)
    ow = []
    for g in range(N_KV):
        sw = jnp.dot(q, wb[g * HEAD_DIM:(g + 1) * HEAD_DIM], preferred_element_type=F32)
        pw = _masked_softmax(sw - slope * wdist, lane_w >= 0)
        ow.append(lax.dot_general(pw.astype(BF16), wb[D_KV + g * HEAD_DIM:D_KV + (g + 1) * HEAD_DIM],
                                  nt, preferred_element_type=F32))
    o_w = jnp.where(first, ow[0], ow[1])

    gates = _sigmoid(gt_ref[0])
    y = oc_ref[0] * gates[:, 0:1] + o_s * gates[:, 1:2] + o_w * gates[:, 2:3]
    ms = jnp.sum(jnp.sum(y * y, axis=-1, keepdims=True), axis=0, keepdims=True) / D_Q
    ha_ref[0] = (y * lax.rsqrt(ms + 1e-6)) * nrm_ref[...]


def _sample_attn(koff, voff, cache2d, q3, meta3, new3, newt, win3, oc3, gt3, nrm8, e16, past_len):
    b = q3.shape[0]
    wbuf = win3.shape[2]
    full = lambda a: pl.BlockSpec(a.shape, lambda i, ko, vo: (0,) * a.ndim)
    per_b = lambda a: pl.BlockSpec((1,) + a.shape[1:], lambda i, ko, vo: (i,) + (0,) * (a.ndim - 1))
    return pl.pallas_call(
        functools.partial(_sample_attn_kernel, past_len=past_len),
        grid_spec=pltpu.PrefetchScalarGridSpec(
            num_scalar_prefetch=2, grid=(b,),
            in_specs=[pl.BlockSpec(memory_space=pl.ANY), per_b(q3), per_b(meta3), per_b(new3),
                      full(newt), per_b(win3), per_b(oc3), per_b(gt3), full(nrm8), full(e16)],
            out_specs=[pl.BlockSpec((1, N_HEADS, HEAD_DIM), lambda i, ko, vo: (i, 0, 0)),
                       pl.BlockSpec((1, 4 * HEAD_DIM, wbuf), lambda i, ko, vo: (i, 0, 0))],
            scratch_shapes=[pltpu.VMEM((2, N_KV, HEAD_DIM, SEL_KEYS), F32),
                            pltpu.VMEM((2, N_KV, HEAD_DIM, SEL_KEYS), F32),
                            pltpu.SemaphoreType.DMA((2,))]),
        out_shape=[jax.ShapeDtypeStruct((b, N_HEADS, HEAD_DIM), F32),
                   jax.ShapeDtypeStruct((b, 4 * HEAD_DIM, wbuf), F32)],
        compiler_params=pltpu.CompilerParams(dimension_semantics=("arbitrary",),
                                             vmem_limit_bytes=VMEM_LIMIT),
        name="sample_attn",
    )(koff, voff, cache2d, q3, meta3, new3, newt, win3, oc3, gt3, nrm8, e16)


def _block_diag(w):
    n, a, b = w.shape
    eye = jnp.eye(n, dtype=w.dtype)
    return (eye[:, None, :, None] * w[:, :, None, :]).reshape(n * a, n * b)


def _cmp_weights(pe, w1, w2):
    pe_cat = jnp.tile(pe, (1, N_KV)).reshape(1, CMP_LEN * D_KV)
    eye = jnp.eye(N_KV, dtype=w1.dtype)
    w1_big = (w1[:, None, :, None, :] * eye[None, :, None, :, None]).reshape(
        CMP_LEN * D_KV, N_KV * CMP_HID)
    w2_big = (w2[None, :, None, :] * eye[:, None, :, None]).reshape(N_KV * CMP_HID, D_KV)
    return pe_cat, w1_big.astype(BF16), w2_big.astype(BF16)


def _sel_matrix(n_cmp, n_lane):
    n = np.arange(n_cmp)[:, None]
    j = np.arange(n_lane)[None, :]
    lo = SEL_RATIO * j - (CMP_LEN // CMP_STRIDE - 1)
    return jnp.asarray(((n >= lo) & (n <= lo + SEL_RATIO)).astype(np.float32), dtype=BF16)


def kernel(x_prompt, x_sample, cache_kv, state_win, state_conv, state_h, page_table, c_prompt, c_sample, w_ada, b_ada, w_in, conv_w, conv_b, rg_wa, rg_ba, rg_wx, rg_bx, rg_lam, cmp_pe_k, cmp_w1_k, cmp_w2_k, cmp_pe_v, cmp_w1_v, cmp_w2_v, norm_rg, norm_attn, w_out, ln1_g, ln1_b, w_up, w_down, ln2_g, ln2_b):
    depth = w_in.shape[0]
    alpha = float((2.0 * depth) ** 0.25)
    bp, t, _ = x_prompt.shape
    assert bp == 1
    bs = x_sample.shape[0]
    assert x_sample.shape[1] == 1
    n_pages = page_table.shape[1]
    past_len = n_pages * PAGE_SIZE
    n_pool = cache_kv.shape[1]
    wbuf = state_win.shape[2]
    assert wbuf == WINDOW and t % CMP_ROWS == 0 and t % SEL_CHUNK == 0 and t >= WIN_KEYS
    assert n_pages % CMP_PARTS == 0 and (n_pages // CMP_PARTS) * PAGE_SIZE >= 2 * 16 * CMP_STRIDE

    xp = x_prompt.reshape(t, D_MODEL)
    xs = x_sample.reshape(bs, D_MODEL)
    pt_flat = page_table.reshape(-1)
    r_mod = -(-(bs + 1) // 8) * 8
    c_all = jnp.zeros((r_mod, D_MODEL), F32).at[0:bs].set(c_sample).at[bs:bs + 1].set(c_prompt)
    vec = lambda a: a.reshape(1, -1)

    outs = [[] for _ in range(8)]
    for l in range(depth):
        mod = _modulation(c_all, w_ada[l], b_ada[l])
        mod_s = [mod[0:bs, k * D_MODEL:(k + 1) * D_MODEL] for k in range(6)]
        mod_p = [mod[bs:bs + 1, k * D_MODEL:(k + 1) * D_MODEL] for k in range(6)]
        w_l = w_in[l]
        w_q = jnp.pad(w_l[:, 2 * D_RNN:2 * D_RNN + D_Q].reshape(D_MODEL, N_HEADS, HEAD_DIM),
                      ((0, 0), (0, 0), (0, Q_SLOT - HEAD_DIM))).reshape(D_MODEL, Q_PAD)
        w_in_b = jnp.concatenate(
            [w_l[:, 0:2 * D_RNN], w_q, w_l[:, 2 * D_RNN + D_Q:],
             jnp.zeros((D_MODEL, GATE_PAD - D_GATE), F32)], axis=1).astype(BF16)
        wrg = jnp.concatenate([_block_diag(rg_wa[l]), _block_diag(rg_wx[l])], axis=1).astype(BF16)
        pek, wk, w2k = _cmp_weights(cmp_pe_k[l], cmp_w1_k[l], cmp_w2_k[l])
        pev, wv, w2v = _cmp_weights(cmp_pe_v[l], cmp_w1_v[l], cmp_w2_v[l])
        wo_b, wup_b, wdn_b = w_out[l].astype(BF16), w_up[l].astype(BF16), w_down[l].astype(BF16)
        rg_args = (conv_w[l], vec(conv_b[l]), wrg, vec(rg_ba[l]), vec(rg_bx[l]), vec(rg_lam[l]),
                   vec(norm_rg[l]))
        ffn_w = (wo_b, vec(ln1_g[l]), vec(ln1_b[l]), wup_b, wdn_b, vec(ln2_g[l]), vec(ln2_b[l]))

        xg, q, rows, gts, vs, vw, kvt, kvwt, kst, kwt = _in_proj(xp, mod_p[0], mod_p[1], w_in_b, 512,
                                                                 2 * D_KV)
        hr, tail, hl = _rg_prompt(xg, *rg_args, tc=256)
        kct, vc = _cmp_prompt(rows, pek, pev, wk, wv, w2k, w2v)
        msel_p = _sel_matrix(t // CMP_STRIDE, t // SEL_BLOCK)
        ha = _attn_prompt(q, gts, kst, vs, kwt, vw, kct, vc, msel_p, vec(norm_attn[l]))
        xp = _merge_ffn(xp, hr, ha, mod_p[2], mod_p[3], mod_p[4], mod_p[5], *ffn_w,
                        tm=512, alpha=alpha)
        outs[0].append(kvt.reshape(4, N_KV, HEAD_DIM, t).transpose(3, 0, 1, 2)[None])
        outs[2].append(kvwt[:, t - WINDOW:].reshape(2, N_KV, HEAD_DIM, WINDOW)
                       .transpose(3, 0, 1, 2)[None])
        outs[4].append(tail[8 - (CONV_W - 1):][None])
        outs[6].append(hl[0:1])

        xg, q, rows, gts, _, _, kvt, kvwt, _, _ = _in_proj(xs, mod_s[0], mod_s[1], w_in_b, bs,
                                                           6 * D_KV)
        sconv = state_conv[l]
        hr, h_new = _rg_sample(xg, sconv[:, 0], sconv[:, 1], sconv[:, 2], state_h[l], *rg_args)
        cache2d = cache_kv[l].transpose(0, 2, 3, 4, 1).reshape(n_pool * CACHE_ROWS_PER_PAGE, PAGE_SIZE)
        n_lane = -(-(past_len // SEL_BLOCK + 1) // LANE) * LANE
        msel_s = _sel_matrix(past_len // CMP_STRIDE, n_lane)
        q3 = q.reshape(bs, N_HEADS, Q_SLOT)[:, :, 0:HEAD_DIM]
        new3 = rows.reshape(bs, 1, 6 * D_KV)
        oc3, pslc = _sample_cmp(pt_flat, cache2d, q3, new3, pek, pev, wk, wv, w2k, w2v, msel_s, n_pages)
        meta = _sample_topk(pslc.reshape(bs * N_KV, n_lane), past_len)
        picks = meta[:, 0:TOP_N].astype(jnp.int32).reshape(bs, N_KV * TOP_N)
        pool = jnp.take_along_axis(page_table, jnp.minimum(picks // 2, n_pages - 1), axis=1)
        grp = (jnp.arange(N_KV * TOP_N, dtype=jnp.int32) // TOP_N)[None, :]
        koff = (pool * CACHE_ROWS_PER_PAGE + (2 * N_KV + grp) * HEAD_DIM).reshape(-1)
        voff = koff + N_KV * HEAD_DIM
        win3 = state_win[l].transpose(0, 2, 3, 4, 1).reshape(bs, 4 * HEAD_DIM, wbuf)
        e16 = jnp.asarray(np.kron(np.eye(TOP_N, dtype=np.float32), np.ones((1, PAGE_SIZE), np.float32)),
                          dtype=BF16)
        ha3, wnew = _sample_attn(koff, voff, cache2d, q3, meta.reshape(bs, N_KV, META_W), new3,
                                 kvwt, win3, oc3, gts[:, 0:D_GATE].reshape(bs, N_HEADS, 3),
                                 norm_attn[l].reshape(N_HEADS, HEAD_DIM), e16, past_len)
        ha = ha3.reshape(bs, D_Q).astype(BF16)
        xs = _merge_ffn(xs, hr, ha, mod_s[2], mod_s[3], mod_s[4], mod_s[5], *ffn_w,
                        tm=bs, alpha=alpha)
        outs[1].append(kvt.reshape(4, N_KV, HEAD_DIM, bs).transpose(3, 0, 1, 2)[:, None])
        outs[3].append(wnew.reshape(bs, 2, N_KV, HEAD_DIM, wbuf).transpose(0, 4, 1, 2, 3))
        outs[5].append(jnp.stack([sconv[:, 1], sconv[:, 2], xg[:, 0:D_RNN]], axis=1))
        outs[7].append(h_new)

    stk = [jnp.stack(o) for o in outs]
    return (xp.reshape(1, t, D_MODEL), xs.reshape(bs, 1, D_MODEL), stk[0], stk[1], stk[2], stk[3],
            stk[4], stk[5], stk[6], stk[7])
```
